```python
import math
import jax, jax.numpy as jnp
from jax import lax
import numpy as np

D_MODEL = 2048
BATCH = 4
SEQ = 4096
DEPTH = 4

N_MIXERS = 4
MIX_WIDTH = D_MODEL
GROUP_WIDTH = MIX_WIDTH // N_MIXERS
HEAD_DIM = 128
N_GROUP_HEADS = GROUP_WIDTH // HEAD_DIM
Q_BLOCK = 128
NORM_EPS = 1e-6
GDN_DK = HEAD_DIM
GDN_DV = HEAD_DIM
GDN_CONV = 4
GDN_CHUNK = 64
DT_MIN = 1e-3
DT_MAX = 1e-1
DIFF_DQK = HEAD_DIM // 2
GLA_DK = HEAD_DIM // 2
GLA_DV = HEAD_DIM
GLA_GATE_RANK = 16
GLA_TAU = 16.0
GLA_CHUNK = 64
FFN_HIDDEN = ((8 * D_MODEL + 3 * 256 - 1) // (3 * 256)) * 256
IN_SPLITS = (
    GROUP_WIDTH, GROUP_WIDTH, GROUP_WIDTH,
    GROUP_WIDTH, GROUP_WIDTH, GROUP_WIDTH, GROUP_WIDTH, N_GROUP_HEADS, N_GROUP_HEADS,
    GROUP_WIDTH, GROUP_WIDTH, GROUP_WIDTH,
    N_GROUP_HEADS * GLA_DK, N_GROUP_HEADS * GLA_DK, GROUP_WIDTH, GROUP_WIDTH, GLA_GATE_RANK,
)
IN_COLS = sum(IN_SPLITS)

kernel_name = "hymba_style_sb_gdn_diff_gla_trunk"


def rms_norm(x, gain):
    xf = x.astype(jnp.float32)
    y = xf * lax.rsqrt(jnp.mean(jnp.square(xf), axis=-1, keepdims=True) + NORM_EPS)
    return (y * gain.astype(jnp.float32)).astype(x.dtype)


def l2norm(x):
    return x * lax.rsqrt(jnp.sum(jnp.square(x), axis=-1, keepdims=True) + NORM_EPS)


def causal_depthwise_conv(x, w):
    k = w.shape[0]
    return lax.conv_general_dilated(
        x, w[:, None, :], window_strides=(1,), padding=[(k - 1, 0)],
        dimension_numbers=("NWC", "WIO", "NWC"), feature_group_count=x.shape[-1])


def stick_breaking_attention(q, k, v):
    B, S, H, D = q.shape
    nb = S // Q_BLOCK
    qb = (q * D ** -0.5).reshape(B, nb, Q_BLOCK, H, D).transpose(1, 0, 3, 2, 4)
    kt = k.transpose(0, 2, 1, 3)
    vt = v.transpose(0, 2, 1, 3)
    kpos = jnp.arange(S)

    def block(args):
        qi, i = args
        qpos = i * Q_BLOCK + jnp.arange(Q_BLOCK)
        mask = kpos[None, :] < qpos[:, None]
        z = jnp.einsum('bhqd,bhkd->bhqk', qi, kt)
        log_fail = jnp.where(mask, -jax.nn.softplus(z), 0.0)
        tail = lax.cumsum(log_fail, axis=3, reverse=True) - log_fail
        att = jnp.where(mask, jnp.exp(jax.nn.log_sigmoid(z) + tail), 0.0)
        return jnp.einsum('bhqk,bhkd->bhqd', att, vt)

    o = lax.map(block, (qb, jnp.arange(nb)))
    return o.transpose(1, 0, 3, 2, 4).reshape(B, S, H, D)


def gated_delta_rule(q, k, v, g, beta):
    B, S, H, Dk = q.shape
    Dv = v.shape[-1]
    C = GDN_CHUNK
    N = S // C

    def chunks(t):
        return t.reshape(B, N, C, H, -1).transpose(1, 0, 3, 2, 4)

    qc, kc, vc = chunks(q), chunks(k), chunks(v)
    gc = jnp.cumsum(g.reshape(B, N, C, H).transpose(1, 0, 3, 2), axis=-1)
    bc = beta.reshape(B, N, C, H).transpose(1, 0, 3, 2)[..., None]
    idx = jnp.arange(C)
    incl = idx[:, None] >= idx[None, :]
    strict = idx[:, None] > idx[None, :]
    decay = jnp.exp(jnp.where(incl, gc[..., :, None] - gc[..., None, :], -jnp.inf))
    kb = kc * bc
    a_mat = jnp.where(strict, jnp.einsum('nbhcd,nbhsd->nbhcs', kb, kc) * decay, 0.0)
    m_mat = a_mat + jnp.eye(C, dtype=a_mat.dtype)
    u = lax.linalg.triangular_solve(m_mat, vc * bc, left_side=True, lower=True, unit_diagonal=True)
    w = lax.linalg.triangular_solve(m_mat, kb * jnp.exp(gc)[..., None], left_side=True, lower=True,
                                    unit_diagonal=True)
    qk = jnp.einsum('nbhcd,nbhsd->nbhcs', qc, kc) * decay

    def step(state, xs):
        q_n, k_n, u_n, w_n, qk_n, g_n = xs
        v_new = u_n - jnp.einsum('bhcd,bhde->bhce', w_n, state)
        o = (jnp.einsum('bhcd,bhde->bhce', q_n * jnp.exp(g_n)[..., None], state)
             + jnp.einsum('bhcs,bhse->bhce', qk_n, v_new))
        g_last = g_n[..., -1:]
        state = (state * jnp.exp(g_last)[..., None]
                 + jnp.einsum('bhcd,bhce->bhde', k_n * jnp.exp(g_last - g_n)[..., None], v_new))
        return state, o

    state0 = jnp.zeros((B, H, Dk, Dv), jnp.float32)
    _, o = lax.scan(step, state0, (qc, kc, u, w, qk, gc))
    return o.transpose(1, 0, 3, 2, 4).reshape(B, S, H, Dv)


def differential_attention(q, k, v, lam):
    B, S, H, _, D = q.shape
    nb = S // Q_BLOCK
    qb = (q * D ** -0.5).reshape(B, nb, Q_BLOCK, H, 2, D).transpose(1, 0, 3, 4, 2, 5)
    kt = k.transpose(0, 2, 3, 1, 4)
    vt = v.transpose(0, 2, 1, 3)
    kpos = jnp.arange(S)

    def block(args):
        qi, i = args
        qpos = i * Q_BLOCK + jnp.arange(Q_BLOCK)
        s = jnp.einsum('bhmqd,bhmkd->bhmqk', qi, kt)
        s = jnp.where(kpos[None, :] <= qpos[:, None], s, -jnp.inf)
        p = jax.nn.softmax(s, axis=-1)
        return jnp.einsum('bhqk,bhkd->bhqd', p[:, :, 0] - lam * p[:, :, 1], vt)

    o = lax.map(block, (qb, jnp.arange(nb)))
    return o.transpose(1, 0, 3, 2, 4).reshape(B, S, H, -1)


def gla_chunked(q, k, v, log_a):
    B, S, H, Dk = q.shape
    Dv = v.shape[-1]
    C = GLA_CHUNK
    N = S // C

    def chunks(t):
        return t.reshape(B, N, C, H, -1).transpose(1, 0, 3, 2, 4)

    qc, kc, vc = chunks(q), chunks(k), chunks(v)
    bc = jnp.cumsum(chunks(log_a), axis=3)
    idx = jnp.arange(C)
    incl = (idx[:, None] >= idx[None, :])[:, :, None]

    def step(state, xs):
        q_n, k_n, v_n, b_n = xs
        inter = jnp.einsum('bhcd,bhde->bhce', q_n * jnp.exp(b_n), state)
        rel = jnp.exp(jnp.where(incl, b_n[:, :, :, None, :] - b_n[:, :, None, :, :], -jnp.inf))
        att = jnp.einsum('bhtd,bhsd,bhtsd->bhts', q_n, k_n, rel)
        o = inter + jnp.einsum('bhts,bhse->bhte', att, v_n)
        b_last = b_n[:, :, -1:, :]
        state = (state * jnp.exp(b_last)[:, :, 0, :, None]
                 + jnp.einsum('bhcd,bhce->bhde', k_n * jnp.exp(b_last - b_n), v_n))
        return state, o

    state0 = jnp.zeros((B, H, Dk, Dv), jnp.float32)
    _, o = lax.scan(step, state0, (qc, kc, vc, bc))
    return o.transpose(1, 0, 3, 2, 4).reshape(B, S, H, Dv)


def hybrid_mixer(h, w_in, gdn_conv_w, gdn_a_log, gdn_dt_bias, gdn_out_norm, lam_q1, lam_k1, lam_q2, lam_k2,
                 diff_out_norm, gla_gate_w2, gla_gate_b, gla_out_norm, lam_init):
    f32 = jnp.float32
    B, S, _ = h.shape
    H = N_GROUP_HEADS
    proj = jnp.einsum('bsd,dc->bsc', h, w_in).astype(f32)
    points = np.cumsum(IN_SPLITS)[:-1].tolist()
    (sb_q, sb_k, sb_v, gd_q, gd_k, gd_v, gd_z, gd_b, gd_a,
     df_q, df_k, df_v, gl_q, gl_k, gl_v, gl_g, gl_r) = jnp.split(proj, points, axis=-1)

    def heads(t):
        return t.reshape(B, S, H, -1)

    o_sb = stick_breaking_attention(heads(sb_q), heads(sb_k), heads(sb_v))

    qkv = jax.nn.silu(causal_depthwise_conv(jnp.concatenate([gd_q, gd_k, gd_v], axis=-1), gdn_conv_w.astype(f32)))
    gd_q, gd_k, gd_v = jnp.split(qkv, 3, axis=-1)
    beta = jax.nn.sigmoid(gd_b)
    g = -jnp.exp(gdn_a_log.astype(f32)) * jax.nn.softplus(gd_a + gdn_dt_bias.astype(f32))
    o_gd = gated_delta_rule(l2norm(heads(gd_q)) * GDN_DK ** -0.5, l2norm(heads(gd_k)), heads(gd_v), g, beta)
    o_gd = rms_norm(o_gd, gdn_out_norm) * jax.nn.silu(heads(gd_z))

    lam = (jnp.exp(jnp.dot(lam_q1.astype(f32), lam_k1.astype(f32)))
           - jnp.exp(jnp.dot(lam_q2.astype(f32), lam_k2.astype(f32))) + lam_init)
    o_df = differential_attention(df_q.reshape(B, S, H, 2, DIFF_DQK), df_k.reshape(B, S, H, 2, DIFF_DQK),
                                  heads(df_v), lam)
    o_df = rms_norm(o_df, diff_out_norm) * (1.0 - lam_init)

    log_a = jax.nn.log_sigmoid(jnp.einsum('bsr,rk->bsk', gl_r, gla_gate_w2.astype(f32))
                               + gla_gate_b.astype(f32)) / GLA_TAU
    o_gl = gla_chunked(heads(gl_q) * GLA_DK ** -0.5, heads(gl_k), heads(gl_v), heads(log_a))
    o_gl = rms_norm(o_gl, gla_out_norm) * jax.nn.silu(heads(gl_g))

    mix = jnp.concatenate([o_sb, o_gd, o_df, o_gl], axis=2).reshape(B, S, MIX_WIDTH)
    return mix.astype(h.dtype)


def swiglu(h, w_gate, w_up, w_down):
    a = jnp.einsum('bsd,df->bsf', h, w_gate)
    b = jnp.einsum('bsd,df->bsf', h, w_up)
    return jnp.einsum('bsf,fd->bsd', jax.nn.silu(a) * b, w_down)


def setup_inputs(seed: int = 0) -> dict:
    key = jax.random.key(seed)
    ks = jax.random.split(key, 21)
    f32 = jnp.float32
    H = N_GROUP_HEADS

    def normal(k, shape, scale):
        return jax.random.normal(k, shape, f32) * scale

    def gain(k, shape):
        return 1.0 + 0.02 * jax.random.normal(k, shape, f32)

    dt = jnp.exp(jax.random.uniform(ks[5], (DEPTH, H), f32, math.log(DT_MIN), math.log(DT_MAX)))
    return {
        "x": normal(ks[0], (BATCH, SEQ, D_MODEL), 1.0),
        "attn_norm": gain(ks[1], (DEPTH, D_MODEL)),
        "w_in": normal(ks[2], (DEPTH, D_MODEL, IN_COLS), D_MODEL ** -0.5),
        "gdn_conv_w": normal(ks[3], (DEPTH, GDN_CONV, 3 * GROUP_WIDTH), GDN_CONV ** -0.5),
        "gdn_a_log": jnp.log(jax.random.uniform(ks[4], (DEPTH, H), f32, 1.0, 16.0)),
        "gdn_dt_bias": dt + jnp.log(-jnp.expm1(-dt)),
        "gdn_out_norm": gain(ks[6], (DEPTH, GDN_DV)),
        "diff_lam_q1": normal(ks[7], (DEPTH, DIFF_DQK), 0.1),
        "diff_lam_k1": normal(ks[8], (DEPTH, DIFF_DQK), 0.1),
        "diff_lam_q2": normal(ks[9], (DEPTH, DIFF_DQK), 0.1),
        "diff_lam_k2": normal(ks[10], (DEPTH, DIFF_DQK), 0.1),
        "diff_out_norm": gain(ks[11], (DEPTH, HEAD_DIM)),
        "gla_gate_w2": normal(ks[12], (DEPTH, GLA_GATE_RANK, H * GLA_DK), GLA_GATE_RANK ** -0.5),
        "gla_gate_b": normal(ks[13], (DEPTH, H * GLA_DK), 0.1),
        "gla_out_norm": gain(ks[14], (DEPTH, GLA_DV)),
        "w_out": normal(ks[15], (DEPTH, MIX_WIDTH, D_MODEL), MIX_WIDTH ** -0.5),
        "ffn_norm": gain(ks[16], (DEPTH, D_MODEL)),
        "w_gate": normal(ks[17], (DEPTH, D_MODEL, FFN_HIDDEN), D_MODEL ** -0.5),
        "w_up": normal(ks[18], (DEPTH, D_MODEL, FFN_HIDDEN), D_MODEL ** -0.5),
        "w_down": normal(ks[19], (DEPTH, FFN_HIDDEN, D_MODEL), FFN_HIDDEN ** -0.5),
        "final_norm": gain(ks[20], (D_MODEL,)),
    }


def reference(x, attn_norm, w_in, gdn_conv_w, gdn_a_log, gdn_dt_bias, gdn_out_norm, diff_lam_q1, diff_lam_k1,
              diff_lam_q2, diff_lam_k2, diff_out_norm, gla_gate_w2, gla_gate_b, gla_out_norm, w_out, ffn_norm,
              w_gate, w_up, w_down, final_norm):
    for l in range(DEPTH):
        lam_init = 0.8 - 0.6 * math.exp(-0.3 * l)
        h = rms_norm(x, attn_norm[l])
        mix = hybrid_mixer(h, w_in[l], gdn_conv_w[l], gdn_a_log[l], gdn_dt_bias[l], gdn_out_norm[l],
                           diff_lam_q1[l], diff_lam_k1[l], diff_lam_q2[l], diff_lam_k2[l], diff_out_norm[l],
                           gla_gate_w2[l], gla_gate_b[l], gla_out_norm[l], lam_init)
        x = x + jnp.einsum('bsc,cd->bsd', mix, w_out[l])
        x = x + swiglu(rms_norm(x, ffn_norm[l]), w_gate[l], w_up[l], w_down[l])
    return rms_norm(x, final_norm)
```

```python
import functools
import math

import jax
import jax.numpy as jnp
from jax import lax
from jax.experimental import pallas as pl
from jax.experimental.pallas import tpu as pltpu

F32 = jnp.float32
BF16 = jnp.bfloat16
HIGHEST = lax.Precision.HIGHEST

NORM_EPS = 1e-6
N_HEADS = 4
HEAD_DIM = 128
GROUP = N_HEADS * HEAD_DIM
CHUNK = 64
CONV_TAPS = 4
GLA_DK = 64
GLA_RANK = 16
GLA_TAU = 16.0
ATT_BLOCK = 128
CONV_HALO = 8

COL_SB = 0
COL_GD = 3 * GROUP
COL_DF = 7 * GROUP
COL_GL = 10 * GROUP
COL_SMALL = 13 * GROUP
SMALL_W = 128
IN_COLS_PACKED = 14 * GROUP
VMEM_LIMIT = 56 * 1024 * 1024


def _cparams(*sem):
    return pltpu.CompilerParams(dimension_semantics=sem, vmem_limit_bytes=VMEM_LIMIT)


def _dot(a, b, precision=None):
    return jnp.dot(a, b, preferred_element_type=F32, precision=precision)


def _dot_nt(a, b, precision=None):
    return lax.dot_general(a, b, (((1,), (1,)), ((), ())), preferred_element_type=F32, precision=precision)


def _dot_tn(a, b, precision=None):
    return lax.dot_general(a, b, (((0,), (0,)), ((), ())), preferred_element_type=F32, precision=precision)


def _rms(x, gain):
    return x * lax.rsqrt(jnp.mean(x * x, axis=-1, keepdims=True) + NORM_EPS) * gain


def _silu(x):
    return x / (1.0 + jnp.exp(-x))


def _sigmoid(x):
    return 1.0 / (1.0 + jnp.exp(-x))


def _softplus(x):
    return jnp.maximum(x, 0.0) + jnp.log1p(jnp.exp(-jnp.abs(x)))


def _norm_matmul_kernel(x_ref, g_ref, w_ref, o_ref, h_ref):
    @pl.when(pl.program_id(1) == 0)
    def _():
        h_ref[...] = _rms(x_ref[...], g_ref[...]).astype(BF16)

    o_ref[...] = _dot(h_ref[...], w_ref[...]).astype(o_ref.dtype)


def _norm_matmul(x, gain, w, *, tm, tn, out_dtype):
    t, d = x.shape
    n = w.shape[1]
    return pl.pallas_call(
        _norm_matmul_kernel,
        grid=(t // tm, n // tn),
        in_specs=[
            pl.BlockSpec((tm, d), lambda i, j: (i, 0)),
            pl.BlockSpec((1, d), lambda i, j: (0, 0)),
            pl.BlockSpec((d, tn), lambda i, j: (0, j)),
        ],
        out_specs=pl.BlockSpec((tm, tn), lambda i, j: (i, j)),
        out_shape=jax.ShapeDtypeStruct((t, n), out_dtype),
        scratch_shapes=[pltpu.VMEM((tm, d), BF16)],
        compiler_params=_cparams("parallel", "arbitrary"),
        name="norm_in_proj",
    )(x, gain.reshape(1, d), w)


def _sb_kernel(q_ref, k_ref, v_ref, o_ref, *, blk, scale):
    i = pl.program_id(2)
    q = q_ref[...].astype(BF16)
    row = lax.broadcasted_iota(jnp.int32, (blk, blk), 0)
    col = lax.broadcasted_iota(jnp.int32, (blk, blk), 1)
    suffix = jnp.where(row > col, 1.0, 0.0).astype(BF16)
    causal = col < row

    def block(j, carry, acc, masked):
        start = pl.multiple_of(j * blk, blk)
        k = k_ref[pl.ds(start, blk), :].astype(BF16)
        v = v_ref[pl.ds(start, blk), :].astype(BF16)
        z = _dot_nt(q, k) * scale
        sp = _softplus(z)
        log_fail = -sp
        if masked:
            log_fail = jnp.where(causal, log_fail, 0.0)
        hi = log_fail.astype(BF16)
        lo = (log_fail - hi.astype(F32)).astype(BF16)
        tail = _dot(hi, suffix) + _dot(lo, suffix)
        att = jnp.exp(z - sp + tail + carry)
        if masked:
            att = jnp.where(causal, att, 0.0)
        acc = acc + _dot(att.astype(BF16), v)
        carry = carry + jnp.sum(log_fail, axis=1, keepdims=True)
        return carry, acc

    carry0 = jnp.zeros((blk, 1), F32)
    acc0 = jnp.zeros((blk, HEAD_DIM), F32)
    carry, acc = block(i, carry0, acc0, True)

    def body(n, state):
        return block(i - 1 - n, state[0], state[1], False)

    carry, acc = lax.fori_loop(0, i, body, (carry, acc))
    o_ref[...] = acc.astype(o_ref.dtype)


def _sb_attention(proj, batch, seq):
    blk = ATT_BLOCK
    nq = seq // blk
    qb, kb, vb = COL_SB // HEAD_DIM, (COL_SB + GROUP) // HEAD_DIM, (COL_SB + 2 * GROUP) // HEAD_DIM
    kern = functools.partial(_sb_kernel, blk=blk, scale=HEAD_DIM ** -0.5)
    return pl.pallas_call(
        kern,
        grid=(batch, N_HEADS, nq),
        in_specs=[
            pl.BlockSpec((blk, HEAD_DIM), lambda b, h, i: (b * nq + i, qb + h)),
            pl.BlockSpec((seq, HEAD_DIM), lambda b, h, i: (b, kb + h)),
            pl.BlockSpec((seq, HEAD_DIM), lambda b, h, i: (b, vb + h)),
        ],
        out_specs=pl.BlockSpec((blk, HEAD_DIM), lambda b, h, i: (b * nq + i, h)),
        out_shape=jax.ShapeDtypeStruct((batch * seq, GROUP), BF16),
        compiler_params=_cparams("parallel", "parallel", "arbitrary"),
        name="sb_attention",
    )(proj, proj, proj)


def _diff_kernel(lam_ref, q_ref, k_ref, v_ref, g_ref, o_ref, *, blk, scale, lam_init):
    i = pl.program_id(2)
    half = HEAD_DIM // 2
    lane = lax.broadcasted_iota(jnp.int32, (blk, HEAD_DIM), 1)
    q = q_ref[...]
    q1 = jnp.where(lane < half, q, 0.0).astype(BF16)
    q2 = jnp.where(lane >= half, q, 0.0).astype(BF16)
    row = lax.broadcasted_iota(jnp.int32, (blk, blk), 0)
    col = lax.broadcasted_iota(jnp.int32, (blk, blk), 1)
    causal = col <= row

    def online(qm, k, v, m, l, acc, masked):
        s = _dot_nt(qm, k) * scale
        if masked:
            s = jnp.where(causal, s, -jnp.inf)
        m_new = jnp.maximum(m, jnp.max(s, axis=1, keepdims=True))
        alpha = jnp.exp(m - m_new)
        p = jnp.exp(s - m_new)
        l = alpha * l + jnp.sum(p, axis=1, keepdims=True)
        acc = alpha * acc + _dot(p.astype(BF16), v)
        return m_new, l, acc

    def block(j, state, masked):
        start = pl.multiple_of(j * blk, blk)
        k = k_ref[pl.ds(start, blk), :].astype(BF16)
        v = v_ref[pl.ds(start, blk), :].astype(BF16)
        m1, l1, a1, m2, l2, a2 = state
        m1, l1, a1 = online(q1, k, v, m1, l1, a1, masked)
        m2, l2, a2 = online(q2, k, v, m2, l2, a2, masked)
        return m1, l1, a1, m2, l2, a2

    neg = jnp.full((blk, 1), -jnp.inf, F32)
    zero = jnp.zeros((blk, 1), F32)
    zacc = jnp.zeros((blk, HEAD_DIM), F32)
    state = (neg, zero, zacc, neg, zero, zacc)
    state = lax.fori_loop(0, i, lambda j, st: block(j, st, False), state)
    _, l1, a1, _, l2, a2 = block(i, state, True)

    lam_v = lam_ref[...]
    d1 = jnp.sum(lam_v[0:1, :] * lam_v[1:2, :], axis=1, keepdims=True)
    d2 = jnp.sum(lam_v[2:3, :] * lam_v[3:4, :], axis=1, keepdims=True)
    lam = jnp.exp(d1) - jnp.exp(d2) + lam_init
    out = a1 / l1 - lam * (a2 / l2)
    o_ref[...] = (_rms(out, g_ref[...]) * (1.0 - lam_init)).astype(o_ref.dtype)


def _diff_attention(proj, lam_vecs, out_gain, lam_init, batch, seq):
    blk = ATT_BLOCK
    nq = seq // blk
    qb, kb, vb = COL_DF // HEAD_DIM, (COL_DF + GROUP) // HEAD_DIM, (COL_DF + 2 * GROUP) // HEAD_DIM
    kern = functools.partial(_diff_kernel, blk=blk, scale=(HEAD_DIM // 2) ** -0.5, lam_init=lam_init)
    return pl.pallas_call(
        kern,
        grid=(batch, N_HEADS, nq),
        in_specs=[
            pl.BlockSpec(lam_vecs.shape, lambda b, h, i: (0, 0)),
            pl.BlockSpec((blk, HEAD_DIM), lambda b, h, i: (b * nq + i, qb + h)),
            pl.BlockSpec((seq, HEAD_DIM), lambda b, h, i: (b, kb + h)),
            pl.BlockSpec((seq, HEAD_DIM), lambda b, h, i: (b, vb + h)),
            pl.BlockSpec((1, HEAD_DIM), lambda b, h, i: (0, 0)),
        ],
        out_specs=pl.BlockSpec((blk, HEAD_DIM), lambda b, h, i: (b * nq + i, h)),
        out_shape=jax.ShapeDtypeStruct((batch * seq, GROUP), BF16),
        compiler_params=_cparams("parallel", "parallel", "arbitrary"),
        name="diff_attention",
    )(lam_vecs, proj, proj, proj, out_gain.reshape(1, HEAD_DIM))


def _chunk_iotas():
    row = lax.broadcasted_iota(jnp.int32, (CHUNK, CHUNK), 0)
    col = lax.broadcasted_iota(jnp.int32, (CHUNK, CHUNK), 1)
    return row, col


def _lower_left_blocks(row, col, level):
    same_pair = (row >> (level + 1)) == (col >> (level + 1))
    return same_pair & (((row >> level) & 1) == 1) & (((col >> level) & 1) == 0)


def _unit_lower_inverse(a, row, col):
    inv = jnp.where(row == col, 1.0, 0.0).astype(F32)
    for level in range(CHUNK.bit_length() - 1):
        c = jnp.where(_lower_left_blocks(row, col, level), a, 0.0)
        inv = inv - _dot(_dot(inv, c, HIGHEST), inv, HIGHEST)
    return inv


def _gdn_kernel(x_ref, z_ref, sm_ref, cw_ref, hp_ref, gain_ref, o_ref,
                xs_ref, qn_ref, kn_ref, vn_ref, st_ref, *, ts):
    t = pl.program_id(1)

    @pl.when(t == 0)
    def _():
        xs_ref[0:CONV_HALO, :] = jnp.zeros((CONV_HALO, 3 * GROUP), F32)
        st_ref[...] = jnp.zeros_like(st_ref)

    xs_ref[CONV_HALO:CONV_HALO + ts, :] = x_ref[...]

    for part, dst in enumerate((qn_ref, kn_ref, vn_ref)):
        for h in range(N_HEADS):
            c0 = part * GROUP + h * HEAD_DIM
            acc = jnp.zeros((ts, HEAD_DIM), F32)
            for j in range(CONV_TAPS):
                off = CONV_HALO - (CONV_TAPS - 1) + j
                acc = acc + xs_ref[off:off + ts, c0:c0 + HEAD_DIM] * cw_ref[j:j + 1, c0:c0 + HEAD_DIM]
            y = _silu(acc)
            if part < 2:
                y = y * lax.rsqrt(jnp.sum(y * y, axis=-1, keepdims=True) + NORM_EPS)
            if part == 0:
                y = y * (HEAD_DIM ** -0.5)
            dst[:, h * HEAD_DIM:(h + 1) * HEAD_DIM] = y

    xs_ref[0:CONV_HALO, :] = xs_ref[ts:ts + CONV_HALO, :]

    row, col = _chunk_iotas()
    incl = row >= col
    strict = row > col
    eye = row == col
    lower_ones = jnp.where(incl, 1.0, 0.0).astype(F32)
    all_ones = jnp.ones((CHUNK, CHUNK), F32)

    def chunk(c, _):
        r0 = pl.multiple_of(c * CHUNK, CHUNK)
        sm = sm_ref[pl.ds(r0, CHUNK), :]
        for h in range(N_HEADS):
            hs = slice(h * HEAD_DIM, (h + 1) * HEAD_DIM)
            beta = _sigmoid(sm[:, h:h + 1])
            g = -jnp.exp(hp_ref[0:1, h:h + 1]) * _softplus(sm[:, N_HEADS + h:N_HEADS + h + 1] + hp_ref[1:2, h:h + 1])
            g_b = jnp.broadcast_to(g, (CHUNK, HEAD_DIM))
            gc = _dot(lower_ones, g_b, HIGHEST)
            gc_row = _dot(all_ones, jnp.where(eye, gc[:, :CHUNK], 0.0), HIGHEST)
            decay = jnp.exp(jnp.where(incl, gc[:, :CHUNK] - gc_row, -jnp.inf))
            g_last = gc[CHUNK - 1:CHUNK, :]

            q = qn_ref[pl.ds(r0, CHUNK), hs]
            k = kn_ref[pl.ds(r0, CHUNK), hs]
            v = vn_ref[pl.ds(r0, CHUNK), hs]
            kb = k * beta
            k16 = k.astype(BF16)
            a_mat = jnp.where(strict, _dot_nt(kb.astype(BF16), k16) * decay, 0.0)
            inv = _unit_lower_inverse(a_mat, row, col)
            u = _dot(inv, v * beta, HIGHEST)
            w = _dot(inv, kb * jnp.exp(gc), HIGHEST)
            qk = _dot_nt(q.astype(BF16), k16) * decay

            state = st_ref[h]
            s16 = state.astype(BF16)
            v_new = u - _dot(w.astype(BF16), s16)
            o = _dot((q * jnp.exp(gc)).astype(BF16), s16) + _dot(qk.astype(BF16), v_new.astype(BF16))
            k_dec = (k * jnp.exp(g_last - gc)).astype(BF16)
            st_ref[h] = state * jnp.exp(g_last) + _dot_tn(k_dec, v_new.astype(BF16))

            zg = z_ref[pl.ds(r0, CHUNK), hs]
            o_ref[pl.ds(r0, CHUNK), hs] = (_rms(o, gain_ref[...]) * _silu(zg)).astype(o_ref.dtype)
        return 0

    lax.fori_loop(0, ts // CHUNK, chunk, 0)


def _gdn(proj, conv_w, head_params, out_gain, batch, seq, *, ts):
    nt = seq // ts
    kern = functools.partial(_gdn_kernel, ts=ts)
    qkv_blk = COL_GD // (3 * GROUP)
    z_blk = (COL_GD + 3 * GROUP) // GROUP
    sm_blk = COL_SMALL // SMALL_W
    return pl.pallas_call(
        kern,
        grid=(batch, nt),
        in_specs=[
            pl.BlockSpec((ts, 3 * GROUP), lambda b, t: (b * nt + t, qkv_blk)),
            pl.BlockSpec((ts, GROUP), lambda b, t: (b * nt + t, z_blk)),
            pl.BlockSpec((ts, SMALL_W), lambda b, t: (b * nt + t, sm_blk)),
            pl.BlockSpec((CONV_TAPS, 3 * GROUP), lambda b, t: (0, 0)),
            pl.BlockSpec(head_params.shape, lambda b, t: (0, 0)),
            pl.BlockSpec((1, HEAD_DIM), lambda b, t: (0, 0)),
        ],
        out_specs=pl.BlockSpec((ts, GROUP), lambda b, t: (b * nt + t, 0)),
        out_shape=jax.ShapeDtypeStruct((batch * seq, GROUP), BF16),
        scratch_shapes=[
            pltpu.VMEM((ts + CONV_HALO, 3 * GROUP), F32),
            pltpu.VMEM((ts, GROUP), F32),
            pltpu.VMEM((ts, GROUP), F32),
            pltpu.VMEM((ts, GROUP), F32),
            pltpu.VMEM((N_HEADS, HEAD_DIM, HEAD_DIM), F32),
        ],
        compiler_params=_cparams("parallel", "arbitrary"),
        name="gated_deltanet",
    )(proj, proj, proj, conv_w, head_params, out_gain.reshape(1, HEAD_DIM))


def _gla_kernel(qk_ref, v_ref, og_ref, sm_ref, w2_ref, gb_ref, gain_ref, o_ref, st_ref, *, ts):
    t = pl.program_id(1)

    @pl.when(t == 0)
    def _():
        st_ref[...] = jnp.zeros_like(st_ref)

    row, col = _chunk_iotas()
    lower_ones = jnp.where(row >= col, 1.0, 0.0).astype(F32)
    eye = row == col
    lane = lax.broadcasted_iota(jnp.int32, (CHUNK, HEAD_DIM), 1)
    row_w = lax.broadcasted_iota(jnp.int32, (CHUNK, HEAD_DIM), 0)
    qk_scale = GLA_DK ** -0.5

    levels = []
    for level in range(CHUNK.bit_length() - 1):
        ref_row = ((row >> (level + 1)) << (level + 1)) + (1 << level)
        pick = jnp.where(col == ref_row, 1.0, 0.0).astype(F32)
        upper_w = ((row_w >> level) & 1) == 1
        levels.append((pick, _lower_left_blocks(row, col, level), upper_w))

    def chunk(c, _):
        r0 = pl.multiple_of(c * CHUNK, CHUNK)
        sm = sm_ref[pl.ds(r0, CHUNK), :].astype(BF16)
        gate = _dot(sm, w2_ref[...]) + gb_ref[...]
        log_a = (gate - _softplus(gate)) * (1.0 / GLA_TAU)
        for pair in range(N_HEADS * GLA_DK // HEAD_DIM):
            ps = slice(pair * HEAD_DIM, (pair + 1) * HEAD_DIM)
            b = _dot(lower_ones, log_a[:, ps], HIGHEST)
            b_last = b[CHUNK - 1:CHUNK, :]
            q2 = qk_ref[pl.ds(r0, CHUNK), ps] * qk_scale
            k2 = qk_ref[pl.ds(r0, CHUNK), GROUP // 2 + pair * HEAD_DIM:GROUP // 2 + (pair + 1) * HEAD_DIM]
            q_in = (q2 * jnp.exp(b)).astype(BF16)
            k_out = (k2 * jnp.exp(b_last - b)).astype(BF16)
            q_lv, k_lv = [], []
            for pick, _, upper_w in levels:
                b_ref = _dot(pick, b, HIGHEST)
                q_lv.append(jnp.where(upper_w, q2 * jnp.exp(jnp.where(upper_w, b - b_ref, 0.0)), 0.0))
                k_lv.append(jnp.where(upper_w, 0.0, k2 * jnp.exp(jnp.where(upper_w, 0.0, b_ref - b))))
            diag_qk = q2 * k2
            for r in range(HEAD_DIM // GLA_DK):
                h = pair * (HEAD_DIM // GLA_DK) + r
                hs = slice(h * HEAD_DIM, (h + 1) * HEAD_DIM)
                mine = (lane >= r * GLA_DK) & (lane < (r + 1) * GLA_DK)
                att = jnp.where(eye, jnp.sum(jnp.where(mine, diag_qk, 0.0), axis=1, keepdims=True), 0.0)
                for (_, pair_mask, _), ql, kl in zip(levels, q_lv, k_lv):
                    s = _dot_nt(jnp.where(mine, ql, 0.0).astype(BF16), kl.astype(BF16))
                    att = att + jnp.where(pair_mask, s, 0.0)
                v = v_ref[pl.ds(r0, CHUNK), hs]
                v16 = v.astype(BF16)
                state = st_ref[h]
                q_h = jnp.where(mine, q_in, jnp.zeros_like(q_in))
                o = _dot_nt(q_h, state.astype(BF16)) + _dot(att.astype(BF16), v16)
                k_h = jnp.where(mine, k_out, jnp.zeros_like(k_out))
                st_ref[h] = state * jnp.exp(b_last) + _dot_tn(v16, k_h)
                og = og_ref[pl.ds(r0, CHUNK), hs]
                o_ref[pl.ds(r0, CHUNK), hs] = (_rms(o, gain_ref[...]) * _silu(og)).astype(o_ref.dtype)
        return 0

    lax.fori_loop(0, ts // CHUNK, chunk, 0)


def _gla(proj, w2_padded, gate_bias, out_gain, batch, seq, *, ts):
    nt = seq // ts
    kern = functools.partial(_gla_kernel, ts=ts)
    qk_blk = COL_GL // GROUP
    sm_blk = COL_SMALL // SMALL_W
    return pl.pallas_call(
        kern,
        grid=(batch, nt),
        in_specs=[
            pl.BlockSpec((ts, GROUP), lambda b, t: (b * nt + t, qk_blk)),
            pl.BlockSpec((ts, GROUP), lambda b, t: (b * nt + t, qk_blk + 1)),
            pl.BlockSpec((ts, GROUP), lambda b, t: (b * nt + t, qk_blk + 2)),
            pl.BlockSpec((ts, SMALL_W), lambda b, t: (b * nt + t, sm_blk)),
            pl.BlockSpec(w2_padded.shape, lambda b, t: (0, 0)),
            pl.BlockSpec((1, N_HEADS * GLA_DK), lambda b, t: (0, 0)),
            pl.BlockSpec((1, HEAD_DIM), lambda b, t: (0, 0)),
        ],
        out_specs=pl.BlockSpec((ts, GROUP), lambda b, t: (b * nt + t, 0)),
        out_shape=jax.ShapeDtypeStruct((batch * seq, GROUP), BF16),
        scratch_shapes=[pltpu.VMEM((N_HEADS, HEAD_DIM, HEAD_DIM), F32)],
        compiler_params=_cparams("parallel", "arbitrary"),
        name="gla",
    )(proj, proj, proj, proj, w2_padded, gate_bias.reshape(1, -1), out_gain.reshape(1, HEAD_DIM))


def _out_proj_kernel(x_ref, m0_ref, m1_ref, m2_ref, m3_ref, w_ref, o_ref):
    acc = x_ref[...]
    for g, m_ref in enumerate((m0_ref, m1_ref, m2_ref, m3_ref)):
        acc = acc + _dot(m_ref[...], w_ref[g * GROUP:(g + 1) * GROUP, :])
    o_ref[...] = acc


def _out_proj(x, mixes, w_out, *, tm):
    t, d = x.shape
    mix_spec = pl.BlockSpec((tm, GROUP), lambda i: (i, 0))
    return pl.pallas_call(
        _out_proj_kernel,
        grid=(t // tm,),
        in_specs=[pl.BlockSpec((tm, d), lambda i: (i, 0)), mix_spec, mix_spec, mix_spec, mix_spec,
                  pl.BlockSpec(w_out.shape, lambda i: (0, 0))],
        out_specs=pl.BlockSpec((tm, d), lambda i: (i, 0)),
        out_shape=jax.ShapeDtypeStruct((t, d), F32),
        compiler_params=_cparams("parallel"),
        name="out_proj_residual",
    )(x, *mixes, w_out)


def _ffn_kernel(x_ref, g_ref, wg_ref, wu_ref, wd_ref, o_ref, h_ref, acc_ref):
    f = pl.program_id(1)

    @pl.when(f == 0)
    def _():
        h_ref[...] = _rms(x_ref[...], g_ref[...]).astype(BF16)
        acc_ref[...] = jnp.zeros_like(acc_ref)

    h = h_ref[...]
    hid = _silu(_dot(h, wg_ref[...])) * _dot(h, wu_ref[...])
    acc_ref[...] += _dot(hid.astype(BF16), wd_ref[...])

    @pl.when(f == pl.num_programs(1) - 1)
    def _():
        o_ref[...] = x_ref[...] + acc_ref[...]


def _ffn(x, gain, w_gate, w_up, w_down, *, tm, tf):
    t, d = x.shape
    hidden = w_gate.shape[1]
    return pl.pallas_call(
        _ffn_kernel,
        grid=(t // tm, hidden // tf),
        in_specs=[
            pl.BlockSpec((tm, d), lambda i, f: (i, 0)),
            pl.BlockSpec((1, d), lambda i, f: (0, 0)),
            pl.BlockSpec((d, tf), lambda i, f: (0, f)),
            pl.BlockSpec((d, tf), lambda i, f: (0, f)),
            pl.BlockSpec((tf, d), lambda i, f: (f, 0)),
        ],
        out_specs=pl.BlockSpec((tm, d), lambda i, f: (i, 0)),
        out_shape=jax.ShapeDtypeStruct((t, d), F32),
        scratch_shapes=[pltpu.VMEM((tm, d), BF16), pltpu.VMEM((tm, d), F32)],
        compiler_params=_cparams("parallel", "arbitrary"),
        name="ffn_residual",
    )(x, gain.reshape(1, d), w_gate, w_up, w_down)


def _final_norm_kernel(x_ref, g_ref, o_ref):
    o_ref[...] = _rms(x_ref[...], g_ref[...])


def _final_norm(x, gain, *, tm):
    t, d = x.shape
    return pl.pallas_call(
        _final_norm_kernel,
        grid=(t // tm,),
        in_specs=[pl.BlockSpec((tm, d), lambda i: (i, 0)), pl.BlockSpec((1, d), lambda i: (0, 0))],
        out_specs=pl.BlockSpec((tm, d), lambda i: (i, 0)),
        out_shape=jax.ShapeDtypeStruct((t, d), F32),
        compiler_params=_cparams("parallel"),
        name="final_norm",
    )(x, gain.reshape(1, d))


def _pack_w_in(w_in):
    d = w_in.shape[0]
    gd_small = COL_GD + 4 * GROUP
    df0 = gd_small + 2 * N_HEADS
    gl0 = df0 + 3 * GROUP
    rank0 = gl0 + 3 * GROUP
    small = jnp.concatenate(
        [w_in[:, gd_small:df0], w_in[:, rank0:rank0 + GLA_RANK],
         jnp.zeros((d, IN_COLS_PACKED - COL_SMALL - 2 * N_HEADS - GLA_RANK), w_in.dtype)], axis=1)
    return jnp.concatenate([w_in[:, :gd_small], w_in[:, df0:rank0], small], axis=1).astype(BF16)


def _mixer_layer(x, l, p, batch, seq):
    lam_init = 0.8 - 0.6 * math.exp(-0.3 * l)
    proj = _norm_matmul(x, p["attn_norm"], _pack_w_in(p["w_in"]), tm=512, tn=512, out_dtype=F32)
    o_sb = _sb_attention(proj, batch, seq)
    head_params = jnp.zeros((8, HEAD_DIM), F32)
    head_params = head_params.at[0, :N_HEADS].set(p["gdn_a_log"])
    head_params = head_params.at[1, :N_HEADS].set(p["gdn_dt_bias"])
    o_gd = _gdn(proj, p["gdn_conv_w"], head_params, p["gdn_out_norm"], batch, seq, ts=512)
    lam_vecs = jnp.stack([p["diff_lam_q1"], p["diff_lam_k1"], p["diff_lam_q2"], p["diff_lam_k2"]])
    o_df = _diff_attention(proj, lam_vecs, p["diff_out_norm"], lam_init, batch, seq)
    w2 = jnp.zeros((SMALL_W, N_HEADS * GLA_DK), F32).at[2 * N_HEADS:2 * N_HEADS + GLA_RANK].set(p["gla_gate_w2"])
    o_gl = _gla(proj, w2.astype(BF16), p["gla_gate_b"], p["gla_out_norm"], batch, seq, ts=512)
    return _out_proj(x, (o_sb, o_gd, o_df, o_gl), p["w_out"].astype(BF16), tm=512)


def kernel(x, attn_norm, w_in, gdn_conv_w, gdn_a_log, gdn_dt_bias, gdn_out_norm, diff_lam_q1, diff_lam_k1,
           diff_lam_q2, diff_lam_k2, diff_out_norm, gla_gate_w2, gla_gate_b, gla_out_norm, w_out, ffn_norm,
           w_gate, w_up, w_down, final_norm):
    batch, seq, d = x.shape
    depth = w_in.shape[0]
    stacked = dict(attn_norm=attn_norm, w_in=w_in, gdn_conv_w=gdn_conv_w, gdn_a_log=gdn_a_log,
                   gdn_dt_bias=gdn_dt_bias, gdn_out_norm=gdn_out_norm, diff_lam_q1=diff_lam_q1,
                   diff_lam_k1=diff_lam_k1, diff_lam_q2=diff_lam_q2, diff_lam_k2=diff_lam_k2,
                   diff_out_norm=diff_out_norm, gla_gate_w2=gla_gate_w2, gla_gate_b=gla_gate_b,
                   gla_out_norm=gla_out_norm, w_out=w_out)
    xt = x.reshape(batch * seq, d)
    for l in range(depth):
        p = {name: arr[l] for name, arr in stacked.items()}
        xt = _mixer_layer(xt, l, p, batch, seq)
        xt = _ffn(xt, ffn_norm[l], w_gate[l].astype(BF16), w_up[l].astype(BF16), w_down[l].astype(BF16),
                  tm=512, tf=512)
    return _final_norm(xt, final_norm, tm=512).reshape(batch, seq, d)
```

```python
import functools
import math

import jax
import jax.numpy as jnp
from jax import lax
from jax.experimental import pallas as pl
from jax.experimental.pallas import tpu as pltpu

F32 = jnp.float32
BF16 = jnp.bfloat16

NORM_EPS = 1e-6
N_HEADS = 4
HEAD_DIM = 128
GROUP = N_HEADS * HEAD_DIM
CHUNK = 64
CONV_TAPS = 4
GLA_DK = 64
GLA_RANK = 16
GLA_TAU = 16.0
ATT_BLOCK = 256
CONV_HALO = 8
SUBLANES = 8
GDN_CHUNKS_PER_STEP = 2

COL_SB = 0
COL_GD = 3 * GROUP
COL_DF = 7 * GROUP
COL_GL = 10 * GROUP
COL_SMALL = 13 * GROUP
SMALL_W = 128
IN_COLS_PACKED = 14 * GROUP
VMEM_LIMIT = 56 * 1024 * 1024


def _cparams(*sem):
    return pltpu.CompilerParams(dimension_semantics=sem, vmem_limit_bytes=VMEM_LIMIT)


def _dot(a, b):
    return jnp.dot(a, b, preferred_element_type=F32)


def _dot_nt(a, b):
    return lax.dot_general(a, b, (((1,), (1,)), ((), ())), preferred_element_type=F32)


def _dot_tn(a, b):
    return lax.dot_general(a, b, (((0,), (0,)), ((), ())), preferred_element_type=F32)


def _split(x):
    hi = x.astype(BF16)
    return hi, (x - hi.astype(F32)).astype(BF16)


def _dot_sel(sel16, x):
    hi, lo = _split(x)
    return _dot(sel16, hi) + _dot(sel16, lo)


def _dot_split(a, b):
    ah, al = _split(a)
    bh, bl = _split(b)
    return _dot(ah, bh) + (_dot(ah, bl) + _dot(al, bh))


def _rms(x, gain):
    return x * lax.rsqrt(jnp.mean(x * x, axis=-1, keepdims=True) + NORM_EPS) * gain


def _silu(x):
    return x / (1.0 + jnp.exp(-x))


def _sigmoid(x):
    return 1.0 / (1.0 + jnp.exp(-x))


def _softplus(x):
    return jnp.maximum(x, 0.0) + jnp.log(1.0 + jnp.exp(-jnp.abs(x)))


def _norm_matmul_kernel(x_ref, g_ref, w_ref, o_ref, small_ref, h_ref):
    j = pl.program_id(1)
    last = pl.num_programs(1) - 1

    @pl.when(j == 0)
    def _():
        h_ref[...] = _rms(x_ref[...], g_ref[...]).astype(BF16)

    y = _dot(h_ref[...], w_ref[...])

    @pl.when(j < last)
    def _():
        o_ref[...] = y.astype(o_ref.dtype)

    @pl.when(j == last)
    def _():
        small_ref[...] = y


def _norm_in_proj(x, gain, w, *, tm):
    t, d = x.shape
    tn = GROUP
    n_main = COL_SMALL // tn
    return pl.pallas_call(
        _norm_matmul_kernel,
        grid=(t // tm, n_main + 1),
        in_specs=[
            pl.BlockSpec((tm, d), lambda i, j: (i, 0)),
            pl.BlockSpec((1, d), lambda i, j: (0, 0)),
            pl.BlockSpec((d, tn), lambda i, j: (0, j)),
        ],
        out_specs=[
            pl.BlockSpec((tm, tn), lambda i, j: (i, jnp.minimum(j, n_main - 1))),
            pl.BlockSpec((tm, tn), lambda i, j: (i, 0)),
        ],
        out_shape=[jax.ShapeDtypeStruct((t, COL_SMALL), BF16), jax.ShapeDtypeStruct((t, tn), F32)],
        scratch_shapes=[pltpu.VMEM((tm, d), BF16)],
        compiler_params=_cparams("parallel", "arbitrary"),
        name="norm_in_proj",
    )(x, gain.reshape(1, d), w)


def _sb_kernel(q_ref, k_ref, v_ref, o_ref, *, blk, scale):
    i = pl.program_id(2)
    q = q_ref[...]
    row = lax.broadcasted_iota(jnp.int32, (blk, blk), 0)
    col = lax.broadcasted_iota(jnp.int32, (blk, blk), 1)
    suffix = jnp.where(row > col, 1.0, 0.0).astype(BF16)
    causal = col < row

    def span(first_block, nblk, carry, acc, masked):
        start = pl.multiple_of(first_block * blk, blk)
        k = k_ref[pl.ds(start, nblk * blk), :]
        v = v_ref[pl.ds(start, nblk * blk), :]
        z = _dot_nt(q, k) * scale
        sp = _softplus(z)
        if masked:
            sp = jnp.where(causal, sp, 0.0)
        parts = [slice(b * blk, (b + 1) * blk) for b in range(nblk)]
        tails = [_dot_sel_rhs(sp[:, s], suffix) for s in parts]
        sums = [jnp.sum(sp[:, s], axis=1, keepdims=True) for s in parts]
        offsets = [None] * nblk
        for b in reversed(range(nblk)):
            offsets[b] = carry
            carry = carry + sums[b]
        att = jnp.concatenate(
            [jnp.exp(z[:, s] - sp[:, s] - tails[b] - offsets[b])
             for b, s in enumerate(parts)], axis=1)
        if masked:
            att = jnp.where(causal, att, 0.0)
        return carry, acc + _dot(att.astype(BF16), v)

    carry0 = jnp.zeros((blk, 1), F32)
    acc0 = jnp.zeros((blk, HEAD_DIM), F32)
    state = span(i, 1, carry0, acc0, True)
    state = lax.fori_loop(0, i // 2, lambda n, st: span(i - 2 - 2 * n, 2, st[0], st[1], False), state)
    state = lax.fori_loop(0, i % 2, lambda n, st: span(0, 1, st[0], st[1], False), state)
    o_ref[...] = state[1].astype(o_ref.dtype)


def _dot_sel_rhs(x, sel16):
    hi, lo = _split(x)
    return _dot(hi, sel16) + _dot(lo, sel16)


def _sb_attention(proj, batch, seq):
    blk = ATT_BLOCK
    nq = seq // blk
    qb, kb, vb = COL_SB // HEAD_DIM, (COL_SB + GROUP) // HEAD_DIM, (COL_SB + 2 * GROUP) // HEAD_DIM
    kern = functools.partial(_sb_kernel, blk=blk, scale=HEAD_DIM ** -0.5)
    return pl.pallas_call(
        kern,
        grid=(batch, N_HEADS, nq),
        in_specs=[
            pl.BlockSpec((blk, HEAD_DIM), lambda b, h, i: (b * nq + i, qb + h)),
            pl.BlockSpec((seq, HEAD_DIM), lambda b, h, i: (b, kb + h)),
            pl.BlockSpec((seq, HEAD_DIM), lambda b, h, i: (b, vb + h)),
        ],
        out_specs=pl.BlockSpec((blk, HEAD_DIM), lambda b, h, i: (b * nq + i, h)),
        out_shape=jax.ShapeDtypeStruct((batch * seq, GROUP), BF16),
        compiler_params=_cparams("parallel", "parallel", "arbitrary"),
        name="sb_attention",
    )(proj, proj, proj)


def _diff_kernel(lam_ref, q_ref, k_ref, v_ref, g_ref, o_ref, *, blk, scale, lam_init):
    i = pl.program_id(2)
    half = HEAD_DIM // 2
    lane = lax.broadcasted_iota(jnp.int32, (blk, HEAD_DIM), 1)
    q = q_ref[...].astype(F32) * scale
    qs = jnp.concatenate([jnp.where(lane < half, q, 0.0), jnp.where(lane >= half, q, 0.0)], axis=0).astype(BF16)
    row = lax.broadcasted_iota(jnp.int32, (2 * blk, blk), 0) & (blk - 1)
    col = lax.broadcasted_iota(jnp.int32, (2 * blk, blk), 1)
    causal = col <= row

    def span(first_block, nblk, state, masked):
        m, l_lanes, acc = state
        start = pl.multiple_of(first_block * blk, blk)
        k = k_ref[pl.ds(start, nblk * blk), :]
        v = v_ref[pl.ds(start, nblk * blk), :]
        s = _dot_nt(qs, k)
        if masked:
            s = jnp.where(causal, s, -jnp.inf)
        m_new = jnp.maximum(m, jnp.max(s, axis=1, keepdims=True))
        alpha = jnp.exp(m - m_new)
        p = jnp.exp(s - m_new)
        p_lanes = p[:, 0:HEAD_DIM]
        for c in range(1, nblk * blk // HEAD_DIM):
            p_lanes = p_lanes + p[:, c * HEAD_DIM:(c + 1) * HEAD_DIM]
        return m_new, alpha * l_lanes + p_lanes, alpha * acc + _dot(p.astype(BF16), v)

    state = (jnp.full((2 * blk, 1), -jnp.inf, F32), jnp.zeros((2 * blk, HEAD_DIM), F32),
             jnp.zeros((2 * blk, HEAD_DIM), F32))
    state = lax.fori_loop(0, i // 2, lambda n, st: span(2 * n, 2, st, False), state)
    state = lax.fori_loop(0, i % 2, lambda n, st: span(i - 1, 1, st, False), state)
    _, l_lanes, acc = span(i, 1, state, True)
    norm = acc / jnp.sum(l_lanes, axis=1, keepdims=True)

    lam_v = lam_ref[...]
    d1 = jnp.sum(lam_v[0:1, :] * lam_v[1:2, :], axis=1, keepdims=True)
    d2 = jnp.sum(lam_v[2:3, :] * lam_v[3:4, :], axis=1, keepdims=True)
    lam = jnp.exp(d1) - jnp.exp(d2) + lam_init
    out = norm[:blk] - lam * norm[blk:]
    o_ref[...] = (_rms(out, g_ref[...]) * (1.0 - lam_init)).astype(o_ref.dtype)


def _diff_attention(proj, lam_vecs, out_gain, lam_init, batch, seq):
    blk = ATT_BLOCK
    nq = seq // blk
    qb, kb, vb = COL_DF // HEAD_DIM, (COL_DF + GROUP) // HEAD_DIM, (COL_DF + 2 * GROUP) // HEAD_DIM
    kern = functools.partial(_diff_kernel, blk=blk, scale=(HEAD_DIM // 2) ** -0.5, lam_init=lam_init)
    return pl.pallas_call(
        kern,
        grid=(batch, N_HEADS, nq),
        in_specs=[
            pl.BlockSpec(lam_vecs.shape, lambda b, h, i: (0, 0)),
            pl.BlockSpec((blk, HEAD_DIM), lambda b, h, i: (b * nq + i, qb + h)),
            pl.BlockSpec((seq, HEAD_DIM), lambda b, h, i: (b, kb + h)),
            pl.BlockSpec((seq, HEAD_DIM), lambda b, h, i: (b, vb + h)),
            pl.BlockSpec((1, HEAD_DIM), lambda b, h, i: (0, 0)),
        ],
        out_specs=pl.BlockSpec((blk, HEAD_DIM), lambda b, h, i: (b * nq + i, h)),
        out_shape=jax.ShapeDtypeStruct((batch * seq, GROUP), BF16),
        compiler_params=_cparams("parallel", "parallel", "arbitrary"),
        name="diff_attention",
    )(lam_vecs, proj, proj, proj, out_gain.reshape(1, HEAD_DIM))


def _chunk_iotas(width=CHUNK):
    row = lax.broadcasted_iota(jnp.int32, (CHUNK, width), 0)
    col = lax.broadcasted_iota(jnp.int32, (CHUNK, width), 1)
    return row, col


def _lower_left_blocks(row, col, level):
    same_pair = (row >> (level + 1)) == (col >> (level + 1))
    return same_pair & (((row >> level) & 1) == 1) & (((col >> level) & 1) == 0)


def _unit_lower_inverses(a_list, row, col):
    eye = jnp.where(row == col, 1.0, 0.0).astype(F32)
    invs = [eye - jnp.where(_lower_left_blocks(row, col, 0), a, 0.0) for a in a_list]
    for level in range(1, CHUNK.bit_length() - 1):
        mask = _lower_left_blocks(row, col, level)
        left = [_dot_split(inv, jnp.where(mask, a, 0.0)) for inv, a in zip(invs, a_list)]
        invs = [inv - _dot_split(x, inv) for inv, x in zip(invs, left)]
    return invs


def _gdn_kernel(x_ref, z_ref, sm_ref, cw_ref, hp_ref, gain_ref, o_ref,
                xs_ref, qn_ref, kn_ref, vn_ref, u_ref, w_ref, qg_ref, kd_ref, qk_ref, el_ref, st_ref, *, ts):
    t = pl.program_id(1)

    @pl.when(t == 0)
    def _():
        xs_ref[0:CONV_HALO, :] = jnp.zeros((CONV_HALO, 3 * GROUP), F32)
        st_ref[...] = jnp.zeros_like(st_ref)

    xs_ref[CONV_HALO:CONV_HALO + ts, :] = x_ref[...].astype(F32)

    for part, dst in enumerate((qn_ref, kn_ref, vn_ref)):
        for h in range(N_HEADS):
            c0 = part * GROUP + h * HEAD_DIM
            acc = jnp.zeros((ts, HEAD_DIM), F32)
            for j in range(CONV_TAPS):
                off = CONV_HALO - (CONV_TAPS - 1) + j
                acc = acc + xs_ref[off:off + ts, c0:c0 + HEAD_DIM] * cw_ref[j:j + 1, c0:c0 + HEAD_DIM]
            y = _silu(acc)
            if part < 2:
                y = y * lax.rsqrt(jnp.sum(y * y, axis=-1, keepdims=True) + NORM_EPS)
            if part == 0:
                y = y * (HEAD_DIM ** -0.5)
            dst[:, h * HEAD_DIM:(h + 1) * HEAD_DIM] = y

    xs_ref[0:CONV_HALO, :] = xs_ref[ts:ts + CONV_HALO, :]

    row, col = _chunk_iotas()
    strict = row > col
    row_w, col_w = _chunk_iotas(HEAD_DIM)
    incl_w = row_w >= col_w
    eye_w = row_w == col_w
    lower_ones = jnp.where(row >= col, 1.0, 0.0).astype(BF16)
    all_ones = jnp.ones((CHUNK, CHUNK), BF16)

    def prepare(pair, _):
        gates = []
        for sub in range(GDN_CHUNKS_PER_STEP):
            c = pair * GDN_CHUNKS_PER_STEP + sub
            r0 = pl.multiple_of(c * CHUNK, CHUNK)
            sm = sm_ref[pl.ds(r0, CHUNK), :]
            beta_all = _sigmoid(sm)
            g_all = -jnp.exp(hp_ref[0:1, :]) * _softplus(sm + hp_ref[1:2, :])
            g_b = jnp.concatenate(
                [jnp.broadcast_to(g_all[:, N_HEADS + h:N_HEADS + h + 1], (CHUNK, HEAD_DIM))
                 for h in range(N_HEADS)], axis=1)
            gc_all = _dot_sel(lower_ones, g_b)
            gates.append((c, r0, beta_all, gc_all))
        gc_rows = []
        for _, _, _, gc_all in gates:
            gc_diag = jnp.concatenate(
                [jnp.where(eye_w, gc_all[:, h * HEAD_DIM:(h + 1) * HEAD_DIM], 0.0) for h in range(N_HEADS)], axis=1)
            gc_rows.append(_dot_sel(all_ones, gc_diag))

        probs = []
        for (c, r0, beta_all, gc_all), gc_row_all in zip(gates, gc_rows):
            for h in range(N_HEADS):
                hs = slice(h * HEAD_DIM, (h + 1) * HEAD_DIM)
                beta = beta_all[:, h:h + 1]
                gc = gc_all[:, hs]
                decay = jnp.exp(jnp.where(incl_w, gc - gc_row_all[:, hs], -jnp.inf))[:, :CHUNK]
                q = qn_ref[pl.ds(r0, CHUNK), hs]
                k = kn_ref[pl.ds(r0, CHUNK), hs]
                v = vn_ref[pl.ds(r0, CHUNK), hs]
                kb = k * beta
                k16 = k.astype(BF16)
                a_mat = jnp.where(strict, _dot_nt(kb.astype(BF16), k16) * decay, 0.0)
                qk = (_dot_nt(q.astype(BF16), k16) * decay).astype(BF16)
                probs.append(dict(c=c, r0=r0, h=h, hs=hs, gc=gc, q=q, k=k, kb=kb, vb=v * beta, a=a_mat, qk=qk))

        invs = _unit_lower_inverses([p["a"] for p in probs], row, col)
        uws = [_dot_split(inv, jnp.concatenate([p["vb"], p["kb"] * jnp.exp(p["gc"])], axis=1))
               for inv, p in zip(invs, probs)]
        for p, uw in zip(probs, uws):
            r0, hs, gc, h = p["r0"], p["hs"], p["gc"], p["h"]
            g_last = gc[CHUNK - 1:CHUNK, :]
            u_ref[pl.ds(r0, CHUNK), hs] = uw[:, :HEAD_DIM]
            w_ref[pl.ds(r0, CHUNK), hs] = uw[:, HEAD_DIM:].astype(BF16)
            qg_ref[pl.ds(r0, CHUNK), hs] = (p["q"] * jnp.exp(gc)).astype(BF16)
            kd_ref[pl.ds(r0, CHUNK), hs] = (p["k"] * jnp.exp(g_last - gc)).astype(BF16)
            qk_ref[pl.ds(r0, CHUNK), h * HEAD_DIM:h * HEAD_DIM + CHUNK] = p["qk"]
            el_ref[pl.ds(pl.multiple_of(p["c"] * SUBLANES, SUBLANES), SUBLANES), hs] = jnp.broadcast_to(
                jnp.exp(g_last), (SUBLANES, HEAD_DIM))
        return 0

    lax.fori_loop(0, ts // (CHUNK * GDN_CHUNKS_PER_STEP), prepare, 0)

    def scan(c, _):
        r0 = pl.multiple_of(c * CHUNK, CHUNK)
        heads = [slice(h * HEAD_DIM, (h + 1) * HEAD_DIM) for h in range(N_HEADS)]
        states = [st_ref[h] for h in range(N_HEADS)]
        s16 = [s.astype(BF16) for s in states]
        ws = [_dot(w_ref[pl.ds(r0, CHUNK), hs], s) for hs, s in zip(heads, s16)]
        inter = [_dot(qg_ref[pl.ds(r0, CHUNK), hs], s) for hs, s in zip(heads, s16)]
        v_new = [(u_ref[pl.ds(r0, CHUNK), hs] - x).astype(BF16) for hs, x in zip(heads, ws)]
        upd = [_dot_tn(kd_ref[pl.ds(r0, CHUNK), hs], vn) for hs, vn in zip(heads, v_new)]
        for h, hs in enumerate(heads):
            e_last = el_ref[pl.ds(pl.multiple_of(c * SUBLANES, SUBLANES), SUBLANES), hs][0:1, :]
            st_ref[h] = states[h] * e_last + upd[h]
        intra = [_dot(qk_ref[pl.ds(r0, CHUNK), h * HEAD_DIM:h * HEAD_DIM + CHUNK], vn)
                 for h, vn in enumerate(v_new)]
        for h, hs in enumerate(heads):
            zg = z_ref[pl.ds(r0, CHUNK), hs].astype(F32)
            o_ref[pl.ds(r0, CHUNK), hs] = (_rms(inter[h] + intra[h], gain_ref[...]) * _silu(zg)).astype(o_ref.dtype)
        return 0

    lax.fori_loop(0, ts // CHUNK, scan, 0)


def _gdn(proj, small, conv_w, head_params, out_gain, batch, seq, *, ts):
    nt = seq // ts
    kern = functools.partial(_gdn_kernel, ts=ts)
    qkv_blk = COL_GD // (3 * GROUP)
    z_blk = (COL_GD + 3 * GROUP) // GROUP
    return pl.pallas_call(
        kern,
        grid=(batch, nt),
        in_specs=[
            pl.BlockSpec((ts, 3 * GROUP), lambda b, t: (b * nt + t, qkv_blk)),
            pl.BlockSpec((ts, GROUP), lambda b, t: (b * nt + t, z_blk)),
            pl.BlockSpec((ts, SMALL_W), lambda b, t: (b * nt + t, 0)),
            pl.BlockSpec((CONV_TAPS, 3 * GROUP), lambda b, t: (0, 0)),
            pl.BlockSpec(head_params.shape, lambda b, t: (0, 0)),
            pl.BlockSpec((1, HEAD_DIM), lambda b, t: (0, 0)),
        ],
        out_specs=pl.BlockSpec((ts, GROUP), lambda b, t: (b * nt + t, 0)),
        out_shape=jax.ShapeDtypeStruct((batch * seq, GROUP), BF16),
        scratch_shapes=[
            pltpu.VMEM((ts + CONV_HALO, 3 * GROUP), F32),
            pltpu.VMEM((ts, GROUP), F32),
            pltpu.VMEM((ts, GROUP), F32),
            pltpu.VMEM((ts, GROUP), F32),
            pltpu.VMEM((ts, GROUP), F32),
            pltpu.VMEM((ts, GROUP), BF16),
            pltpu.VMEM((ts, GROUP), BF16),
            pltpu.VMEM((ts, GROUP), BF16),
            pltpu.VMEM((ts, GROUP), BF16),
            pltpu.VMEM((ts // CHUNK * SUBLANES, GROUP), F32),
            pltpu.VMEM((N_HEADS, HEAD_DIM, HEAD_DIM), F32),
        ],
        compiler_params=_cparams("parallel", "arbitrary"),
        name="gated_deltanet",
    )(proj, proj, small, conv_w, head_params, out_gain.reshape(1, HEAD_DIM))


def _gla_kernel(qk_ref, v_ref, og_ref, sm_ref, w2_ref, gb_ref, gain_ref, o_ref, st_ref, *, ts):
    t = pl.program_id(1)

    @pl.when(t == 0)
    def _():
        st_ref[...] = jnp.zeros_like(st_ref)

    row, col = _chunk_iotas()
    lower_ones = jnp.where(row >= col, 1.0, 0.0).astype(BF16)
    eye = row == col
    row_w, lane = _chunk_iotas(HEAD_DIM)
    qk_scale = GLA_DK ** -0.5

    levels = []
    for level in range(CHUNK.bit_length() - 1):
        ref_row = ((row >> (level + 1)) << (level + 1)) + (1 << level)
        pick = jnp.where(col == ref_row, 1.0, 0.0).astype(BF16)
        upper_w = ((row_w >> level) & 1) == 1
        levels.append((pick, _lower_left_blocks(row, col, level), upper_w))

    def chunk(c, _):
        r0 = pl.multiple_of(c * CHUNK, CHUNK)
        sm = sm_ref[pl.ds(r0, CHUNK), :].astype(BF16)
        gate = _dot(sm, w2_ref[...]) + gb_ref[...]
        log_a = (gate - _softplus(gate)) * (1.0 / GLA_TAU)
        b_all = _dot_sel(lower_ones, log_a)
        for pair in range(N_HEADS * GLA_DK // HEAD_DIM):
            ps = slice(pair * HEAD_DIM, (pair + 1) * HEAD_DIM)
            b = b_all[:, ps]
            b_last = b[CHUNK - 1:CHUNK, :]
            q2 = qk_ref[pl.ds(r0, CHUNK), ps].astype(F32) * qk_scale
            k2 = qk_ref[pl.ds(r0, CHUNK),
                        GROUP // 2 + pair * HEAD_DIM:GROUP // 2 + (pair + 1) * HEAD_DIM].astype(F32)
            q_in = (q2 * jnp.exp(b)).astype(BF16)
            k_out = (k2 * jnp.exp(b_last - b)).astype(BF16)
            q_lv, k_lv = [], []
            for pick, _, upper_w in levels:
                b_ref = _dot_sel(pick, b)
                q_lv.append(jnp.where(upper_w, q2 * jnp.exp(jnp.where(upper_w, b - b_ref, 0.0)), 0.0))
                k_lv.append(jnp.where(upper_w, 0.0, k2 * jnp.exp(jnp.where(upper_w, 0.0, b_ref - b))))
            diag_qk = q2 * k2
            for r in range(HEAD_DIM // GLA_DK):
                h = pair * (HEAD_DIM // GLA_DK) + r
                hs = slice(h * HEAD_DIM, (h + 1) * HEAD_DIM)
                mine = (lane >= r * GLA_DK) & (lane < (r + 1) * GLA_DK)
                att = jnp.where(eye, jnp.sum(jnp.where(mine, diag_qk, 0.0), axis=1, keepdims=True), 0.0)
                for (_, pair_mask, _), ql, kl in zip(levels, q_lv, k_lv):
                    s = _dot_nt(jnp.where(mine, ql, 0.0).astype(BF16), kl.astype(BF16))
                    att = att + jnp.where(pair_mask, s, 0.0)
                v16 = v_ref[pl.ds(r0, CHUNK), hs]
                state = st_ref[h]
                q_h = jnp.where(mine, q_in, jnp.zeros_like(q_in))
                o = _dot_nt(q_h, state.astype(BF16)) + _dot(att.astype(BF16), v16)
                k_h = jnp.where(mine, k_out, jnp.zeros_like(k_out))
                st_ref[h] = state * jnp.exp(b_last) + _dot_tn(v16, k_h)
                og = og_ref[pl.ds(r0, CHUNK), hs].astype(F32)
                o_ref[pl.ds(r0, CHUNK), hs] = (_rms(o, gain_ref[...]) * _silu(og)).astype(o_ref.dtype)
        return 0

    lax.fori_loop(0, ts // CHUNK, chunk, 0)


def _gla(proj, small, w2_padded, gate_bias, out_gain, batch, seq, *, ts):
    nt = seq // ts
    kern = functools.partial(_gla_kernel, ts=ts)
    qk_blk = COL_GL // GROUP
    return pl.pallas_call(
        kern,
        grid=(batch, nt),
        in_specs=[
            pl.BlockSpec((ts, GROUP), lambda b, t: (b * nt + t, qk_blk)),
            pl.BlockSpec((ts, GROUP), lambda b, t: (b * nt + t, qk_blk + 1)),
            pl.BlockSpec((ts, GROUP), lambda b, t: (b * nt + t, qk_blk + 2)),
            pl.BlockSpec((ts, SMALL_W), lambda b, t: (b * nt + t, 0)),
            pl.BlockSpec(w2_padded.shape, lambda b, t: (0, 0)),
            pl.BlockSpec((1, N_HEADS * GLA_DK), lambda b, t: (0, 0)),
            pl.BlockSpec((1, HEAD_DIM), lambda b, t: (0, 0)),
        ],
        out_specs=pl.BlockSpec((ts, GROUP), lambda b, t: (b * nt + t, 0)),
        out_shape=jax.ShapeDtypeStruct((batch * seq, GROUP), BF16),
        scratch_shapes=[pltpu.VMEM((N_HEADS, HEAD_DIM, HEAD_DIM), F32)],
        compiler_params=_cparams("parallel", "arbitrary"),
        name="gla",
    )(proj, proj, proj, small, w2_padded, gate_bias.reshape(1, -1), out_gain.reshape(1, HEAD_DIM))


def _out_proj_kernel(x_ref, m0_ref, m1_ref, m2_ref, m3_ref, w_ref, o_ref):
    acc = x_ref[...]
    for g, m_ref in enumerate((m0_ref, m1_ref, m2_ref, m3_ref)):
        acc = acc + _dot(m_ref[...], w_ref[g * GROUP:(g + 1) * GROUP, :])
    o_ref[...] = acc


def _out_proj(x, mixes, w_out, *, tm):
    t, d = x.shape
    mix_spec = pl.BlockSpec((tm, GROUP), lambda i: (i, 0))
    return pl.pallas_call(
        _out_proj_kernel,
        grid=(t // tm,),
        in_specs=[pl.BlockSpec((tm, d), lambda i: (i, 0)), mix_spec, mix_spec, mix_spec, mix_spec,
                  pl.BlockSpec(w_out.shape, lambda i: (0, 0))],
        out_specs=pl.BlockSpec((tm, d), lambda i: (i, 0)),
        out_shape=jax.ShapeDtypeStruct((t, d), F32),
        compiler_params=_cparams("parallel"),
        name="out_proj_residual",
    )(x, *mixes, w_out)


def _ffn_kernel(x_ref, g_ref, wg_ref, wu_ref, wd_ref, o_ref, h_ref, acc_ref):
    f = pl.program_id(1)

    @pl.when(f == 0)
    def _():
        h_ref[...] = _rms(x_ref[...], g_ref[...]).astype(BF16)
        acc_ref[...] = jnp.zeros_like(acc_ref)

    h = h_ref[...]
    hid = _silu(_dot(h, wg_ref[...])) * _dot(h, wu_ref[...])
    acc_ref[...] += _dot(hid.astype(BF16), wd_ref[...])

    @pl.when(f == pl.num_programs(1) - 1)
    def _():
        o_ref[...] = x_ref[...] + acc_ref[...]


def _ffn(x, gain, w_gate, w_up, w_down, *, tm, tf):
    t, d = x.shape
    hidden = w_gate.shape[1]
    return pl.pallas_call(
        _ffn_kernel,
        grid=(t // tm, hidden // tf),
        in_specs=[
            pl.BlockSpec((tm, d), lambda i, f: (i, 0)),
            pl.BlockSpec((1, d), lambda i, f: (0, 0)),
            pl.BlockSpec((d, tf), lambda i, f: (0, f)),
            pl.BlockSpec((d, tf), lambda i, f: (0, f)),
            pl.BlockSpec((tf, d), lambda i, f: (f, 0)),
        ],
        out_specs=pl.BlockSpec((tm, d), lambda i, f: (i, 0)),
        out_shape=jax.ShapeDtypeStruct((t, d), F32),
        scratch_shapes=[pltpu.VMEM((tm, d), BF16), pltpu.VMEM((tm, d), F32)],
        compiler_params=_cparams("parallel", "arbitrary"),
        name="ffn_residual",
    )(x, gain.reshape(1, d), w_gate, w_up, w_down)


def _final_norm_kernel(x_ref, g_ref, o_ref):
    o_ref[...] = _rms(x_ref[...], g_ref[...])


def _final_norm(x, gain, *, tm):
    t, d = x.shape
    return pl.pallas_call(
        _final_norm_kernel,
        grid=(t // tm,),
        in_specs=[pl.BlockSpec((tm, d), lambda i: (i, 0)), pl.BlockSpec((1, d), lambda i: (0, 0))],
        out_specs=pl.BlockSpec((tm, d), lambda i: (i, 0)),
        out_shape=jax.ShapeDtypeStruct((t, d), F32),
        compiler_params=_cparams("parallel"),
        name="final_norm",
    )(x, gain.reshape(1, d))


def _pack_w_in(w_in):
    d = w_in.shape[0]
    gd_small = COL_GD + 4 * GROUP
    df0 = gd_small + 2 * N_HEADS
    gl0 = df0 + 3 * GROUP
    rank0 = gl0 + 3 * GROUP
    small = jnp.concatenate(
        [w_in[:, gd_small:df0], w_in[:, rank0:rank0 + GLA_RANK],
         jnp.zeros((d, IN_COLS_PACKED - COL_SMALL - 2 * N_HEADS - GLA_RANK), w_in.dtype)], axis=1)
    return jnp.concatenate([w_in[:, :gd_small], w_in[:, df0:rank0], small], axis=1).astype(BF16)


def _mixer_layer(x, l, p, batch, seq):
    lam_init = 0.8 - 0.6 * math.exp(-0.3 * l)
    proj, small = _norm_in_proj(x, p["attn_norm"], _pack_w_in(p["w_in"]), tm=1024)
    o_sb = _sb_attention(proj, batch, seq)
    head_params = jnp.zeros((SUBLANES, SMALL_W), F32)
    head_params = head_params.at[0, N_HEADS:2 * N_HEADS].set(p["gdn_a_log"])
    head_params = head_params.at[1, N_HEADS:2 * N_HEADS].set(p["gdn_dt_bias"])
    o_gd = _gdn(proj, small, p["gdn_conv_w"], head_params, p["gdn_out_norm"], batch, seq, ts=512)
    lam_vecs = jnp.stack([p["diff_lam_q1"], p["diff_lam_k1"], p["diff_lam_q2"], p["diff_lam_k2"]])
    o_df = _diff_attention(proj, lam_vecs, p["diff_out_norm"], lam_init, batch, seq)
    w2 = jnp.zeros((SMALL_W, N_HEADS * GLA_DK), F32).at[2 * N_HEADS:2 * N_HEADS + GLA_RANK].set(p["gla_gate_w2"])
    o_gl = _gla(proj, small, w2.astype(BF16), p["gla_gate_b"], p["gla_out_norm"], batch, seq, ts=512)
    return _out_proj(x, (o_sb, o_gd, o_df, o_gl), p["w_out"].astype(BF16), tm=512)


def kernel(x, attn_norm, w_in, gdn_conv_w, gdn_a_log, gdn_dt_bias, gdn_out_norm, diff_lam_q1, diff_lam_k1,
           diff_lam_q2, diff_lam_k2, diff_out_norm, gla_gate_w2, gla_gate_b, gla_out_norm, w_out, ffn_norm,
           w_gate, w_up, w_down, final_norm):
    batch, seq, d = x.shape
    depth = w_in.shape[0]
    stacked = dict(attn_norm=attn_norm, w_in=w_in, gdn_conv_w=gdn_conv_w, gdn_a_log=gdn_a_log,
                   gdn_dt_bias=gdn_dt_bias, gdn_out_norm=gdn_out_norm, diff_lam_q1=diff_lam_q1,
                   diff_lam_k1=diff_lam_k1, diff_lam_q2=diff_lam_q2, diff_lam_k2=diff_lam_k2,
                   diff_out_norm=diff_out_norm, gla_gate_w2=gla_gate_w2, gla_gate_b=gla_gate_b,
                   gla_out_norm=gla_out_norm, w_out=w_out)
    xt = x.reshape(batch * seq, d)
    for l in range(depth):
        p = {name: arr[l] for name, arr in stacked.items()}
        xt = _mixer_layer(xt, l, p, batch, seq)
        xt = _ffn(xt, ffn_norm[l], w_gate[l].astype(BF16), w_up[l].astype(BF16), w_down[l].astype(BF16),
                  tm=512, tf=512)
    return _final_norm(xt, final_norm, tm=512).reshape(batch, seq, d)
```

```python
import functools
import math

import jax
import jax.numpy as jnp
from jax import lax
from jax.experimental import pallas as pl
from jax.experimental.pallas import tpu as pltpu

F32 = jnp.float32
BF16 = jnp.bfloat16

NORM_EPS = 1e-6
LOG2_E = math.log2(math.e)
SB_DEAD_LOG2 = 110.0 * LOG2_E
N_HEADS = 4
HEAD_DIM = 128
GROUP = N_HEADS * HEAD_DIM
CHUNK = 64
CONV_TAPS = 4
GLA_DK = 64
GLA_RANK = 16
GLA_TAU = 16.0
ATT_BLOCK = 256
CONV_HALO = 8
SUBLANES = 8
GDN_CHUNKS_PER_STEP = 2
GLA_CHUNKS_PER_STEP = 2
COL_SB = 0
COL_GD = 3 * GROUP
COL_DF = 7 * GROUP
COL_GL = 10 * GROUP
COL_SMALL = 13 * GROUP
SMALL_W = 128
IN_COLS_PACKED = 14 * GROUP
VMEM_LIMIT = 56 * 1024 * 1024


def _cparams(*sem):
    return pltpu.CompilerParams(dimension_semantics=sem, vmem_limit_bytes=VMEM_LIMIT)


def _dot(a, b):
    return jnp.dot(a, b, preferred_element_type=F32)


def _dot_nt(a, b):
    return lax.dot_general(a, b, (((1,), (1,)), ((), ())), preferred_element_type=F32)


def _dot_tn(a, b):
    return lax.dot_general(a, b, (((0,), (0,)), ((), ())), preferred_element_type=F32)


def _split(x):
    hi = x.astype(BF16)
    return hi, (x - hi.astype(F32)).astype(BF16)


def _dot_sel(sel16, x):
    hi, lo = _split(x)
    return _dot(sel16, hi) + _dot(sel16, lo)


def _dot_split(a, b):
    ah, al = _split(a)
    bh, bl = _split(b)
    return _dot(ah, bh) + (_dot(ah, bl) + _dot(al, bh))


def _rms(x, gain):
    return x * lax.rsqrt(jnp.mean(x * x, axis=-1, keepdims=True) + NORM_EPS) * gain


def _silu(x):
    return x / (1.0 + jnp.exp(-x))


def _sigmoid(x):
    return 1.0 / (1.0 + jnp.exp(-x))


def _softplus(x):
    return jnp.maximum(x, 0.0) + jnp.log(1.0 + jnp.exp(-jnp.abs(x)))


def _norm_matmul_kernel(x_ref, g_ref, w_ref, o_ref, small_ref, h_ref):
    j = pl.program_id(1)
    last = pl.num_programs(1) - 1

    @pl.when(j == 0)
    def _():
        h_ref[...] = _rms(x_ref[...], g_ref[...]).astype(BF16)

    y = _dot(h_ref[...], w_ref[...])

    @pl.when(j < last)
    def _():
        o_ref[...] = y.astype(o_ref.dtype)

    @pl.when(j == last)
    def _():
        small_ref[...] = y


def _norm_in_proj(x, gain, w, layer, *, tm):
    t, d = x.shape
    tn = GROUP
    n_main = COL_SMALL // tn
    return pl.pallas_call(
        _norm_matmul_kernel,
        grid=(t // tm, n_main + 1),
        in_specs=[
            pl.BlockSpec((tm, d), lambda i, j: (i, 0)),
            pl.BlockSpec((1, d), lambda i, j: (0, 0)),
            pl.BlockSpec((None, d, tn), lambda i, j: (layer, 0, j)),
        ],
        out_specs=[
            pl.BlockSpec((tm, tn), lambda i, j: (i, jnp.minimum(j, n_main - 1))),
            pl.BlockSpec((tm, tn), lambda i, j: (i, 0)),
        ],
        out_shape=[jax.ShapeDtypeStruct((t, COL_SMALL), BF16), jax.ShapeDtypeStruct((t, tn), F32)],
        scratch_shapes=[pltpu.VMEM((tm, d), BF16)],
        compiler_params=_cparams("parallel", "arbitrary"),
        name="norm_in_proj",
    )(x, gain.reshape(1, d), w)


def _sb_kernel(q_ref, k_ref, v_ref, o_ref, *, blk, scale):
    i = pl.program_id(2)
    q = q_ref[...]
    row = lax.broadcasted_iota(jnp.int32, (blk, blk), 0)
    col = lax.broadcasted_iota(jnp.int32, (blk, blk), 1)
    suffix = jnp.where(row > col, 1.0, 0.0).astype(BF16)
    causal = col < row

    def logits(first_block, nblk):
        start = pl.multiple_of(first_block * blk, blk)
        return _dot_nt(q, k_ref[pl.ds(start, nblk * blk), :]) * (scale * LOG2_E)

    def span(first_block, nblk, z, carry, acc, masked):
        start = pl.multiple_of(first_block * blk, blk)
        v = v_ref[pl.ds(start, nblk * blk), :]
        sp = jnp.maximum(z, 0.0) + jnp.log2(1.0 + jnp.exp2(-jnp.abs(z)))
        if masked:
            sp = jnp.where(causal, sp, 0.0)
        parts = [slice(b * blk, (b + 1) * blk) for b in range(nblk)]
        tails = [_dot_sel_rhs(sp[:, s], suffix) for s in parts]
        sums = [jnp.sum(sp[:, s], axis=1, keepdims=True) for s in parts]
        offsets = [None] * nblk
        for b in reversed(range(nblk)):
            offsets[b] = carry
            carry = carry + sums[b]
        att = jnp.concatenate(
            [jnp.exp2(z[:, s] - sp[:, s] - tails[b] - offsets[b])
             for b, s in enumerate(parts)], axis=1)
        if masked:
            att = jnp.where(causal, att, 0.0)
        return carry, acc + _dot(att.astype(BF16), v)

    carry0 = jnp.zeros((blk, 1), F32)
    acc0 = jnp.zeros((blk, HEAD_DIM), F32)
    carry, acc = span(i, 1, logits(i, 1), carry0, acc0, True)

    def alive(carry):
        return jnp.min(carry) < SB_DEAD_LOG2

    def pair(state):
        n, z, carry, acc = state
        z_next = logits(jnp.maximum(i - 4 - 2 * n, 0), 2)
        carry, acc = span(i - 2 - 2 * n, 2, z, carry, acc, False)
        return n + 1, z_next, carry, acc

    _, _, carry, acc = lax.while_loop(
        lambda state: jnp.logical_and(state[0] < i // 2, alive(state[2])), pair,
        (jnp.int32(0), logits(jnp.maximum(i - 2, 0), 2), carry, acc))
    odd_left = jnp.where(alive(carry), i % 2, 0)
    carry, acc = lax.fori_loop(0, odd_left, lambda n, st: span(0, 1, logits(0, 1), st[0], st[1], False), (carry, acc))
    o_ref[...] = acc.astype(o_ref.dtype)


def _dot_sel_rhs(x, sel16):
    hi, lo = _split(x)
    return _dot(hi, sel16) + _dot(lo, sel16)


def _sb_attention(proj, batch, seq):
    blk = ATT_BLOCK
    nq = seq // blk
    qb, kb, vb = COL_SB // HEAD_DIM, (COL_SB + GROUP) // HEAD_DIM, (COL_SB + 2 * GROUP) // HEAD_DIM
    kern = functools.partial(_sb_kernel, blk=blk, scale=HEAD_DIM ** -0.5)
    return pl.pallas_call(
        kern,
        grid=(batch, N_HEADS, nq),
        in_specs=[
            pl.BlockSpec((blk, HEAD_DIM), lambda b, h, i: (b * nq + i, qb + h)),
            pl.BlockSpec((seq, HEAD_DIM), lambda b, h, i: (b, kb + h)),
            pl.BlockSpec((seq, HEAD_DIM), lambda b, h, i: (b, vb + h)),
        ],
        out_specs=pl.BlockSpec((blk, HEAD_DIM), lambda b, h, i: (b * nq + i, h)),
        out_shape=jax.ShapeDtypeStruct((batch * seq, GROUP), BF16),
        compiler_params=_cparams("parallel", "parallel", "arbitrary"),
        name="sb_attention",
    )(proj, proj, proj)


def _diff_kernel(lam_ref, q_ref, k_ref, v_ref, g_ref, o_ref, *, blk, scale, lam_init):
    i = pl.program_id(2)
    half = HEAD_DIM // 2
    lane = lax.broadcasted_iota(jnp.int32, (blk, HEAD_DIM), 1)
    q = q_ref[...].astype(F32) * scale
    qs = jnp.concatenate([jnp.where(lane < half, q, 0.0), jnp.where(lane >= half, q, 0.0)], axis=0).astype(BF16)
    row = lax.broadcasted_iota(jnp.int32, (2 * blk, blk), 0) & (blk - 1)
    col = lax.broadcasted_iota(jnp.int32, (2 * blk, blk), 1)
    causal = col <= row

    def scores(first_block, nblk):
        start = pl.multiple_of(first_block * blk, blk)
        return _dot_nt(qs, k_ref[pl.ds(start, nblk * blk), :])

    def values(first_block, nblk):
        start = pl.multiple_of(first_block * blk, blk)
        return v_ref[pl.ds(start, nblk * blk), :]

    def span(first_block, nblk, state, masked):
        m, l_lanes, acc = state
        s = scores(first_block, nblk)
        if masked:
            s = jnp.where(causal, s, -jnp.inf)
        m_new = jnp.maximum(m, jnp.max(s, axis=1, keepdims=True))
        alpha = jnp.exp(m - m_new)
        p = jnp.exp(s - m_new)
        p_lanes = p[:, 0:HEAD_DIM]
        for c in range(1, nblk * blk // HEAD_DIM):
            p_lanes = p_lanes + p[:, c * HEAD_DIM:(c + 1) * HEAD_DIM]
        return m_new, alpha * l_lanes + p_lanes, alpha * acc + _dot(p.astype(BF16), values(first_block, nblk))

    state = (jnp.full((2 * blk, 1), -jnp.inf, F32), jnp.zeros((2 * blk, HEAD_DIM), F32),
             jnp.zeros((2 * blk, HEAD_DIM), F32))
    state = lax.fori_loop(0, i // 2, lambda n, st: span(2 * n, 2, st, False), state)
    state = lax.fori_loop(0, i % 2, lambda n, st: span(i - 1, 1, st, False), state)
    _, l_lanes, acc = span(i, 1, state, True)
    norm = acc / jnp.sum(l_lanes, axis=1, keepdims=True)

    lam_v = lam_ref[...]
    d1 = jnp.sum(lam_v[0:1, :] * lam_v[1:2, :], axis=1, keepdims=True)
    d2 = jnp.sum(lam_v[2:3, :] * lam_v[3:4, :], axis=1, keepdims=True)
    lam = jnp.exp(d1) - jnp.exp(d2) + lam_init
    out = norm[:blk] - lam * norm[blk:]
    o_ref[...] = (_rms(out, g_ref[...]) * (1.0 - lam_init)).astype(o_ref.dtype)


def _diff_attention(proj, lam_vecs, out_gain, lam_init, batch, seq):
    blk = ATT_BLOCK
    nq = seq // blk
    qb, kb, vb = COL_DF // HEAD_DIM, (COL_DF + GROUP) // HEAD_DIM, (COL_DF + 2 * GROUP) // HEAD_DIM
    kern = functools.partial(_diff_kernel, blk=blk, scale=(HEAD_DIM // 2) ** -0.5, lam_init=lam_init)
    return pl.pallas_call(
        kern,
        grid=(batch, N_HEADS, nq),
        in_specs=[
            pl.BlockSpec(lam_vecs.shape, lambda b, h, i: (0, 0)),
            pl.BlockSpec((blk, HEAD_DIM), lambda b, h, i: (b * nq + i, qb + h)),
            pl.BlockSpec((seq, HEAD_DIM), lambda b, h, i: (b, kb + h)),
            pl.BlockSpec((seq, HEAD_DIM), lambda b, h, i: (b, vb + h)),
            pl.BlockSpec((1, HEAD_DIM), lambda b, h, i: (0, 0)),
        ],
        out_specs=pl.BlockSpec((blk, HEAD_DIM), lambda b, h, i: (b * nq + i, h)),
        out_shape=jax.ShapeDtypeStruct((batch * seq, GROUP), BF16),
        compiler_params=_cparams("parallel", "parallel", "arbitrary"),
        name="diff_attention",
    )(lam_vecs, proj, proj, proj, out_gain.reshape(1, HEAD_DIM))


def _chunk_iotas(width=CHUNK):
    row = lax.broadcasted_iota(jnp.int32, (CHUNK, width), 0)
    col = lax.broadcasted_iota(jnp.int32, (CHUNK, width), 1)
    return row, col


def _lower_left_blocks(row, col, level):
    same_pair = (row >> (level + 1)) == (col >> (level + 1))
    return same_pair & (((row >> level) & 1) == 1) & (((col >> level) & 1) == 0)


def _unit_lower_inverses(a_list, row, col):
    eye = jnp.where(row == col, 1.0, 0.0).astype(F32)
    invs = [eye - jnp.where(_lower_left_blocks(row, col, 0), a, 0.0) for a in a_list]
    for level in range(1, CHUNK.bit_length() - 1):
        mask = _lower_left_blocks(row, col, level)
        left = [_dot_split(inv, jnp.where(mask, a, 0.0)) for inv, a in zip(invs, a_list)]
        invs = [inv - _dot_split(x, inv) for inv, x in zip(invs, left)]
    return invs


def _gdn_kernel(x_ref, z_ref, sm_ref, cw_ref, hp_ref, gain_ref, o_ref,
                xs_ref, qn_ref, kn_ref, vn_ref, u_ref, w_ref, qg_ref, kd_ref, qk_ref, el_ref, st_ref, *, ts):
    t = pl.program_id(1)

    @pl.when(t == 0)
    def _():
        xs_ref[0:CONV_HALO, :] = jnp.zeros((CONV_HALO, 3 * GROUP), F32)
        st_ref[...] = jnp.zeros_like(st_ref)

    xs_ref[CONV_HALO:CONV_HALO + ts, :] = x_ref[...].astype(F32)

    for part, dst in enumerate((qn_ref, kn_ref, vn_ref)):
        for h in range(N_HEADS):
            c0 = part * GROUP + h * HEAD_DIM
            acc = jnp.zeros((ts, HEAD_DIM), F32)
            for j in range(CONV_TAPS):
                off = CONV_HALO - (CONV_TAPS - 1) + j
                acc = acc + xs_ref[off:off + ts, c0:c0 + HEAD_DIM] * cw_ref[j:j + 1, c0:c0 + HEAD_DIM]
            y = _silu(acc)
            if part < 2:
                y = y * lax.rsqrt(jnp.sum(y * y, axis=-1, keepdims=True) + NORM_EPS)
            if part == 0:
                y = y * (HEAD_DIM ** -0.5)
            dst[:, h * HEAD_DIM:(h + 1) * HEAD_DIM] = y

    xs_ref[0:CONV_HALO, :] = xs_ref[ts:ts + CONV_HALO, :]

    row, col = _chunk_iotas()
    strict = row > col
    row_w, col_w = _chunk_iotas(HEAD_DIM)
    incl_w = row_w >= col_w
    eye_w = row_w == col_w
    lower_ones = jnp.where(row >= col, 1.0, 0.0).astype(BF16)
    all_ones = jnp.ones((CHUNK, CHUNK), BF16)

    def prepare(pair, _):
        gates = []
        for sub in range(GDN_CHUNKS_PER_STEP):
            c = pair * GDN_CHUNKS_PER_STEP + sub
            r0 = pl.multiple_of(c * CHUNK, CHUNK)
            sm = sm_ref[pl.ds(r0, CHUNK), :]
            beta_all = _sigmoid(sm)
            g_all = -jnp.exp(hp_ref[0:1, :]) * _softplus(sm + hp_ref[1:2, :])
            g_b = jnp.concatenate(
                [jnp.broadcast_to(g_all[:, N_HEADS + h:N_HEADS + h + 1], (CHUNK, HEAD_DIM))
                 for h in range(N_HEADS)], axis=1)
            gc_all = _dot_sel(lower_ones, g_b)
            gates.append((c, r0, beta_all, gc_all))
        gc_rows = []
        for _, _, _, gc_all in gates:
            gc_diag = jnp.concatenate(
                [jnp.where(eye_w, gc_all[:, h * HEAD_DIM:(h + 1) * HEAD_DIM], 0.0) for h in range(N_HEADS)], axis=1)
            gc_rows.append(_dot_sel(all_ones, gc_diag))

        probs = []
        for (c, r0, beta_all, gc_all), gc_row_all in zip(gates, gc_rows):
            for h in range(N_HEADS):
                hs = slice(h * HEAD_DIM, (h + 1) * HEAD_DIM)
                beta = beta_all[:, h:h + 1]
                gc = gc_all[:, hs]
                decay = jnp.exp(jnp.where(incl_w, gc - gc_row_all[:, hs], -jnp.inf))[:, :CHUNK]
                q = qn_ref[pl.ds(r0, CHUNK), hs]
                k = kn_ref[pl.ds(r0, CHUNK), hs]
                v = vn_ref[pl.ds(r0, CHUNK), hs]
                kb = k * beta
                k16 = k.astype(BF16)
                a_mat = jnp.where(strict, _dot_nt(kb.astype(BF16), k16) * decay, 0.0)
                qk = (_dot_nt(q.astype(BF16), k16) * decay).astype(BF16)
                probs.append(dict(c=c, r0=r0, h=h, hs=hs, gc=gc, q=q, k=k, kb=kb, vb=v * beta, a=a_mat, qk=qk))

        invs = _unit_lower_inverses([p["a"] for p in probs], row, col)
        uws = [_dot_split(inv, jnp.concatenate([p["vb"], p["kb"] * jnp.exp(p["gc"])], axis=1))
               for inv, p in zip(invs, probs)]
        for p, uw in zip(probs, uws):
            r0, hs, gc, h = p["r0"], p["hs"], p["gc"], p["h"]
            g_last = gc[CHUNK - 1:CHUNK, :]
            u_ref[pl.ds(r0, CHUNK), hs] = uw[:, :HEAD_DIM]
            w_ref[pl.ds(r0, CHUNK), hs] = uw[:, HEAD_DIM:].astype(BF16)
            qg_ref[pl.ds(r0, CHUNK), hs] = (p["q"] * jnp.exp(gc)).astype(BF16)
            kd_ref[pl.ds(r0, CHUNK), hs] = (p["k"] * jnp.exp(g_last - gc)).astype(BF16)
            qk_ref[pl.ds(r0, CHUNK), h * HEAD_DIM:h * HEAD_DIM + CHUNK] = p["qk"]
            el_ref[pl.ds(pl.multiple_of(p["c"] * SUBLANES, SUBLANES), SUBLANES), hs] = jnp.broadcast_to(
                jnp.exp(g_last), (SUBLANES, HEAD_DIM))
        return 0

    lax.fori_loop(0, ts // (CHUNK * GDN_CHUNKS_PER_STEP), prepare, 0)

    def scan(c, _):
        r0 = pl.multiple_of(c * CHUNK, CHUNK)
        heads = [slice(h * HEAD_DIM, (h + 1) * HEAD_DIM) for h in range(N_HEADS)]
        states = [st_ref[h] for h in range(N_HEADS)]
        s16 = [s.astype(BF16) for s in states]
        ws = [_dot(w_ref[pl.ds(r0, CHUNK), hs], s) for hs, s in zip(heads, s16)]
        inter = [_dot(qg_ref[pl.ds(r0, CHUNK), hs], s) for hs, s in zip(heads, s16)]
        v_new = [(u_ref[pl.ds(r0, CHUNK), hs] - x).astype(BF16) for hs, x in zip(heads, ws)]
        upd = [_dot_tn(kd_ref[pl.ds(r0, CHUNK), hs], vn) for hs, vn in zip(heads, v_new)]
        for h, hs in enumerate(heads):
            e_last = el_ref[pl.ds(pl.multiple_of(c * SUBLANES, SUBLANES), SUBLANES), hs][0:1, :]
            st_ref[h] = states[h] * e_last + upd[h]
        intra = [_dot(qk_ref[pl.ds(r0, CHUNK), h * HEAD_DIM:h * HEAD_DIM + CHUNK], vn)
                 for h, vn in enumerate(v_new)]
        for h, hs in enumerate(heads):
            zg = z_ref[pl.ds(r0, CHUNK), hs].astype(F32)
            o_ref[pl.ds(r0, CHUNK), hs] = (_rms(inter[h] + intra[h], gain_ref[...]) * _silu(zg)).astype(o_ref.dtype)
        return 0

    lax.fori_loop(0, ts // CHUNK, scan, 0)


def _gdn(proj, small, conv_w, head_params, out_gain, batch, seq, *, ts):
    nt = seq // ts
    kern = functools.partial(_gdn_kernel, ts=ts)
    qkv_blk = COL_GD // (3 * GROUP)
    z_blk = (COL_GD + 3 * GROUP) // GROUP
    return pl.pallas_call(
        kern,
        grid=(batch, nt),
        in_specs=[
            pl.BlockSpec((ts, 3 * GROUP), lambda b, t: (b * nt + t, qkv_blk)),
            pl.BlockSpec((ts, GROUP), lambda b, t: (b * nt + t, z_blk)),
            pl.BlockSpec((ts, SMALL_W), lambda b, t: (b * nt + t, 0)),
            pl.BlockSpec((CONV_TAPS, 3 * GROUP), lambda b, t: (0, 0)),
            pl.BlockSpec(head_params.shape, lambda b, t: (0, 0)),
            pl.BlockSpec((1, HEAD_DIM), lambda b, t: (0, 0)),
        ],
        out_specs=pl.BlockSpec((ts, GROUP), lambda b, t: (b * nt + t, 0)),
        out_shape=jax.ShapeDtypeStruct((batch * seq, GROUP), BF16),
        scratch_shapes=[
            pltpu.VMEM((ts + CONV_HALO, 3 * GROUP), F32),
            pltpu.VMEM((ts, GROUP), F32),
            pltpu.VMEM((ts, GROUP), F32),
            pltpu.VMEM((ts, GROUP), F32),
            pltpu.VMEM((ts, GROUP), F32),
            pltpu.VMEM((ts, GROUP), BF16),
            pltpu.VMEM((ts, GROUP), BF16),
            pltpu.VMEM((ts, GROUP), BF16),
            pltpu.VMEM((ts, GROUP), BF16),
            pltpu.VMEM((ts // CHUNK * SUBLANES, GROUP), F32),
            pltpu.VMEM((N_HEADS, HEAD_DIM, HEAD_DIM), F32),
        ],
        compiler_params=_cparams("parallel", "arbitrary"),
        name="gated_deltanet",
    )(proj, proj, small, conv_w, head_params, out_gain.reshape(1, HEAD_DIM))


def _gla_kernel(qk_ref, v_ref, og_ref, sm_ref, w2_ref, gb_ref, gain_ref, o_ref,
                att_ref, qin_ref, kout_ref, el_ref, st_ref, *, ts):
    t = pl.program_id(1)

    @pl.when(t == 0)
    def _():
        st_ref[...] = jnp.zeros_like(st_ref)

    row, col = _chunk_iotas()
    lower_ones = jnp.where(row >= col, 1.0, 0.0).astype(BF16)
    eye = row == col
    row_w, lane = _chunk_iotas(HEAD_DIM)
    qk_scale = GLA_DK ** -0.5

    levels = []
    for level in range(CHUNK.bit_length() - 1):
        ref_row = ((row >> (level + 1)) << (level + 1)) + (1 << level)
        pick = jnp.where(col == ref_row, 1.0, 0.0).astype(BF16)
        upper_w = ((row_w >> level) & 1) == 1
        levels.append((pick, _lower_left_blocks(row, col, level), upper_w))

    n_pairs = N_HEADS * GLA_DK // HEAD_DIM
    heads_per_pair = HEAD_DIM // GLA_DK

    def prepare(step, _):
        cums = []
        for sub in range(GLA_CHUNKS_PER_STEP):
            c = step * GLA_CHUNKS_PER_STEP + sub
            r0 = pl.multiple_of(c * CHUNK, CHUNK)
            sm = sm_ref[pl.ds(r0, CHUNK), :].astype(BF16)
            gate = _dot(sm, w2_ref[...]) + gb_ref[...]
            log_a = (gate - _softplus(gate)) * (1.0 / GLA_TAU)
            cums.append((c, r0, _dot_sel(lower_ones, log_a)))
        probs = []
        for c, r0, b_all in cums:
            for pair in range(n_pairs):
                ps = slice(pair * HEAD_DIM, (pair + 1) * HEAD_DIM)
                b = b_all[:, ps]
                q2 = qk_ref[pl.ds(r0, CHUNK), ps].astype(F32) * qk_scale
                k2 = qk_ref[pl.ds(r0, CHUNK),
                            GROUP // 2 + pair * HEAD_DIM:GROUP // 2 + (pair + 1) * HEAD_DIM].astype(F32)
                probs.append(dict(c=c, r0=r0, pair=pair, ps=ps, b=b, q2=q2, k2=k2,
                                  b_refs=[_dot_sel(pick, b) for pick, _, _ in levels]))
        for p in probs:
            b, q2, k2 = p["b"], p["q2"], p["k2"]
            p["q_lv"] = [jnp.where(upper_w, q2 * jnp.exp(jnp.where(upper_w, b - b_ref, 0.0)), 0.0)
                         for (_, _, upper_w), b_ref in zip(levels, p["b_refs"])]
            p["k_lv"] = [jnp.where(upper_w, 0.0, k2 * jnp.exp(jnp.where(upper_w, 0.0, b_ref - b))).astype(BF16)
                         for (_, _, upper_w), b_ref in zip(levels, p["b_refs"])]
        for p in probs:
            r0, b, q2, k2 = p["r0"], p["b"], p["q2"], p["k2"]
            b_last = b[CHUNK - 1:CHUNK, :]
            q_in = (q2 * jnp.exp(b)).astype(BF16)
            k_out = (k2 * jnp.exp(b_last - b)).astype(BF16)
            diag_qk = q2 * k2
            for r in range(heads_per_pair):
                h = p["pair"] * heads_per_pair + r
                hs = slice(h * HEAD_DIM, (h + 1) * HEAD_DIM)
                mine = (lane >= r * GLA_DK) & (lane < (r + 1) * GLA_DK)
                att = jnp.where(eye, jnp.sum(jnp.where(mine, diag_qk, 0.0), axis=1, keepdims=True), 0.0)
                for (_, pair_mask, _), ql, kl in zip(levels, p["q_lv"], p["k_lv"]):
                    s = _dot_nt(jnp.where(mine, ql, 0.0).astype(BF16), kl)
                    att = att + jnp.where(pair_mask, s, 0.0)
                att_ref[pl.ds(r0, CHUNK), h * HEAD_DIM:h * HEAD_DIM + CHUNK] = att.astype(BF16)
                qin_ref[pl.ds(r0, CHUNK), hs] = jnp.where(mine, q_in, jnp.zeros_like(q_in))
                kout_ref[pl.ds(r0, CHUNK), hs] = jnp.where(mine, k_out, jnp.zeros_like(k_out))
            el_ref[pl.ds(pl.multiple_of(p["c"] * SUBLANES, SUBLANES), SUBLANES), p["ps"]] = jnp.broadcast_to(
                jnp.exp(b_last), (SUBLANES, HEAD_DIM))
        return 0

    lax.fori_loop(0, ts // (CHUNK * GLA_CHUNKS_PER_STEP), prepare, 0)

    def scan(c, _):
        r0 = pl.multiple_of(c * CHUNK, CHUNK)
        heads = [slice(h * HEAD_DIM, (h + 1) * HEAD_DIM) for h in range(N_HEADS)]
        states = [st_ref[h] for h in range(N_HEADS)]
        values = [v_ref[pl.ds(r0, CHUNK), hs] for hs in heads]
        inter = [_dot_nt(qin_ref[pl.ds(r0, CHUNK), hs], s.astype(BF16)) for hs, s in zip(heads, states)]
        intra = [_dot(att_ref[pl.ds(r0, CHUNK), h * HEAD_DIM:h * HEAD_DIM + CHUNK], v)
                 for h, v in enumerate(values)]
        upd = [_dot_tn(v, kout_ref[pl.ds(r0, CHUNK), hs]) for hs, v in zip(heads, values)]
        for h, hs in enumerate(heads):
            ps = slice((h // heads_per_pair) * HEAD_DIM, (h // heads_per_pair + 1) * HEAD_DIM)
            e_last = el_ref[pl.ds(pl.multiple_of(c * SUBLANES, SUBLANES), SUBLANES), ps][0:1, :]
            st_ref[h] = states[h] * e_last + upd[h]
            og = og_ref[pl.ds(r0, CHUNK), hs].astype(F32)
            o_ref[pl.ds(r0, CHUNK), hs] = (_rms(inter[h] + intra[h], gain_ref[...]) * _silu(og)).astype(o_ref.dtype)
        return 0

    lax.fori_loop(0, ts // CHUNK, scan, 0)


def _gla(proj, small, w2_padded, gate_bias, out_gain, batch, seq, *, ts):
    nt = seq // ts
    kern = functools.partial(_gla_kernel, ts=ts)
    qk_blk = COL_GL // GROUP
    return pl.pallas_call(
        kern,
        grid=(batch, nt),
        in_specs=[
            pl.BlockSpec((ts, GROUP), lambda b, t: (b * nt + t, qk_blk)),
            pl.BlockSpec((ts, GROUP), lambda b, t: (b * nt + t, qk_blk + 1)),
            pl.BlockSpec((ts, GROUP), lambda b, t: (b * nt + t, qk_blk + 2)),
            pl.BlockSpec((ts, SMALL_W), lambda b, t: (b * nt + t, 0)),
            pl.BlockSpec(w2_padded.shape, lambda b, t: (0, 0)),
            pl.BlockSpec((1, N_HEADS * GLA_DK), lambda b, t: (0, 0)),
            pl.BlockSpec((1, HEAD_DIM), lambda b, t: (0, 0)),
        ],
        out_specs=pl.BlockSpec((ts, GROUP), lambda b, t: (b * nt + t, 0)),
        out_shape=jax.ShapeDtypeStruct((batch * seq, GROUP), BF16),
        scratch_shapes=[
            pltpu.VMEM((ts, GROUP), BF16),
            pltpu.VMEM((ts, GROUP), BF16),
            pltpu.VMEM((ts, GROUP), BF16),
            pltpu.VMEM((ts // CHUNK * SUBLANES, N_HEADS * GLA_DK), F32),
            pltpu.VMEM((N_HEADS, HEAD_DIM, HEAD_DIM), F32),
        ],
        compiler_params=_cparams("parallel", "arbitrary"),
        name="gla",
    )(proj, proj, proj, small, w2_padded, gate_bias.reshape(1, -1), out_gain.reshape(1, HEAD_DIM))


def _out_proj_kernel(x_ref, m0_ref, m1_ref, m2_ref, m3_ref, w_ref, o_ref):
    acc = x_ref[...]
    for g, m_ref in enumerate((m0_ref, m1_ref, m2_ref, m3_ref)):
        acc = acc + _dot(m_ref[...], w_ref[g * GROUP:(g + 1) * GROUP, :])
    o_ref[...] = acc


def _out_proj(x, mixes, w_out, layer, *, tm):
    t, d = x.shape
    mix_spec = pl.BlockSpec((tm, GROUP), lambda i: (i, 0))
    return pl.pallas_call(
        _out_proj_kernel,
        grid=(t // tm,),
        in_specs=[pl.BlockSpec((tm, d), lambda i: (i, 0)), mix_spec, mix_spec, mix_spec, mix_spec,
                  pl.BlockSpec((None,) + w_out.shape[1:], lambda i: (layer, 0, 0))],
        out_specs=pl.BlockSpec((tm, d), lambda i: (i, 0)),
        out_shape=jax.ShapeDtypeStruct((t, d), F32),
        compiler_params=_cparams("parallel"),
        name="out_proj_residual",
    )(x, *mixes, w_out)


def _ffn_kernel(x_ref, g_ref, wg_ref, wu_ref, wd_ref, o_ref, h_ref, acc_ref):
    f = pl.program_id(1)

    @pl.when(f == 0)
    def _():
        h_ref[...] = _rms(x_ref[...], g_ref[...]).astype(BF16)
        acc_ref[...] = jnp.zeros_like(acc_ref)

    h = h_ref[...]
    hid = _silu(_dot(h, wg_ref[...])) * _dot(h, wu_ref[...])
    acc_ref[...] += _dot(hid.astype(BF16), wd_ref[...])

    @pl.when(f == pl.num_programs(1) - 1)
    def _():
        o_ref[...] = x_ref[...] + acc_ref[...]


def _ffn(x, gain, w_gate, w_up, w_down, layer, *, tm, tf):
    t, d = x.shape
    hidden = w_gate.shape[2]
    return pl.pallas_call(
        _ffn_kernel,
        grid=(t // tm, hidden // tf),
        in_specs=[
            pl.BlockSpec((tm, d), lambda i, f: (i, 0)),
            pl.BlockSpec((1, d), lambda i, f: (0, 0)),
            pl.BlockSpec((None, d, tf), lambda i, f: (layer, 0, f)),
            pl.BlockSpec((None, d, tf), lambda i, f: (layer, 0, f)),
            pl.BlockSpec((None, tf, d), lambda i, f: (layer, f, 0)),
        ],
        out_specs=pl.BlockSpec((tm, d), lambda i, f: (i, 0)),
        out_shape=jax.ShapeDtypeStruct((t, d), F32),
        scratch_shapes=[pltpu.VMEM((tm, d), BF16), pltpu.VMEM((tm, d), F32)],
        compiler_params=_cparams("parallel", "arbitrary"),
        name="ffn_residual",
    )(x, gain.reshape(1, d), w_gate, w_up, w_down)


def _final_norm_kernel(x_ref, g_ref, o_ref):
    o_ref[...] = _rms(x_ref[...], g_ref[...])


def _final_norm(x, gain, *, tm):
    t, d = x.shape
    return pl.pallas_call(
        _final_norm_kernel,
        grid=(t // tm,),
        in_specs=[pl.BlockSpec((tm, d), lambda i: (i, 0)), pl.BlockSpec((1, d), lambda i: (0, 0))],
        out_specs=pl.BlockSpec((tm, d), lambda i: (i, 0)),
        out_shape=jax.ShapeDtypeStruct((t, d), F32),
        compiler_params=_cparams("parallel"),
        name="final_norm",
    )(x, gain.reshape(1, d))


def _pack_w_in(w_in):
    gd_small = COL_GD + 4 * GROUP
    df0 = gd_small + 2 * N_HEADS
    gl0 = df0 + 3 * GROUP
    rank0 = gl0 + 3 * GROUP
    pad = jnp.zeros(w_in.shape[:-1] + (IN_COLS_PACKED - COL_SMALL - 2 * N_HEADS - GLA_RANK,), BF16)
    parts = [w_in[..., :gd_small], w_in[..., df0:rank0], w_in[..., gd_small:df0], w_in[..., rank0:rank0 + GLA_RANK]]
    return jnp.concatenate([part.astype(BF16) for part in parts] + [pad], axis=-1)


def _mixer_layer(x, l, p, big, batch, seq):
    lam_init = 0.8 - 0.6 * math.exp(-0.3 * l)
    proj, small = _norm_in_proj(x, p["attn_norm"], big["w_in"], l, tm=1024)
    o_sb = _sb_attention(proj, batch, seq)
    head_params = jnp.zeros((SUBLANES, SMALL_W), F32)
    head_params = head_params.at[0, N_HEADS:2 * N_HEADS].set(p["gdn_a_log"])
    head_params = head_params.at[1, N_HEADS:2 * N_HEADS].set(p["gdn_dt_bias"])
    o_gd = _gdn(proj, small, p["gdn_conv_w"], head_params, p["gdn_out_norm"], batch, seq, ts=512)
    lam_vecs = jnp.stack([p["diff_lam_q1"], p["diff_lam_k1"], p["diff_lam_q2"], p["diff_lam_k2"]])
    o_df = _diff_attention(proj, lam_vecs, p["diff_out_norm"], lam_init, batch, seq)
    w2 = jnp.zeros((SMALL_W, N_HEADS * GLA_DK), F32).at[2 * N_HEADS:2 * N_HEADS + GLA_RANK].set(p["gla_gate_w2"])
    o_gl = _gla(proj, small, w2.astype(BF16), p["gla_gate_b"], p["gla_out_norm"], batch, seq, ts=512)
    return _out_proj(x, (o_sb, o_gd, o_df, o_gl), big["w_out"], l, tm=512)


def kernel(x, attn_norm, w_in, gdn_conv_w, gdn_a_log, gdn_dt_bias, gdn_out_norm, diff_lam_q1, diff_lam_k1,
           diff_lam_q2, diff_lam_k2, diff_out_norm, gla_gate_w2, gla_gate_b, gla_out_norm, w_out, ffn_norm,
           w_gate, w_up, w_down, final_norm):
    batch, seq, d = x.shape
    depth = w_in.shape[0]
    stacked = dict(attn_norm=attn_norm, gdn_conv_w=gdn_conv_w, gdn_a_log=gdn_a_log,
                   gdn_dt_bias=gdn_dt_bias, gdn_out_norm=gdn_out_norm, diff_lam_q1=diff_lam_q1,
                   diff_lam_k1=diff_lam_k1, diff_lam_q2=diff_lam_q2, diff_lam_k2=diff_lam_k2,
                   diff_out_norm=diff_out_norm, gla_gate_w2=gla_gate_w2, gla_gate_b=gla_gate_b,
                   gla_out_norm=gla_out_norm)
    big = dict(w_in=_pack_w_in(w_in), w_out=w_out.astype(BF16), w_gate=w_gate.astype(BF16),
               w_up=w_up.astype(BF16), w_down=w_down.astype(BF16))
    xt = x.reshape(batch * seq, d)
    for l in range(depth):
        p = {name: arr[l] for name, arr in stacked.items()}
        xt = _mixer_layer(xt, l, p, big, batch, seq)
        xt = _ffn(xt, ffn_norm[l], big["w_gate"], big["w_up"], big["w_down"], l, tm=512, tf=512)
    return _final_norm(xt, final_norm, tm=512).reshape(batch, seq, d)
```

```python
import functools
import math

import jax
import jax.numpy as jnp
from jax import lax
from jax.experimental import pallas as pl
from jax.experimental.pallas import tpu as pltpu

F32 = jnp.float32
BF16 = jnp.bfloat16

NORM_EPS = 1e-6
LOG2_E = math.log2(math.e)
SB_DEAD_LOG2 = 110.0 * LOG2_E
N_HEADS = 4
HEAD_DIM = 128
GROUP = N_HEADS * HEAD_DIM
CHUNK = 64
CONV_TAPS = 4
GLA_DK = 64
GLA_RANK = 16
GLA_TAU = 16.0
ATT_BLOCK = 256
CONV_HALO = 8
SUBLANES = 8
GDN_CHUNKS_PER_STEP = 2
GLA_CHUNKS_PER_STEP = 2
DIFF_HEADS_PER_STEP = 2
COL_SB = 0
COL_GD = 3 * GROUP
COL_DF = 7 * GROUP
COL_GL = 10 * GROUP
COL_SMALL = 13 * GROUP
SMALL_W = 128
IN_COLS_PACKED = 14 * GROUP
VMEM_LIMIT = 56 * 1024 * 1024


def _cparams(*sem):
    return pltpu.CompilerParams(dimension_semantics=sem, vmem_limit_bytes=VMEM_LIMIT)


def _dot(a, b):
    return jnp.dot(a, b, preferred_element_type=F32)


def _dot_nt(a, b):
    return lax.dot_general(a, b, (((1,), (1,)), ((), ())), preferred_element_type=F32)


def _dot_tn(a, b):
    return lax.dot_general(a, b, (((0,), (0,)), ((), ())), preferred_element_type=F32)


def _split(x):
    hi = x.astype(BF16)
    return hi, (x - hi.astype(F32)).astype(BF16)


def _dot_sel(sel16, x):
    hi, lo = _split(x)
    return _dot(sel16, hi) + _dot(sel16, lo)


def _dot_split(a, b):
    ah, al = _split(a)
    bh, bl = _split(b)
    return _dot(ah, bh) + (_dot(ah, bl) + _dot(al, bh))


def _rms(x, gain):
    return x * lax.rsqrt(jnp.mean(x * x, axis=-1, keepdims=True) + NORM_EPS) * gain


def _silu(x):
    return x / (1.0 + jnp.exp(-x))


def _sigmoid(x):
    return 1.0 / (1.0 + jnp.exp(-x))


def _softplus(x):
    return jnp.maximum(x, 0.0) + jnp.log(1.0 + jnp.exp(-jnp.abs(x)))


def _norm_matmul_kernel(x_ref, g_ref, w_ref, o_ref, small_ref, h_ref):
    j = pl.program_id(1)
    last = pl.num_programs(1) - 1

    @pl.when(j == 0)
    def _():
        h_ref[...] = _rms(x_ref[...], g_ref[...]).astype(BF16)

    y = _dot(h_ref[...], w_ref[...])

    @pl.when(j < last)
    def _():
        o_ref[...] = y.astype(o_ref.dtype)

    @pl.when(j == last)
    def _():
        small_ref[...] = y


def _norm_in_proj(x, gain, w, layer, *, tm):
    t, d = x.shape
    tn = GROUP
    n_main = COL_SMALL // tn
    return pl.pallas_call(
        _norm_matmul_kernel,
        grid=(t // tm, n_main + 1),
        in_specs=[
            pl.BlockSpec((tm, d), lambda i, j: (i, 0)),
            pl.BlockSpec((1, d), lambda i, j: (0, 0)),
            pl.BlockSpec((None, d, tn), lambda i, j: (layer, 0, j)),
        ],
        out_specs=[
            pl.BlockSpec((tm, tn), lambda i, j: (i, jnp.minimum(j, n_main - 1))),
            pl.BlockSpec((tm, tn), lambda i, j: (i, 0)),
        ],
        out_shape=[jax.ShapeDtypeStruct((t, COL_SMALL), BF16), jax.ShapeDtypeStruct((t, tn), F32)],
        scratch_shapes=[pltpu.VMEM((tm, d), BF16)],
        compiler_params=_cparams("parallel", "arbitrary"),
        name="norm_in_proj",
    )(x, gain.reshape(1, d), w)


def _sb_kernel(q_ref, k_ref, v_ref, o_ref, *, blk, scale):
    i = pl.program_id(2)
    q = q_ref[...]
    row = lax.broadcasted_iota(jnp.int32, (blk, blk), 0)
    col = lax.broadcasted_iota(jnp.int32, (blk, blk), 1)
    suffix = jnp.where(row > col, 1.0, 0.0).astype(BF16)
    lead = jnp.maximum(i - 1, 0)
    col_minus_row = (lax.broadcasted_iota(jnp.int32, (blk, 2 * blk), 1)
                     - lax.broadcasted_iota(jnp.int32, (blk, 2 * blk), 0))
    causal = col_minus_row < (i - lead) * blk

    def logits(first_block, nblk):
        start = pl.multiple_of(first_block * blk, blk)
        return _dot_nt(q, k_ref[pl.ds(start, nblk * blk), :]) * (scale * LOG2_E)

    def span(first_block, nblk, z, carry, acc, masked):
        start = pl.multiple_of(first_block * blk, blk)
        v = v_ref[pl.ds(start, nblk * blk), :]
        sp = jnp.maximum(z, 0.0) + jnp.log2(1.0 + jnp.exp2(-jnp.abs(z)))
        if masked:
            sp = jnp.where(causal, sp, 0.0)
        parts = [slice(b * blk, (b + 1) * blk) for b in range(nblk)]
        tails = [_dot_sel_rhs(sp[:, s], suffix) for s in parts]
        sums = [jnp.sum(sp[:, s], axis=1, keepdims=True) for s in parts]
        offsets = [None] * nblk
        for b in reversed(range(nblk)):
            offsets[b] = carry
            carry = carry + sums[b]
        att = jnp.concatenate(
            [jnp.exp2(z[:, s] - sp[:, s] - tails[b] - offsets[b])
             for b, s in enumerate(parts)], axis=1)
        if masked:
            att = jnp.where(causal, att, 0.0)
        return carry, acc + _dot(att.astype(BF16), v)

    carry0 = jnp.zeros((blk, 1), F32)
    acc0 = jnp.zeros((blk, HEAD_DIM), F32)
    carry, acc = span(lead, 2, logits(lead, 2), carry0, acc0, True)

    def alive(carry):
        return jnp.min(carry) < SB_DEAD_LOG2

    def pair(state):
        n, z, carry, acc = state
        z_next = logits(jnp.maximum(lead - 4 - 2 * n, 0), 2)
        carry, acc = span(lead - 2 - 2 * n, 2, z, carry, acc, False)
        return n + 1, z_next, carry, acc

    _, _, carry, acc = lax.while_loop(
        lambda state: jnp.logical_and(state[0] < lead // 2, alive(state[2])), pair,
        (jnp.int32(0), logits(jnp.maximum(lead - 2, 0), 2), carry, acc))
    odd_left = jnp.where(alive(carry), lead % 2, 0)
    carry, acc = lax.fori_loop(0, odd_left, lambda n, st: span(0, 1, logits(0, 1), st[0], st[1], False), (carry, acc))
    o_ref[...] = acc.astype(o_ref.dtype)


def _dot_sel_rhs(x, sel16):
    hi, lo = _split(x)
    return _dot(hi, sel16) + _dot(lo, sel16)


def _sb_attention(proj, batch, seq):
    blk = ATT_BLOCK
    nq = seq // blk
    qb, kb, vb = COL_SB // HEAD_DIM, (COL_SB + GROUP) // HEAD_DIM, (COL_SB + 2 * GROUP) // HEAD_DIM
    kern = functools.partial(_sb_kernel, blk=blk, scale=HEAD_DIM ** -0.5)
    return pl.pallas_call(
        kern,
        grid=(batch, N_HEADS, nq),
        in_specs=[
            pl.BlockSpec((blk, HEAD_DIM), lambda b, h, i: (b * nq + i, qb + h)),
            pl.BlockSpec((seq, HEAD_DIM), lambda b, h, i: (b, kb + h)),
            pl.BlockSpec((seq, HEAD_DIM), lambda b, h, i: (b, vb + h)),
        ],
        out_specs=pl.BlockSpec((blk, HEAD_DIM), lambda b, h, i: (b * nq + i, h)),
        out_shape=jax.ShapeDtypeStruct((batch * seq, GROUP), BF16),
        compiler_params=_cparams("parallel", "parallel", "arbitrary"),
        name="sb_attention",
    )(proj, proj, proj)


def _diff_kernel(lam_ref, q_ref, k_ref, v_ref, g_ref, o_ref, *, blk, scale, lam_init):
    i = pl.program_id(2)
    half = HEAD_DIM // 2
    lane = lax.broadcasted_iota(jnp.int32, (blk, HEAD_DIM), 1)
    heads = [slice(h * HEAD_DIM, (h + 1) * HEAD_DIM) for h in range(DIFF_HEADS_PER_STEP)]
    qs = []
    for hs in heads:
        q = q_ref[:, hs].astype(F32) * scale
        qs.append(jnp.concatenate([jnp.where(lane < half, q, 0.0), jnp.where(lane >= half, q, 0.0)],
                                  axis=0).astype(BF16))
    lead = jnp.maximum(i - 1, 0)
    row = lax.broadcasted_iota(jnp.int32, (2 * blk, 2 * blk), 0) & (blk - 1)
    col = lax.broadcasted_iota(jnp.int32, (2 * blk, 2 * blk), 1)
    causal = col - row <= (i - lead) * blk

    def softmax(s, state, masked):
        m, l_lanes, acc = state
        if masked:
            s = jnp.where(causal, s, -jnp.inf)
        m_new = jnp.maximum(m, jnp.max(s, axis=1, keepdims=True))
        alpha = jnp.exp(m - m_new)
        p = jnp.exp(s - m_new)
        p_lanes = p[:, 0:HEAD_DIM]
        for c in range(1, s.shape[1] // HEAD_DIM):
            p_lanes = p_lanes + p[:, c * HEAD_DIM:(c + 1) * HEAD_DIM]
        return m_new, alpha * l_lanes + p_lanes, alpha * acc, p.astype(BF16)

    def span(first_block, nblk, states, masked):
        rows = pl.ds(pl.multiple_of(first_block * blk, blk), nblk * blk)
        scores = [_dot_nt(q, k_ref[rows, hs]) for q, hs in zip(qs, heads)]
        out = []
        for s, hs, state in zip(scores, heads, states):
            m, l_lanes, acc, p = softmax(s, state, masked)
            out.append((m, l_lanes, acc + _dot(p, v_ref[rows, hs])))
        return tuple(out)

    init = (jnp.full((2 * blk, 1), -jnp.inf, F32), jnp.zeros((2 * blk, HEAD_DIM), F32),
            jnp.zeros((2 * blk, HEAD_DIM), F32))
    states = span(lead, 2, (init,) * len(heads), True)
    states = lax.fori_loop(0, lead // 2, lambda n, st: span(2 * n, 2, st, False), states)
    states = lax.fori_loop(0, lead % 2, lambda n, st: span(lead - 1, 1, st, False), states)

    lam_v = lam_ref[...]
    d1 = jnp.sum(lam_v[0:1, :] * lam_v[1:2, :], axis=1, keepdims=True)
    d2 = jnp.sum(lam_v[2:3, :] * lam_v[3:4, :], axis=1, keepdims=True)
    lam = jnp.exp(d1) - jnp.exp(d2) + lam_init
    for hs, (_, l_lanes, acc) in zip(heads, states):
        norm = acc / jnp.sum(l_lanes, axis=1, keepdims=True)
        out = norm[:blk] - lam * norm[blk:]
        o_ref[:, hs] = (_rms(out, g_ref[...]) * (1.0 - lam_init)).astype(o_ref.dtype)


def _diff_attention(proj, lam_vecs, out_gain, lam_init, batch, seq):
    blk = ATT_BLOCK
    nq = seq // blk
    width = DIFF_HEADS_PER_STEP * HEAD_DIM
    qb, kb, vb = COL_DF // width, (COL_DF + GROUP) // width, (COL_DF + 2 * GROUP) // width
    kern = functools.partial(_diff_kernel, blk=blk, scale=(HEAD_DIM // 2) ** -0.5, lam_init=lam_init)
    return pl.pallas_call(
        kern,
        grid=(batch, N_HEADS // DIFF_HEADS_PER_STEP, nq),
        in_specs=[
            pl.BlockSpec(lam_vecs.shape, lambda b, h, i: (0, 0)),
            pl.BlockSpec((blk, width), lambda b, h, i: (b * nq + i, qb + h)),
            pl.BlockSpec((seq, width), lambda b, h, i: (b, kb + h)),
            pl.BlockSpec((seq, width), lambda b, h, i: (b, vb + h)),
            pl.BlockSpec((1, HEAD_DIM), lambda b, h, i: (0, 0)),
        ],
        out_specs=pl.BlockSpec((blk, width), lambda b, h, i: (b * nq + i, h)),
        out_shape=jax.ShapeDtypeStruct((batch * seq, GROUP), BF16),
        compiler_params=_cparams("parallel", "parallel", "arbitrary"),
        name="diff_attention",
    )(lam_vecs, proj, proj, proj, out_gain.reshape(1, HEAD_DIM))


def _chunk_iotas(width=CHUNK):
    row = lax.broadcasted_iota(jnp.int32, (CHUNK, width), 0)
    col = lax.broadcasted_iota(jnp.int32, (CHUNK, width), 1)
    return row, col


def _lower_left_blocks(row, col, level):
    same_pair = (row >> (level + 1)) == (col >> (level + 1))
    return same_pair & (((row >> level) & 1) == 1) & (((col >> level) & 1) == 0)


def _unit_lower_inverses(a_list, row, col):
    eye = jnp.where(row == col, 1.0, 0.0).astype(F32)
    invs = [eye - jnp.where(_lower_left_blocks(row, col, 0), a, 0.0) for a in a_list]
    for level in range(1, CHUNK.bit_length() - 1):
        mask = _lower_left_blocks(row, col, level)
        left = [_dot_split(inv, jnp.where(mask, a, 0.0)) for inv, a in zip(invs, a_list)]
        invs = [inv - _dot_split(x, inv) for inv, x in zip(invs, left)]
    return invs


def _gdn_kernel(x_ref, z_ref, sm_ref, cw_ref, hp_ref, gain_ref, o_ref,
                xs_ref, qn_ref, kn_ref, vn_ref, u_ref, w_ref, qg_ref, kd_ref, qk_ref, el_ref, st_ref, *, ts):
    t = pl.program_id(1)

    @pl.when(t == 0)
    def _():
        xs_ref[0:CONV_HALO, :] = jnp.zeros((CONV_HALO, 3 * GROUP), F32)
        st_ref[...] = jnp.zeros_like(st_ref)

    xs_ref[CONV_HALO:CONV_HALO + ts, :] = x_ref[...].astype(F32)

    for part, dst in enumerate((qn_ref, kn_ref, vn_ref)):
        for h in range(N_HEADS):
            c0 = part * GROUP + h * HEAD_DIM
            acc = jnp.zeros((ts, HEAD_DIM), F32)
            for j in range(CONV_TAPS):
                off = CONV_HALO - (CONV_TAPS - 1) + j
                acc = acc + xs_ref[off:off + ts, c0:c0 + HEAD_DIM] * cw_ref[j:j + 1, c0:c0 + HEAD_DIM]
            y = _silu(acc)
            if part < 2:
                y = y * lax.rsqrt(jnp.sum(y * y, axis=-1, keepdims=True) + NORM_EPS)
            if part == 0:
                y = y * (HEAD_DIM ** -0.5)
            dst[:, h * HEAD_DIM:(h + 1) * HEAD_DIM] = y

    xs_ref[0:CONV_HALO, :] = xs_ref[ts:ts + CONV_HALO, :]

    row, col = _chunk_iotas()
    strict = row > col
    row_w, col_w = _chunk_iotas(HEAD_DIM)
    incl_w = row_w >= col_w
    eye_w = row_w == col_w
    lower_ones = jnp.where(row >= col, 1.0, 0.0).astype(BF16)
    all_ones = jnp.ones((CHUNK, CHUNK), BF16)

    def prepare(pair, _):
        gates = []
        for sub in range(GDN_CHUNKS_PER_STEP):
            c = pair * GDN_CHUNKS_PER_STEP + sub
            r0 = pl.multiple_of(c * CHUNK, CHUNK)
            sm = sm_ref[pl.ds(r0, CHUNK), :]
            beta_all = _sigmoid(sm)
            g_all = -jnp.exp(hp_ref[0:1, :]) * _softplus(sm + hp_ref[1:2, :])
            g_b = jnp.concatenate(
                [jnp.broadcast_to(g_all[:, N_HEADS + h:N_HEADS + h + 1], (CHUNK, HEAD_DIM))
                 for h in range(N_HEADS)], axis=1)
            gc_all = _dot_sel(lower_ones, g_b)
            gates.append((c, r0, beta_all, gc_all))
        gc_rows = []
        for _, _, _, gc_all in gates:
            gc_diag = jnp.concatenate(
                [jnp.where(eye_w, gc_all[:, h * HEAD_DIM:(h + 1) * HEAD_DIM], 0.0) for h in range(N_HEADS)], axis=1)
            gc_rows.append(_dot_sel(all_ones, gc_diag))

        probs = []
        for (c, r0, beta_all, gc_all), gc_row_all in zip(gates, gc_rows):
            for h in range(N_HEADS):
                hs = slice(h * HEAD_DIM, (h + 1) * HEAD_DIM)
                beta = beta_all[:, h:h + 1]
                gc = gc_all[:, hs]
                decay = jnp.exp(jnp.where(incl_w, gc - gc_row_all[:, hs], -jnp.inf))[:, :CHUNK]
                q = qn_ref[pl.ds(r0, CHUNK), hs]
                k = kn_ref[pl.ds(r0, CHUNK), hs]
                v = vn_ref[pl.ds(r0, CHUNK), hs]
                kb = k * beta
                k16 = k.astype(BF16)
                a_mat = jnp.where(strict, _dot_nt(kb.astype(BF16), k16) * decay, 0.0)
                qk = (_dot_nt(q.astype(BF16), k16) * decay).astype(BF16)
                probs.append(dict(c=c, r0=r0, h=h, hs=hs, gc=gc, q=q, k=k, kb=kb, vb=v * beta, a=a_mat, qk=qk))

        invs = _unit_lower_inverses([p["a"] for p in probs], row, col)
        uws = [_dot_split(inv, jnp.concatenate([p["vb"], p["kb"] * jnp.exp(p["gc"])], axis=1))
               for inv, p in zip(invs, probs)]
        for p, uw in zip(probs, uws):
            r0, hs, gc, h = p["r0"], p["hs"], p["gc"], p["h"]
            g_last = gc[CHUNK - 1:CHUNK, :]
            u_ref[pl.ds(r0, CHUNK), hs] = uw[:, :HEAD_DIM]
            w_ref[pl.ds(r0, CHUNK), hs] = uw[:, HEAD_DIM:].astype(BF16)
            qg_ref[pl.ds(r0, CHUNK), hs] = (p["q"] * jnp.exp(gc)).astype(BF16)
            kd_ref[pl.ds(r0, CHUNK), hs] = (p["k"] * jnp.exp(g_last - gc)).astype(BF16)
            qk_ref[pl.ds(r0, CHUNK), h * HEAD_DIM:h * HEAD_DIM + CHUNK] = p["qk"]
            el_ref[pl.ds(pl.multiple_of(p["c"] * SUBLANES, SUBLANES), SUBLANES), hs] = jnp.broadcast_to(
                jnp.exp(g_last), (SUBLANES, HEAD_DIM))
        return 0

    lax.fori_loop(0, ts // (CHUNK * GDN_CHUNKS_PER_STEP), prepare, 0)

    def scan(c, _):
        r0 = pl.multiple_of(c * CHUNK, CHUNK)
        heads = [slice(h * HEAD_DIM, (h + 1) * HEAD_DIM) for h in range(N_HEADS)]
        states = [st_ref[h] for h in range(N_HEADS)]
        s16 = [s.astype(BF16) for s in states]
        ws = [_dot(w_ref[pl.ds(r0, CHUNK), hs], s) for hs, s in zip(heads, s16)]
        inter = [_dot(qg_ref[pl.ds(r0, CHUNK), hs], s) for hs, s in zip(heads, s16)]
        v_new = [(u_ref[pl.ds(r0, CHUNK), hs] - x).astype(BF16) for hs, x in zip(heads, ws)]
        upd = [_dot_tn(kd_ref[pl.ds(r0, CHUNK), hs], vn) for hs, vn in zip(heads, v_new)]
        for h, hs in enumerate(heads):
            e_last = el_ref[pl.ds(pl.multiple_of(c * SUBLANES, SUBLANES), SUBLANES), hs][0:1, :]
            st_ref[h] = states[h] * e_last + upd[h]
        intra = [_dot(qk_ref[pl.ds(r0, CHUNK), h * HEAD_DIM:h * HEAD_DIM + CHUNK], vn)
                 for h, vn in enumerate(v_new)]
        for h, hs in enumerate(heads):
            zg = z_ref[pl.ds(r0, CHUNK), hs].astype(F32)
            o_ref[pl.ds(r0, CHUNK), hs] = (_rms(inter[h] + intra[h], gain_ref[...]) * _silu(zg)).astype(o_ref.dtype)
        return 0

    lax.fori_loop(0, ts // CHUNK, scan, 0)


def _gdn(proj, small, conv_w, head_params, out_gain, batch, seq, *, ts):
    nt = seq // ts
    kern = functools.partial(_gdn_kernel, ts=ts)
    qkv_blk = COL_GD // (3 * GROUP)
    z_blk = (COL_GD + 3 * GROUP) // GROUP
    return pl.pallas_call(
        kern,
        grid=(batch, nt),
        in_specs=[
            pl.BlockSpec((ts, 3 * GROUP), lambda b, t: (b * nt + t, qkv_blk)),
            pl.BlockSpec((ts, GROUP), lambda b, t: (b * nt + t, z_blk)),
            pl.BlockSpec((ts, SMALL_W), lambda b, t: (b * nt + t, 0)),
            pl.BlockSpec((CONV_TAPS, 3 * GROUP), lambda b, t: (0, 0)),
            pl.BlockSpec(head_params.shape, lambda b, t: (0, 0)),
            pl.BlockSpec((1, HEAD_DIM), lambda b, t: (0, 0)),
        ],
        out_specs=pl.BlockSpec((ts, GROUP), lambda b, t: (b * nt + t, 0)),
        out_shape=jax.ShapeDtypeStruct((batch * seq, GROUP), BF16),
        scratch_shapes=[
            pltpu.VMEM((ts + CONV_HALO, 3 * GROUP), F32),
            pltpu.VMEM((ts, GROUP), F32),
            pltpu.VMEM((ts, GROUP), F32),
            pltpu.VMEM((ts, GROUP), F32),
            pltpu.VMEM((ts, GROUP), F32),
            pltpu.VMEM((ts, GROUP), BF16),
            pltpu.VMEM((ts, GROUP), BF16),
            pltpu.VMEM((ts, GROUP), BF16),
            pltpu.VMEM((ts, GROUP), BF16),
            pltpu.VMEM((ts // CHUNK * SUBLANES, GROUP), F32),
            pltpu.VMEM((N_HEADS, HEAD_DIM, HEAD_DIM), F32),
        ],
        compiler_params=_cparams("parallel", "arbitrary"),
        name="gated_deltanet",
    )(proj, proj, small, conv_w, head_params, out_gain.reshape(1, HEAD_DIM))


def _gla_kernel(qk_ref, v_ref, og_ref, sm_ref, w2_ref, gb_ref, gain_ref, o_ref,
                att_ref, qin_ref, kout_ref, el_ref, st_ref, *, ts):
    t = pl.program_id(1)

    @pl.when(t == 0)
    def _():
        st_ref[...] = jnp.zeros_like(st_ref)

    row, col = _chunk_iotas()
    lower_ones = jnp.where(row >= col, 1.0, 0.0).astype(BF16)
    eye = row == col
    row_w, lane = _chunk_iotas(HEAD_DIM)
    qk_scale = GLA_DK ** -0.5

    levels = []
    for level in range(CHUNK.bit_length() - 1):
        ref_row = ((row >> (level + 1)) << (level + 1)) + (1 << level)
        pick = jnp.where(col == ref_row, 1.0, 0.0).astype(BF16)
        upper_w = ((row_w >> level) & 1) == 1
        levels.append((pick, _lower_left_blocks(row, col, level), upper_w))

    n_pairs = N_HEADS * GLA_DK // HEAD_DIM
    heads_per_pair = HEAD_DIM // GLA_DK

    def prepare(step, _):
        cums = []
        for sub in range(GLA_CHUNKS_PER_STEP):
            c = step * GLA_CHUNKS_PER_STEP + sub
            r0 = pl.multiple_of(c * CHUNK, CHUNK)
            sm = sm_ref[pl.ds(r0, CHUNK), :].astype(BF16)
            gate = _dot(sm, w2_ref[...]) + gb_ref[...]
            log_a = (gate - _softplus(gate)) * (1.0 / GLA_TAU)
            cums.append((c, r0, _dot_sel(lower_ones, log_a)))
        probs = []
        for c, r0, b_all in cums:
            for pair in range(n_pairs):
                ps = slice(pair * HEAD_DIM, (pair + 1) * HEAD_DIM)
                b = b_all[:, ps]
                q2 = qk_ref[pl.ds(r0, CHUNK), ps].astype(F32) * qk_scale
                k2 = qk_ref[pl.ds(r0, CHUNK),
                            GROUP // 2 + pair * HEAD_DIM:GROUP // 2 + (pair + 1) * HEAD_DIM].astype(F32)
                probs.append(dict(c=c, r0=r0, pair=pair, ps=ps, b=b, q2=q2, k2=k2,
                                  b_refs=[_dot_sel(pick, b) for pick, _, _ in levels]))
        for p in probs:
            b, q2, k2 = p["b"], p["q2"], p["k2"]
            p["q_lv"] = [jnp.where(upper_w, q2 * jnp.exp(jnp.where(upper_w, b - b_ref, 0.0)), 0.0)
                         for (_, _, upper_w), b_ref in zip(levels, p["b_refs"])]
            p["k_lv"] = [jnp.where(upper_w, 0.0, k2 * jnp.exp(jnp.where(upper_w, 0.0, b_ref - b))).astype(BF16)
                         for (_, _, upper_w), b_ref in zip(levels, p["b_refs"])]
        for p in probs:
            r0, b, q2, k2 = p["r0"], p["b"], p["q2"], p["k2"]
            b_last = b[CHUNK - 1:CHUNK, :]
            q_in = (q2 * jnp.exp(b)).astype(BF16)
            k_out = (k2 * jnp.exp(b_last - b)).astype(BF16)
            diag_qk = q2 * k2
            for r in range(heads_per_pair):
                h = p["pair"] * heads_per_pair + r
                hs = slice(h * HEAD_DIM, (h + 1) * HEAD_DIM)
                mine = (lane >= r * GLA_DK) & (lane < (r + 1) * GLA_DK)
                att = jnp.where(eye, jnp.sum(jnp.where(mine, diag_qk, 0.0), axis=1, keepdims=True), 0.0)
                for (_, pair_mask, _), ql, kl in zip(levels, p["q_lv"], p["k_lv"]):
                    s = _dot_nt(jnp.where(mine, ql, 0.0).astype(BF16), kl)
                    att = att + jnp.where(pair_mask, s, 0.0)
                att_ref[pl.ds(r0, CHUNK), h * HEAD_DIM:h * HEAD_DIM + CHUNK] = att.astype(BF16)
                qin_ref[pl.ds(r0, CHUNK), hs] = jnp.where(mine, q_in, jnp.zeros_like(q_in))
                kout_ref[pl.ds(r0, CHUNK), hs] = jnp.where(mine, k_out, jnp.zeros_like(k_out))
            el_ref[pl.ds(pl.multiple_of(p["c"] * SUBLANES, SUBLANES), SUBLANES), p["ps"]] = jnp.broadcast_to(
                jnp.exp(b_last), (SUBLANES, HEAD_DIM))
        return 0

    lax.fori_loop(0, ts // (CHUNK * GLA_CHUNKS_PER_STEP), prepare, 0)

    def scan(c, _):
        r0 = pl.multiple_of(c * CHUNK, CHUNK)
        heads = [slice(h * HEAD_DIM, (h + 1) * HEAD_DIM) for h in range(N_HEADS)]
        states = [st_ref[h] for h in range(N_HEADS)]
        values = [v_ref[pl.ds(r0, CHUNK), hs] for hs in heads]
        inter = [_dot_nt(qin_ref[pl.ds(r0, CHUNK), hs], s.astype(BF16)) for hs, s in zip(heads, states)]
        intra = [_dot(att_ref[pl.ds(r0, CHUNK), h * HEAD_DIM:h * HEAD_DIM + CHUNK], v)
                 for h, v in enumerate(values)]
        upd = [_dot_tn(v, kout_ref[pl.ds(r0, CHUNK), hs]) for hs, v in zip(heads, values)]
        for h, hs in enumerate(heads):
            ps = slice((h // heads_per_pair) * HEAD_DIM, (h // heads_per_pair + 1) * HEAD_DIM)
            e_last = el_ref[pl.ds(pl.multiple_of(c * SUBLANES, SUBLANES), SUBLANES), ps][0:1, :]
            st_ref[h] = states[h] * e_last + upd[h]
            og = og_ref[pl.ds(r0, CHUNK), hs].astype(F32)
            o_ref[pl.ds(r0, CHUNK), hs] = (_rms(inter[h] + intra[h], gain_ref[...]) * _silu(og)).astype(o_ref.dtype)
        return 0

    lax.fori_loop(0, ts // CHUNK, scan, 0)


def _gla(proj, small, w2_padded, gate_bias, out_gain, batch, seq, *, ts):
    nt = seq // ts
    kern = functools.partial(_gla_kernel, ts=ts)
    qk_blk = COL_GL // GROUP
    return pl.pallas_call(
        kern,
        grid=(batch, nt),
        in_specs=[
            pl.BlockSpec((ts, GROUP), lambda b, t: (b * nt + t, qk_blk)),
            pl.BlockSpec((ts, GROUP), lambda b, t: (b * nt + t, qk_blk + 1)),
            pl.BlockSpec((ts, GROUP), lambda b, t: (b * nt + t, qk_blk + 2)),
            pl.BlockSpec((ts, SMALL_W), lambda b, t: (b * nt + t, 0)),
            pl.BlockSpec(w2_padded.shape, lambda b, t: (0, 0)),
            pl.BlockSpec((1, N_HEADS * GLA_DK), lambda b, t: (0, 0)),
            pl.BlockSpec((1, HEAD_DIM), lambda b, t: (0, 0)),
        ],
        out_specs=pl.BlockSpec((ts, GROUP), lambda b, t: (b * nt + t, 0)),
        out_shape=jax.ShapeDtypeStruct((batch * seq, GROUP), BF16),
        scratch_shapes=[
            pltpu.VMEM((ts, GROUP), BF16),
            pltpu.VMEM((ts, GROUP), BF16),
            pltpu.VMEM((ts, GROUP), BF16),
            pltpu.VMEM((ts // CHUNK * SUBLANES, N_HEADS * GLA_DK), F32),
            pltpu.VMEM((N_HEADS, HEAD_DIM, HEAD_DIM), F32),
        ],
        compiler_params=_cparams("parallel", "arbitrary"),
        name="gla",
    )(proj, proj, proj, small, w2_padded, gate_bias.reshape(1, -1), out_gain.reshape(1, HEAD_DIM))


def _out_proj_kernel(x_ref, m0_ref, m1_ref, m2_ref, m3_ref, w_ref, o_ref):
    acc = x_ref[...]
    for g, m_ref in enumerate((m0_ref, m1_ref, m2_ref, m3_ref)):
        acc = acc + _dot(m_ref[...], w_ref[g * GROUP:(g + 1) * GROUP, :])
    o_ref[...] = acc


def _out_proj(x, mixes, w_out, layer, *, tm):
    t, d = x.shape
    mix_spec = pl.BlockSpec((tm, GROUP), lambda i: (i, 0))
    return pl.pallas_call(
        _out_proj_kernel,
        grid=(t // tm,),
        in_specs=[pl.BlockSpec((tm, d), lambda i: (i, 0)), mix_spec, mix_spec, mix_spec, mix_spec,
                  pl.BlockSpec((None,) + w_out.shape[1:], lambda i: (layer, 0, 0))],
        out_specs=pl.BlockSpec((tm, d), lambda i: (i, 0)),
        out_shape=jax.ShapeDtypeStruct((t, d), F32),
        compiler_params=_cparams("parallel"),
        name="out_proj_residual",
    )(x, *mixes, w_out)


def _ffn_kernel(x_ref, g_ref, wg_ref, wu_ref, wd_ref, o_ref, h_ref, acc_ref):
    f = pl.program_id(1)

    @pl.when(f == 0)
    def _():
        h_ref[...] = _rms(x_ref[...], g_ref[...]).astype(BF16)
        acc_ref[...] = jnp.zeros_like(acc_ref)

    h = h_ref[...]
    hid = _silu(_dot(h, wg_ref[...])) * _dot(h, wu_ref[...])
    acc_ref[...] += _dot(hid.astype(BF16), wd_ref[...])

    @pl.when(f == pl.num_programs(1) - 1)
    def _():
        o_ref[...] = x_ref[...] + acc_ref[...]


def _ffn(x, gain, w_gate, w_up, w_down, layer, *, tm, tf):
    t, d = x.shape
    hidden = w_gate.shape[2]
    return pl.pallas_call(
        _ffn_kernel,
        grid=(t // tm, hidden // tf),
        in_specs=[
            pl.BlockSpec((tm, d), lambda i, f: (i, 0)),
            pl.BlockSpec((1, d), lambda i, f: (0, 0)),
            pl.BlockSpec((None, d, tf), lambda i, f: (layer, 0, f)),
            pl.BlockSpec((None, d, tf), lambda i, f: (layer, 0, f)),
            pl.BlockSpec((None, tf, d), lambda i, f: (layer, f, 0)),
        ],
        out_specs=pl.BlockSpec((tm, d), lambda i, f: (i, 0)),
        out_shape=jax.ShapeDtypeStruct((t, d), F32),
        scratch_shapes=[pltpu.VMEM((tm, d), BF16), pltpu.VMEM((tm, d), F32)],
        compiler_params=_cparams("parallel", "arbitrary"),
        name="ffn_residual",
    )(x, gain.reshape(1, d), w_gate, w_up, w_down)


def _final_norm_kernel(x_ref, g_ref, o_ref):
    o_ref[...] = _rms(x_ref[...], g_ref[...])


def _final_norm(x, gain, *, tm):
    t, d = x.shape
    return pl.pallas_call(
        _final_norm_kernel,
        grid=(t // tm,),
        in_specs=[pl.BlockSpec((tm, d), lambda i: (i, 0)), pl.BlockSpec((1, d), lambda i: (0, 0))],
        out_specs=pl.BlockSpec((tm, d), lambda i: (i, 0)),
        out_shape=jax.ShapeDtypeStruct((t, d), F32),
        compiler_params=_cparams("parallel"),
        name="final_norm",
    )(x, gain.reshape(1, d))


def _pack_w_in(w_in):
    gd_small = COL_GD + 4 * GROUP
    df0 = gd_small + 2 * N_HEADS
    gl0 = df0 + 3 * GROUP
    rank0 = gl0 + 3 * GROUP
    pad = jnp.zeros(w_in.shape[:-1] + (IN_COLS_PACKED - COL_SMALL - 2 * N_HEADS - GLA_RANK,), BF16)
    parts = [w_in[..., :gd_small], w_in[..., df0:rank0], w_in[..., gd_small:df0], w_in[..., rank0:rank0 + GLA_RANK]]
    return jnp.concatenate([part.astype(BF16) for part in parts] + [pad], axis=-1)


def _mixer_layer(x, l, p, big, batch, seq):
    lam_init = 0.8 - 0.6 * math.exp(-0.3 * l)
    proj, small = _norm_in_proj(x, p["attn_norm"], big["w_in"], l, tm=1024)
    o_sb = _sb_attention(proj, batch, seq)
    head_params = jnp.zeros((SUBLANES, SMALL_W), F32)
    head_params = head_params.at[0, N_HEADS:2 * N_HEADS].set(p["gdn_a_log"])
    head_params = head_params.at[1, N_HEADS:2 * N_HEADS].set(p["gdn_dt_bias"])
    o_gd = _gdn(proj, small, p["gdn_conv_w"], head_params, p["gdn_out_norm"], batch, seq, ts=512)
    lam_vecs = jnp.stack([p["diff_lam_q1"], p["diff_lam_k1"], p["diff_lam_q2"], p["diff_lam_k2"]])
    o_df = _diff_attention(proj, lam_vecs, p["diff_out_norm"], lam_init, batch, seq)
    w2 = jnp.zeros((SMALL_W, N_HEADS * GLA_DK), F32).at[2 * N_HEADS:2 * N_HEADS + GLA_RANK].set(p["gla_gate_w2"])
    o_gl = _gla(proj, small, w2.astype(BF16), p["gla_gate_b"], p["gla_out_norm"], batch, seq, ts=512)
    return _out_proj(x, (o_sb, o_gd, o_df, o_gl), big["w_out"], l, tm=512)


def kernel(x, attn_norm, w_in, gdn_conv_w, gdn_a_log, gdn_dt_bias, gdn_out_norm, diff_lam_q1, diff_lam_k1,
           diff_lam_q2, diff_lam_k2, diff_out_norm, gla_gate_w2, gla_gate_b, gla_out_norm, w_out, ffn_norm,
           w_gate, w_up, w_down, final_norm):
    batch, seq, d = x.shape
    depth = w_in.shape[0]
    stacked = dict(attn_norm=attn_norm, gdn_conv_w=gdn_conv_w, gdn_a_log=gdn_a_log,
                   gdn_dt_bias=gdn_dt_bias, gdn_out_norm=gdn_out_norm, diff_lam_q1=diff_lam_q1,
                   diff_lam_k1=diff_lam_k1, diff_lam_q2=diff_lam_q2, diff_lam_k2=diff_lam_k2,
                   diff_out_norm=diff_out_norm, gla_gate_w2=gla_gate_w2, gla_gate_b=gla_gate_b,
                   gla_out_norm=gla_out_norm)
    big = dict(w_in=_pack_w_in(w_in), w_out=w_out.astype(BF16), w_gate=w_gate.astype(BF16),
               w_up=w_up.astype(BF16), w_down=w_down.astype(BF16))
    xt = x.reshape(batch * seq, d)
    for l in range(depth):
        p = {name: arr[l] for name, arr in stacked.items()}
        xt = _mixer_layer(xt, l, p, big, batch, seq)
        xt = _ffn(xt, ffn_norm[l], big["w_gate"], big["w_up"], big["w_down"], l, tm=512, tf=512)
    return _final_norm(xt, final_norm, tm=512).reshape(batch, seq, d)
```

```python
import functools
import math

import jax
import jax.numpy as jnp
from jax import lax
from jax.experimental import pallas as pl
from jax.experimental.pallas import tpu as pltpu

F32 = jnp.float32
BF16 = jnp.bfloat16

NORM_EPS = 1e-6
LOG2_E = math.log2(math.e)
SB_DEAD_LOG2 = 110.0 * LOG2_E
N_HEADS = 4
HEAD_DIM = 128
GROUP = N_HEADS * HEAD_DIM
CHUNK = 64
CONV_TAPS = 4
GLA_DK = 64
GLA_RANK = 16
GLA_TAU = 16.0
ATT_BLOCK = 256
CONV_HALO = 8
SUBLANES = 8
GDN_CHUNKS_PER_STEP = 2
GLA_CHUNKS_PER_STEP = 2
DIFF_HEADS_PER_STEP = 4
SB_HEADS_PER_STEP = 2
COL_SB = 0
COL_GD = 3 * GROUP
COL_DF = 7 * GROUP
COL_GL = 10 * GROUP
COL_SMALL = 13 * GROUP
SMALL_W = 128
IN_COLS_PACKED = 14 * GROUP
VMEM_LIMIT = 56 * 1024 * 1024


def _cparams(*sem):
    return pltpu.CompilerParams(dimension_semantics=sem, vmem_limit_bytes=VMEM_LIMIT)


def _dot(a, b):
    return jnp.dot(a, b, preferred_element_type=F32)


def _dot_nt(a, b):
    return lax.dot_general(a, b, (((1,), (1,)), ((), ())), preferred_element_type=F32)


def _dot_tn(a, b):
    return lax.dot_general(a, b, (((0,), (0,)), ((), ())), preferred_element_type=F32)


def _split(x):
    hi = x.astype(BF16)
    return hi, (x - hi.astype(F32)).astype(BF16)


def _dot_sel(sel16, x):
    hi, lo = _split(x)
    return _dot(sel16, hi) + _dot(sel16, lo)


def _dot_split(a, b):
    ah, al = _split(a)
    bh, bl = _split(b)
    return _dot(ah, bh) + (_dot(ah, bl) + _dot(al, bh))


def _rms(x, gain):
    return x * lax.rsqrt(jnp.mean(x * x, axis=-1, keepdims=True) + NORM_EPS) * gain


def _silu(x):
    return x / (1.0 + jnp.exp(-x))


def _sigmoid(x):
    return 1.0 / (1.0 + jnp.exp(-x))


def _softplus(x):
    return jnp.maximum(x, 0.0) + jnp.log(1.0 + jnp.exp(-jnp.abs(x)))


def _norm_matmul_kernel(x_ref, g_ref, w_ref, o_ref, small_ref, h_ref):
    j = pl.program_id(1)
    last = pl.num_programs(1) - 1

    @pl.when(j == 0)
    def _():
        h_ref[...] = _rms(x_ref[...], g_ref[...]).astype(BF16)

    y = _dot(h_ref[...], w_ref[...])
    o_ref[...] = y.astype(o_ref.dtype)

    @pl.when(j == last)
    def _():
        small_ref[...] = y[:, y.shape[1] - small_ref.shape[1]:]


def _norm_in_proj(x, gain, w, layer, *, tm, tn):
    t, d = x.shape
    return pl.pallas_call(
        _norm_matmul_kernel,
        grid=(t // tm, IN_COLS_PACKED // tn),
        in_specs=[
            pl.BlockSpec((tm, d), lambda i, j: (i, 0)),
            pl.BlockSpec((1, d), lambda i, j: (0, 0)),
            pl.BlockSpec((None, d, tn), lambda i, j: (layer, 0, j)),
        ],
        out_specs=[
            pl.BlockSpec((tm, tn), lambda i, j: (i, j)),
            pl.BlockSpec((tm, IN_COLS_PACKED - COL_SMALL), lambda i, j: (i, 0)),
        ],
        out_shape=[jax.ShapeDtypeStruct((t, IN_COLS_PACKED), BF16),
                   jax.ShapeDtypeStruct((t, IN_COLS_PACKED - COL_SMALL), F32)],
        scratch_shapes=[pltpu.VMEM((tm, d), BF16)],
        compiler_params=_cparams("parallel", "arbitrary"),
        name="norm_in_proj",
    )(x, gain.reshape(1, d), w)


def _sb_kernel(q_ref, k_ref, v_ref, o_ref, *, blk, scale):
    i = pl.program_id(2)
    heads = [slice(h * HEAD_DIM, (h + 1) * HEAD_DIM) for h in range(SB_HEADS_PER_STEP)]
    row = lax.broadcasted_iota(jnp.int32, (blk, blk), 0)
    col = lax.broadcasted_iota(jnp.int32, (blk, blk), 1)
    suffix = jnp.where(row > col, 1.0, 0.0).astype(BF16)
    lead = jnp.maximum(i - 1, 0)
    col_minus_row = (lax.broadcasted_iota(jnp.int32, (blk, 2 * blk), 1)
                     - lax.broadcasted_iota(jnp.int32, (blk, 2 * blk), 0))
    causal = col_minus_row < (i - lead) * blk

    def span(first_block, nblk, state, masked):
        rows = pl.ds(pl.multiple_of(first_block * blk, blk), nblk * blk)
        parts = [slice(b * blk, (b + 1) * blk) for b in range(nblk)]
        zs = [_dot_nt(q_ref[:, hs], k_ref[rows, hs]) * (scale * LOG2_E) for hs in heads]
        sps = []
        for z in zs:
            sp = jnp.maximum(z, 0.0) + jnp.log2(1.0 + jnp.exp2(-jnp.abs(z)))
            sps.append(jnp.where(causal, sp, 0.0) if masked else sp)
        tails = [[_dot_sel_rhs(sp[:, s], suffix) for s in parts] for sp in sps]
        out = []
        for hs, z, sp, tail, (carry, acc) in zip(heads, zs, sps, tails, state):
            sums = [jnp.sum(sp[:, s], axis=1, keepdims=True) for s in parts]
            offsets = [None] * nblk
            for b in reversed(range(nblk)):
                offsets[b] = carry
                carry = carry + sums[b]
            att = jnp.concatenate(
                [jnp.exp2(z[:, s] - sp[:, s] - tail[b] - offsets[b])
                 for b, s in enumerate(parts)], axis=1)
            if masked:
                att = jnp.where(causal, att, 0.0)
            out.append((carry, acc + _dot(att.astype(BF16), v_ref[rows, hs])))
        return tuple(out)

    init = (jnp.zeros((blk, 1), F32), jnp.zeros((blk, HEAD_DIM), F32))
    state = span(lead, 2, (init,) * len(heads), True)

    def alive(state):
        low = state[0][0]
        for carry, _ in state[1:]:
            low = jnp.minimum(low, carry)
        return jnp.min(low) < SB_DEAD_LOG2

    _, state = lax.while_loop(
        lambda carried: jnp.logical_and(carried[0] < lead // 2, alive(carried[1])),
        lambda carried: (carried[0] + 1, span(lead - 2 - 2 * carried[0], 2, carried[1], False)),
        (jnp.int32(0), state))
    odd_left = jnp.where(alive(state), lead % 2, 0)
    state = lax.fori_loop(0, odd_left, lambda n, st: span(0, 1, st, False), state)
    for hs, (_, acc) in zip(heads, state):
        o_ref[:, hs] = acc.astype(o_ref.dtype)


def _dot_sel_rhs(x, sel16):
    hi, lo = _split(x)
    return _dot(hi, sel16) + _dot(lo, sel16)


def _sb_attention(proj, batch, seq):
    blk = ATT_BLOCK
    nq = seq // blk
    width = SB_HEADS_PER_STEP * HEAD_DIM
    qb, kb, vb = COL_SB // width, (COL_SB + GROUP) // width, (COL_SB + 2 * GROUP) // width
    kern = functools.partial(_sb_kernel, blk=blk, scale=HEAD_DIM ** -0.5)
    return pl.pallas_call(
        kern,
        grid=(batch, N_HEADS // SB_HEADS_PER_STEP, nq),
        in_specs=[
            pl.BlockSpec((blk, width), lambda b, h, i: (b * nq + i, qb + h)),
            pl.BlockSpec((seq, width), lambda b, h, i: (b, kb + h)),
            pl.BlockSpec((seq, width), lambda b, h, i: (b, vb + h)),
        ],
        out_specs=pl.BlockSpec((blk, width), lambda b, h, i: (b * nq + i, h)),
        out_shape=jax.ShapeDtypeStruct((batch * seq, GROUP), BF16),
        compiler_params=_cparams("parallel", "parallel", "arbitrary"),
        name="sb_attention",
    )(proj, proj, proj)


def _diff_kernel(lam_ref, q_ref, k_ref, v_ref, g_ref, o_ref, *, blk, scale, lam_init):
    i = pl.program_id(2)
    half = HEAD_DIM // 2
    lane = lax.broadcasted_iota(jnp.int32, (blk, HEAD_DIM), 1)
    heads = [slice(h * HEAD_DIM, (h + 1) * HEAD_DIM) for h in range(DIFF_HEADS_PER_STEP)]
    qs = []
    for hs in heads:
        q = q_ref[:, hs].astype(F32) * scale
        qs.append(jnp.concatenate([jnp.where(lane < half, q, 0.0), jnp.where(lane >= half, q, 0.0)],
                                  axis=0).astype(BF16))
    lead = jnp.maximum(i - 1, 0)
    row = lax.broadcasted_iota(jnp.int32, (2 * blk, 2 * blk), 0) & (blk - 1)
    col = lax.broadcasted_iota(jnp.int32, (2 * blk, 2 * blk), 1)
    causal = col - row <= (i - lead) * blk

    def softmax(s, state, masked):
        m, l_lanes, acc = state
        if masked:
            s = jnp.where(causal, s, -jnp.inf)
        m_new = jnp.maximum(m, jnp.max(s, axis=1, keepdims=True))
        alpha = jnp.exp(m - m_new)
        p = jnp.exp(s - m_new)
        p_lanes = p[:, 0:HEAD_DIM]
        for c in range(1, s.shape[1] // HEAD_DIM):
            p_lanes = p_lanes + p[:, c * HEAD_DIM:(c + 1) * HEAD_DIM]
        return m_new, alpha * l_lanes + p_lanes, alpha * acc, p.astype(BF16)

    def span(first_block, nblk, states, masked):
        rows = pl.ds(pl.multiple_of(first_block * blk, blk), nblk * blk)
        scores = [_dot_nt(q, k_ref[rows, hs]) for q, hs in zip(qs, heads)]
        out = []
        for s, hs, state in zip(scores, heads, states):
            m, l_lanes, acc, p = softmax(s, state, masked)
            out.append((m, l_lanes, acc + _dot(p, v_ref[rows, hs])))
        return tuple(out)

    init = (jnp.full((2 * blk, 1), -jnp.inf, F32), jnp.zeros((2 * blk, HEAD_DIM), F32),
            jnp.zeros((2 * blk, HEAD_DIM), F32))
    states = span(lead, 2, (init,) * len(heads), True)
    states = lax.fori_loop(0, lead // 2, lambda n, st: span(2 * n, 2, st, False), states)
    states = lax.fori_loop(0, lead % 2, lambda n, st: span(lead - 1, 1, st, False), states)

    lam_v = lam_ref[...]
    d1 = jnp.sum(lam_v[0:1, :] * lam_v[1:2, :], axis=1, keepdims=True)
    d2 = jnp.sum(lam_v[2:3, :] * lam_v[3:4, :], axis=1, keepdims=True)
    lam = jnp.exp(d1) - jnp.exp(d2) + lam_init
    for hs, (_, l_lanes, acc) in zip(heads, states):
        norm = acc / jnp.sum(l_lanes, axis=1, keepdims=True)
        out = norm[:blk] - lam * norm[blk:]
        o_ref[:, hs] = (_rms(out, g_ref[...]) * (1.0 - lam_init)).astype(o_ref.dtype)


def _diff_attention(proj, lam_vecs, out_gain, lam_init, batch, seq):
    blk = ATT_BLOCK
    nq = seq // blk
    width = DIFF_HEADS_PER_STEP * HEAD_DIM
    qb, kb, vb = COL_DF // width, (COL_DF + GROUP) // width, (COL_DF + 2 * GROUP) // width
    kern = functools.partial(_diff_kernel, blk=blk, scale=(HEAD_DIM // 2) ** -0.5, lam_init=lam_init)
    return pl.pallas_call(
        kern,
        grid=(batch, N_HEADS // DIFF_HEADS_PER_STEP, nq),
        in_specs=[
            pl.BlockSpec(lam_vecs.shape, lambda b, h, i: (0, 0)),
            pl.BlockSpec((blk, width), lambda b, h, i: (b * nq + i, qb + h)),
            pl.BlockSpec((seq, width), lambda b, h, i: (b, kb + h)),
            pl.BlockSpec((seq, width), lambda b, h, i: (b, vb + h)),
            pl.BlockSpec((1, HEAD_DIM), lambda b, h, i: (0, 0)),
        ],
        out_specs=pl.BlockSpec((blk, width), lambda b, h, i: (b * nq + i, h)),
        out_shape=jax.ShapeDtypeStruct((batch * seq, GROUP), BF16),
        compiler_params=_cparams("parallel", "parallel", "arbitrary"),
        name="diff_attention",
    )(lam_vecs, proj, proj, proj, out_gain.reshape(1, HEAD_DIM))


def _chunk_iotas(width=CHUNK):
    row = lax.broadcasted_iota(jnp.int32, (CHUNK, width), 0)
    col = lax.broadcasted_iota(jnp.int32, (CHUNK, width), 1)
    return row, col


def _lower_left_blocks(row, col, level):
    same_pair = (row >> (level + 1)) == (col >> (level + 1))
    return same_pair & (((row >> level) & 1) == 1) & (((col >> level) & 1) == 0)


def _unit_lower_inverses(a_list, row, col):
    eye = jnp.where(row == col, 1.0, 0.0).astype(F32)
    invs = [eye - jnp.where(_lower_left_blocks(row, col, 0), a, 0.0) for a in a_list]
    for level in range(1, CHUNK.bit_length() - 1):
        mask = _lower_left_blocks(row, col, level)
        left = [_dot_split(inv, jnp.where(mask, a, 0.0)) for inv, a in zip(invs, a_list)]
        invs = [inv - _dot_split(x, inv) for inv, x in zip(invs, left)]
    return invs


def _gdn_kernel(x_ref, z_ref, sm_ref, cw_ref, hp_ref, gain_ref, o_ref,
                xs_ref, qn_ref, kn_ref, vn_ref, u_ref, w_ref, qg_ref, kd_ref, qk_ref, el_ref, st_ref, *, ts):
    t = pl.program_id(1)

    @pl.when(t == 0)
    def _():
        xs_ref[0:CONV_HALO, :] = jnp.zeros((CONV_HALO, 3 * GROUP), F32)
        st_ref[...] = jnp.zeros_like(st_ref)

    xs_ref[CONV_HALO:CONV_HALO + ts, :] = x_ref[...].astype(F32)

    for part, dst in enumerate((qn_ref, kn_ref, vn_ref)):
        for h in range(N_HEADS):
            c0 = part * GROUP + h * HEAD_DIM
            acc = jnp.zeros((ts, HEAD_DIM), F32)
            for j in range(CONV_TAPS):
                off = CONV_HALO - (CONV_TAPS - 1) + j
                acc = acc + xs_ref[off:off + ts, c0:c0 + HEAD_DIM] * cw_ref[j:j + 1, c0:c0 + HEAD_DIM]
            y = _silu(acc)
            if part < 2:
                y = y * lax.rsqrt(jnp.sum(y * y, axis=-1, keepdims=True) + NORM_EPS)
            if part == 0:
                y = y * (HEAD_DIM ** -0.5)
            dst[:, h * HEAD_DIM:(h + 1) * HEAD_DIM] = y

    xs_ref[0:CONV_HALO, :] = xs_ref[ts:ts + CONV_HALO, :]

    row, col = _chunk_iotas()
    strict = row > col
    row_w, col_w = _chunk_iotas(HEAD_DIM)
    incl_w = row_w >= col_w
    eye_w = row_w == col_w
    lower_ones = jnp.where(row >= col, 1.0, 0.0).astype(BF16)
    all_ones = jnp.ones((CHUNK, CHUNK), BF16)

    def prepare(pair, _):
        gates = []
        for sub in range(GDN_CHUNKS_PER_STEP):
            c = pair * GDN_CHUNKS_PER_STEP + sub
            r0 = pl.multiple_of(c * CHUNK, CHUNK)
            sm = sm_ref[pl.ds(r0, CHUNK), :]
            beta_all = _sigmoid(sm)
            g_all = -jnp.exp(hp_ref[0:1, :]) * _softplus(sm + hp_ref[1:2, :])
            g_b = jnp.concatenate(
                [jnp.broadcast_to(g_all[:, N_HEADS + h:N_HEADS + h + 1], (CHUNK, HEAD_DIM))
                 for h in range(N_HEADS)], axis=1)
            gc_all = _dot_sel(lower_ones, g_b)
            gates.append((c, r0, beta_all, gc_all))
        gc_rows = []
        for _, _, _, gc_all in gates:
            gc_diag = jnp.concatenate(
                [jnp.where(eye_w, gc_all[:, h * HEAD_DIM:(h + 1) * HEAD_DIM], 0.0) for h in range(N_HEADS)], axis=1)
            gc_rows.append(_dot_sel(all_ones, gc_diag))

        probs = []
        for (c, r0, beta_all, gc_all), gc_row_all in zip(gates, gc_rows):
            for h in range(N_HEADS):
                hs = slice(h * HEAD_DIM, (h + 1) * HEAD_DIM)
                beta = beta_all[:, h:h + 1]
                gc = gc_all[:, hs]
                decay = jnp.exp(jnp.where(incl_w, gc - gc_row_all[:, hs], -jnp.inf))[:, :CHUNK]
                q = qn_ref[pl.ds(r0, CHUNK), hs]
                k = kn_ref[pl.ds(r0, CHUNK), hs]
                v = vn_ref[pl.ds(r0, CHUNK), hs]
                kb = k * beta
                k16 = k.astype(BF16)
                a_mat = jnp.where(strict, _dot_nt(kb.astype(BF16), k16) * decay, 0.0)
                qk = (_dot_nt(q.astype(BF16), k16) * decay).astype(BF16)
                probs.append(dict(c=c, r0=r0, h=h, hs=hs, gc=gc, q=q, k=k, kb=kb, vb=v * beta, a=a_mat, qk=qk))

        invs = _unit_lower_inverses([p["a"] for p in probs], row, col)
        uws = [_dot_split(inv, jnp.concatenate([p["vb"], p["kb"] * jnp.exp(p["gc"])], axis=1))
               for inv, p in zip(invs, probs)]
        for p, uw in zip(probs, uws):
            r0, hs, gc, h = p["r0"], p["hs"], p["gc"], p["h"]
            g_last = gc[CHUNK - 1:CHUNK, :]
            u_ref[pl.ds(r0, CHUNK), hs] = uw[:, :HEAD_DIM]
            w_ref[pl.ds(r0, CHUNK), hs] = uw[:, HEAD_DIM:].astype(BF16)
            qg_ref[pl.ds(r0, CHUNK), hs] = (p["q"] * jnp.exp(gc)).astype(BF16)
            kd_ref[pl.ds(r0, CHUNK), hs] = (p["k"] * jnp.exp(g_last - gc)).astype(BF16)
            qk_ref[pl.ds(r0, CHUNK), h * HEAD_DIM:h * HEAD_DIM + CHUNK] = p["qk"]
            el_ref[pl.ds(pl.multiple_of(p["c"] * SUBLANES, SUBLANES), SUBLANES), hs] = jnp.broadcast_to(
                jnp.exp(g_last), (SUBLANES, HEAD_DIM))
        return 0

    lax.fori_loop(0, ts // (CHUNK * GDN_CHUNKS_PER_STEP), prepare, 0)

    def scan(c, _):
        r0 = pl.multiple_of(c * CHUNK, CHUNK)
        heads = [slice(h * HEAD_DIM, (h + 1) * HEAD_DIM) for h in range(N_HEADS)]
        states = [st_ref[h] for h in range(N_HEADS)]
        s16 = [s.astype(BF16) for s in states]
        ws = [_dot(w_ref[pl.ds(r0, CHUNK), hs], s) for hs, s in zip(heads, s16)]
        inter = [_dot(qg_ref[pl.ds(r0, CHUNK), hs], s) for hs, s in zip(heads, s16)]
        v_new = [(u_ref[pl.ds(r0, CHUNK), hs] - x).astype(BF16) for hs, x in zip(heads, ws)]
        upd = [_dot_tn(kd_ref[pl.ds(r0, CHUNK), hs], vn) for hs, vn in zip(heads, v_new)]
        for h, hs in enumerate(heads):
            e_last = el_ref[pl.ds(pl.multiple_of(c * SUBLANES, SUBLANES), SUBLANES), hs][0:1, :]
            st_ref[h] = states[h] * e_last + upd[h]
        intra = [_dot(qk_ref[pl.ds(r0, CHUNK), h * HEAD_DIM:h * HEAD_DIM + CHUNK], vn)
                 for h, vn in enumerate(v_new)]
        for h, hs in enumerate(heads):
            zg = z_ref[pl.ds(r0, CHUNK), hs].astype(F32)
            o_ref[pl.ds(r0, CHUNK), hs] = (_rms(inter[h] + intra[h], gain_ref[...]) * _silu(zg)).astype(o_ref.dtype)
        return 0

    lax.fori_loop(0, ts // CHUNK, scan, 0)


def _gdn(proj, small, conv_w, head_params, out_gain, batch, seq, *, ts):
    nt = seq // ts
    kern = functools.partial(_gdn_kernel, ts=ts)
    qkv_blk = COL_GD // (3 * GROUP)
    z_blk = (COL_GD + 3 * GROUP) // GROUP
    return pl.pallas_call(
        kern,
        grid=(batch, nt),
        in_specs=[
            pl.BlockSpec((ts, 3 * GROUP), lambda b, t: (b * nt + t, qkv_blk)),
            pl.BlockSpec((ts, GROUP), lambda b, t: (b * nt + t, z_blk)),
            pl.BlockSpec((ts, SMALL_W), lambda b, t: (b * nt + t, 0)),
            pl.BlockSpec((CONV_TAPS, 3 * GROUP), lambda b, t: (0, 0)),
            pl.BlockSpec(head_params.shape, lambda b, t: (0, 0)),
            pl.BlockSpec((1, HEAD_DIM), lambda b, t: (0, 0)),
        ],
        out_specs=pl.BlockSpec((ts, GROUP), lambda b, t: (b * nt + t, 0)),
        out_shape=jax.ShapeDtypeStruct((batch * seq, GROUP), BF16),
        scratch_shapes=[
            pltpu.VMEM((ts + CONV_HALO, 3 * GROUP), F32),
            pltpu.VMEM((ts, GROUP), F32),
            pltpu.VMEM((ts, GROUP), F32),
            pltpu.VMEM((ts, GROUP), F32),
            pltpu.VMEM((ts, GROUP), F32),
            pltpu.VMEM((ts, GROUP), BF16),
            pltpu.VMEM((ts, GROUP), BF16),
            pltpu.VMEM((ts, GROUP), BF16),
            pltpu.VMEM((ts, GROUP), BF16),
            pltpu.VMEM((ts // CHUNK * SUBLANES, GROUP), F32),
            pltpu.VMEM((N_HEADS, HEAD_DIM, HEAD_DIM), F32),
        ],
        compiler_params=_cparams("parallel", "arbitrary"),
        name="gated_deltanet",
    )(proj, proj, small, conv_w, head_params, out_gain.reshape(1, HEAD_DIM))


def _gla_kernel(qk_ref, v_ref, og_ref, sm_ref, w2_ref, gb_ref, gain_ref, o_ref,
                att_ref, qin_ref, kout_ref, el_ref, st_ref, *, ts):
    t = pl.program_id(1)

    @pl.when(t == 0)
    def _():
        st_ref[...] = jnp.zeros_like(st_ref)

    row, col = _chunk_iotas()
    lower_ones = jnp.where(row >= col, 1.0, 0.0).astype(BF16)
    eye = row == col
    row_w, lane = _chunk_iotas(HEAD_DIM)
    qk_scale = GLA_DK ** -0.5

    levels = []
    for level in range(CHUNK.bit_length() - 1):
        ref_row = ((row >> (level + 1)) << (level + 1)) + (1 << level)
        pick = jnp.where(col == ref_row, 1.0, 0.0).astype(BF16)
        upper_w = ((row_w >> level) & 1) == 1
        levels.append((pick, _lower_left_blocks(row, col, level), upper_w))

    n_pairs = N_HEADS * GLA_DK // HEAD_DIM
    heads_per_pair = HEAD_DIM // GLA_DK

    def prepare(step, _):
        cums = []
        for sub in range(GLA_CHUNKS_PER_STEP):
            c = step * GLA_CHUNKS_PER_STEP + sub
            r0 = pl.multiple_of(c * CHUNK, CHUNK)
            sm = sm_ref[pl.ds(r0, CHUNK), :].astype(BF16)
            gate = _dot(sm, w2_ref[...]) + gb_ref[...]
            log_a = (gate - _softplus(gate)) * (1.0 / GLA_TAU)
            cums.append((c, r0, _dot_sel(lower_ones, log_a)))
        probs = []
        for c, r0, b_all in cums:
            for pair in range(n_pairs):
                ps = slice(pair * HEAD_DIM, (pair + 1) * HEAD_DIM)
                b = b_all[:, ps]
                q2 = qk_ref[pl.ds(r0, CHUNK), ps].astype(F32) * qk_scale
                k2 = qk_ref[pl.ds(r0, CHUNK),
                            GROUP // 2 + pair * HEAD_DIM:GROUP // 2 + (pair + 1) * HEAD_DIM].astype(F32)
                probs.append(dict(c=c, r0=r0, pair=pair, ps=ps, b=b, q2=q2, k2=k2,
                                  b_refs=[_dot_sel(pick, b) for pick, _, _ in levels]))
        for p in probs:
            b, q2, k2 = p["b"], p["q2"], p["k2"]
            p["q_lv"] = [jnp.where(upper_w, q2 * jnp.exp(jnp.where(upper_w, b - b_ref, 0.0)), 0.0)
                         for (_, _, upper_w), b_ref in zip(levels, p["b_refs"])]
            p["k_lv"] = [jnp.where(upper_w, 0.0, k2 * jnp.exp(jnp.where(upper_w, 0.0, b_ref - b))).astype(BF16)
                         for (_, _, upper_w), b_ref in zip(levels, p["b_refs"])]
        for p in probs:
            r0, b, q2, k2 = p["r0"], p["b"], p["q2"], p["k2"]
            b_last = b[CHUNK - 1:CHUNK, :]
            q_in = (q2 * jnp.exp(b)).astype(BF16)
            k_out = (k2 * jnp.exp(b_last - b)).astype(BF16)
            diag_qk = q2 * k2
            for r in range(heads_per_pair):
                h = p["pair"] * heads_per_pair + r
                hs = slice(h * HEAD_DIM, (h + 1) * HEAD_DIM)
                mine = (lane >= r * GLA_DK) & (lane < (r + 1) * GLA_DK)
                att = jnp.where(eye, jnp.sum(jnp.where(mine, diag_qk, 0.0), axis=1, keepdims=True), 0.0)
                for (_, pair_mask, _), ql, kl in zip(levels, p["q_lv"], p["k_lv"]):
                    s = _dot_nt(jnp.where(mine, ql, 0.0).astype(BF16), kl)
                    att = att + jnp.where(pair_mask, s, 0.0)
                att_ref[pl.ds(r0, CHUNK), h * HEAD_DIM:h * HEAD_DIM + CHUNK] = att.astype(BF16)
                qin_ref[pl.ds(r0, CHUNK), hs] = jnp.where(mine, q_in, jnp.zeros_like(q_in))
                kout_ref[pl.ds(r0, CHUNK), hs] = jnp.where(mine, k_out, jnp.zeros_like(k_out))
            el_ref[pl.ds(pl.multiple_of(p["c"] * SUBLANES, SUBLANES), SUBLANES), p["ps"]] = jnp.broadcast_to(
                jnp.exp(b_last), (SUBLANES, HEAD_DIM))
        return 0

    lax.fori_loop(0, ts // (CHUNK * GLA_CHUNKS_PER_STEP), prepare, 0)

    def scan(c, _):
        r0 = pl.multiple_of(c * CHUNK, CHUNK)
        heads = [slice(h * HEAD_DIM, (h + 1) * HEAD_DIM) for h in range(N_HEADS)]
        states = [st_ref[h] for h in range(N_HEADS)]
        values = [v_ref[pl.ds(r0, CHUNK), hs] for hs in heads]
        inter = [_dot_nt(qin_ref[pl.ds(r0, CHUNK), hs], s.astype(BF16)) for hs, s in zip(heads, states)]
        intra = [_dot(att_ref[pl.ds(r0, CHUNK), h * HEAD_DIM:h * HEAD_DIM + CHUNK], v)
                 for h, v in enumerate(values)]
        upd = [_dot_tn(v, kout_ref[pl.ds(r0, CHUNK), hs]) for hs, v in zip(heads, values)]
        for h, hs in enumerate(heads):
            ps = slice((h // heads_per_pair) * HEAD_DIM, (h // heads_per_pair + 1) * HEAD_DIM)
            e_last = el_ref[pl.ds(pl.multiple_of(c * SUBLANES, SUBLANES), SUBLANES), ps][0:1, :]
            st_ref[h] = states[h] * e_last + upd[h]
            og = og_ref[pl.ds(r0, CHUNK), hs].astype(F32)
            o_ref[pl.ds(r0, CHUNK), hs] = (_rms(inter[h] + intra[h], gain_ref[...]) * _silu(og)).astype(o_ref.dtype)
        return 0

    lax.fori_loop(0, ts // CHUNK, scan, 0)


def _gla(proj, small, w2_padded, gate_bias, out_gain, batch, seq, *, ts):
    nt = seq // ts
    kern = functools.partial(_gla_kernel, ts=ts)
    qk_blk = COL_GL // GROUP
    return pl.pallas_call(
        kern,
        grid=(batch, nt),
        in_specs=[
            pl.BlockSpec((ts, GROUP), lambda b, t: (b * nt + t, qk_blk)),
            pl.BlockSpec((ts, GROUP), lambda b, t: (b * nt + t, qk_blk + 1)),
            pl.BlockSpec((ts, GROUP), lambda b, t: (b * nt + t, qk_blk + 2)),
            pl.BlockSpec((ts, SMALL_W), lambda b, t: (b * nt + t, 0)),
            pl.BlockSpec(w2_padded.shape, lambda b, t: (0, 0)),
            pl.BlockSpec((1, N_HEADS * GLA_DK), lambda b, t: (0, 0)),
            pl.BlockSpec((1, HEAD_DIM), lambda b, t: (0, 0)),
        ],
        out_specs=pl.BlockSpec((ts, GROUP), lambda b, t: (b * nt + t, 0)),
        out_shape=jax.ShapeDtypeStruct((batch * seq, GROUP), BF16),
        scratch_shapes=[
            pltpu.VMEM((ts, GROUP), BF16),
            pltpu.VMEM((ts, GROUP), BF16),
            pltpu.VMEM((ts, GROUP), BF16),
            pltpu.VMEM((ts // CHUNK * SUBLANES, N_HEADS * GLA_DK), F32),
            pltpu.VMEM((N_HEADS, HEAD_DIM, HEAD_DIM), F32),
        ],
        compiler_params=_cparams("parallel", "arbitrary"),
        name="gla",
    )(proj, proj, proj, small, w2_padded, gate_bias.reshape(1, -1), out_gain.reshape(1, HEAD_DIM))


def _out_proj_kernel(x_ref, m0_ref, m1_ref, m2_ref, m3_ref, w_ref, o_ref):
    acc = x_ref[...]
    for g, m_ref in enumerate((m0_ref, m1_ref, m2_ref, m3_ref)):
        acc = acc + _dot(m_ref[...], w_ref[g * GROUP:(g + 1) * GROUP, :])
    o_ref[...] = acc


def _out_proj(x, mixes, w_out, layer, *, tm):
    t, d = x.shape
    mix_spec = pl.BlockSpec((tm, GROUP), lambda i: (i, 0))
    return pl.pallas_call(
        _out_proj_kernel,
        grid=(t // tm,),
        in_specs=[pl.BlockSpec((tm, d), lambda i: (i, 0)), mix_spec, mix_spec, mix_spec, mix_spec,
                  pl.BlockSpec((None,) + w_out.shape[1:], lambda i: (layer, 0, 0))],
        out_specs=pl.BlockSpec((tm, d), lambda i: (i, 0)),
        out_shape=jax.ShapeDtypeStruct((t, d), F32),
        compiler_params=_cparams("parallel"),
        name="out_proj_residual",
    )(x, *mixes, w_out)


def _ffn_kernel(x_ref, g_ref, wg_ref, wu_ref, wd_ref, og_ref, o_ref, h_ref, acc_ref, *, norm_output):
    f = pl.program_id(1)

    @pl.when(f == 0)
    def _():
        h_ref[...] = _rms(x_ref[...], g_ref[...]).astype(BF16)
        acc_ref[...] = jnp.zeros_like(acc_ref)

    h = h_ref[...]
    hid = _silu(_dot(h, wg_ref[...])) * _dot(h, wu_ref[...])
    acc_ref[...] += _dot(hid.astype(BF16), wd_ref[...])

    @pl.when(f == pl.num_programs(1) - 1)
    def _():
        y = x_ref[...] + acc_ref[...]
        o_ref[...] = _rms(y, og_ref[...]) if norm_output else y


def _ffn(x, gain, w_gate, w_up, w_down, layer, out_gain, *, norm_output, tm, tf):
    t, d = x.shape
    hidden = w_gate.shape[2]
    return pl.pallas_call(
        functools.partial(_ffn_kernel, norm_output=norm_output),
        grid=(t // tm, hidden // tf),
        in_specs=[
            pl.BlockSpec((tm, d), lambda i, f: (i, 0)),
            pl.BlockSpec((1, d), lambda i, f: (0, 0)),
            pl.BlockSpec((None, d, tf), lambda i, f: (layer, 0, f)),
            pl.BlockSpec((None, d, tf), lambda i, f: (layer, 0, f)),
            pl.BlockSpec((None, tf, d), lambda i, f: (layer, f, 0)),
            pl.BlockSpec((1, d), lambda i, f: (0, 0)),
        ],
        out_specs=pl.BlockSpec((tm, d), lambda i, f: (i, 0)),
        out_shape=jax.ShapeDtypeStruct((t, d), F32),
        scratch_shapes=[pltpu.VMEM((tm, d), BF16), pltpu.VMEM((tm, d), F32)],
        compiler_params=_cparams("parallel", "arbitrary"),
        name="ffn_residual",
    )(x, gain.reshape(1, d), w_gate, w_up, w_down, out_gain.reshape(1, d))


def _pack_w_in(w_in):
    gd_small = COL_GD + 4 * GROUP
    df0 = gd_small + 2 * N_HEADS
    gl0 = df0 + 3 * GROUP
    rank0 = gl0 + 3 * GROUP
    pad = jnp.zeros(w_in.shape[:-1] + (IN_COLS_PACKED - COL_SMALL - 2 * N_HEADS - GLA_RANK,), BF16)
    parts = [w_in[..., :gd_small], w_in[..., df0:rank0], w_in[..., gd_small:df0], w_in[..., rank0:rank0 + GLA_RANK]]
    return jnp.concatenate([part.astype(BF16) for part in parts] + [pad], axis=-1)


def _mixer_layer(x, l, p, big, batch, seq):
    lam_init = 0.8 - 0.6 * math.exp(-0.3 * l)
    proj, small = _norm_in_proj(x, p["attn_norm"], big["w_in"], l, tm=1024, tn=1024)
    o_sb = _sb_attention(proj, batch, seq)
    head_params = jnp.zeros((SUBLANES, SMALL_W), F32)
    head_params = head_params.at[0, N_HEADS:2 * N_HEADS].set(p["gdn_a_log"])
    head_params = head_params.at[1, N_HEADS:2 * N_HEADS].set(p["gdn_dt_bias"])
    o_gd = _gdn(proj, small, p["gdn_conv_w"], head_params, p["gdn_out_norm"], batch, seq, ts=512)
    lam_vecs = jnp.stack([p["diff_lam_q1"], p["diff_lam_k1"], p["diff_lam_q2"], p["diff_lam_k2"]])
    o_df = _diff_attention(proj, lam_vecs, p["diff_out_norm"], lam_init, batch, seq)
    w2 = jnp.zeros((SMALL_W, N_HEADS * GLA_DK), F32).at[2 * N_HEADS:2 * N_HEADS + GLA_RANK].set(p["gla_gate_w2"])
    o_gl = _gla(proj, small, w2.astype(BF16), p["gla_gate_b"], p["gla_out_norm"], batch, seq, ts=512)
    return _out_proj(x, (o_sb, o_gd, o_df, o_gl), big["w_out"], l, tm=512)


def kernel(x, attn_norm, w_in, gdn_conv_w, gdn_a_log, gdn_dt_bias, gdn_out_norm, diff_lam_q1, diff_lam_k1,
           diff_lam_q2, diff_lam_k2, diff_out_norm, gla_gate_w2, gla_gate_b, gla_out_norm, w_out, ffn_norm,
           w_gate, w_up, w_down, final_norm):
    batch, seq, d = x.shape
    depth = w_in.shape[0]
    stacked = dict(attn_norm=attn_norm, gdn_conv_w=gdn_conv_w, gdn_a_log=gdn_a_log,
                   gdn_dt_bias=gdn_dt_bias, gdn_out_norm=gdn_out_norm, diff_lam_q1=diff_lam_q1,
                   diff_lam_k1=diff_lam_k1, diff_lam_q2=diff_lam_q2, diff_lam_k2=diff_lam_k2,
                   diff_out_norm=diff_out_norm, gla_gate_w2=gla_gate_w2, gla_gate_b=gla_gate_b,
                   gla_out_norm=gla_out_norm)
    big = dict(w_in=_pack_w_in(w_in), w_out=w_out.astype(BF16), w_gate=w_gate.astype(BF16),
               w_up=w_up.astype(BF16), w_down=w_down.astype(BF16))
    xt = x.reshape(batch * seq, d)
    for l in range(depth):
        p = {name: arr[l] for name, arr in stacked.items()}
        xt = _mixer_layer(xt, l, p, big, batch, seq)
        xt = _ffn(xt, ffn_norm[l], big["w_gate"], big["w_up"], big["w_down"], l, final_norm,
                  norm_output=(l == depth - 1), tm=512, tf=512)
    return xt.reshape(batch, seq, d)
```

```python
import functools
import math

import jax
import jax.numpy as jnp
from jax import lax
from jax.experimental import pallas as pl
from jax.experimental.pallas import tpu as pltpu

F32 = jnp.float32
BF16 = jnp.bfloat16

NORM_EPS = 1e-6
LOG2_E = math.log2(math.e)
SB_DEAD_LOG2 = 110.0 * LOG2_E
N_HEADS = 4
HEAD_DIM = 128
GROUP = N_HEADS * HEAD_DIM
CHUNK = 64
CONV_TAPS = 4
GLA_DK = 64
GLA_RANK = 16
GLA_TAU = 16.0
ATT_BLOCK = 256
CONV_HALO = 8
SUBLANES = 8
GDN_CHUNKS_PER_STEP = 4
GLA_CHUNKS_PER_STEP = 4
DIFF_HEADS_PER_STEP = 4
SB_HEADS_PER_STEP = 2
COL_SB = 0
COL_GD = 3 * GROUP
COL_DF = 7 * GROUP
COL_GL = 10 * GROUP
COL_SMALL = 13 * GROUP
SMALL_W = 128
IN_COLS_PACKED = 14 * GROUP
VMEM_LIMIT = 56 * 1024 * 1024


def _cparams(*sem):
    return pltpu.CompilerParams(dimension_semantics=sem, vmem_limit_bytes=VMEM_LIMIT)


def _dot(a, b):
    return jnp.dot(a, b, preferred_element_type=F32)


def _dot_nt(a, b):
    return lax.dot_general(a, b, (((1,), (1,)), ((), ())), preferred_element_type=F32)


def _dot_tn(a, b):
    return lax.dot_general(a, b, (((0,), (0,)), ((), ())), preferred_element_type=F32)


def _split(x):
    hi = x.astype(BF16)
    return hi, (x - hi.astype(F32)).astype(BF16)


def _dot_sel(sel16, x):
    hi, lo = _split(x)
    return _dot(sel16, hi) + _dot(sel16, lo)


def _dot_unit_left(unit16, x):
    hi, lo = _split(x)
    return _dot(unit16, hi) + _dot(unit16, lo)


def _dot_unit_right(x, unit16):
    hi, lo = _split(x)
    return _dot(hi, unit16) + _dot(lo, unit16)


def _rms(x, gain):
    return x * lax.rsqrt(jnp.mean(x * x, axis=-1, keepdims=True) + NORM_EPS) * gain


def _silu(x):
    return x / (1.0 + jnp.exp(-x))


def _sigmoid(x):
    return 1.0 / (1.0 + jnp.exp(-x))


def _softplus(x):
    return jnp.maximum(x, 0.0) + jnp.log(1.0 + jnp.exp(-jnp.abs(x)))


def _norm_matmul_kernel(x_ref, g_ref, w_ref, o_ref, small_ref, h_ref):
    j = pl.program_id(1)
    last = pl.num_programs(1) - 1

    @pl.when(j == 0)
    def _():
        h_ref[...] = _rms(x_ref[...], g_ref[...]).astype(BF16)

    y = _dot(h_ref[...], w_ref[...])
    o_ref[...] = y.astype(o_ref.dtype)

    @pl.when(j == last)
    def _():
        small_ref[...] = y[:, y.shape[1] - small_ref.shape[1]:]


def _norm_in_proj(x, gain, w, layer, *, tm, tn):
    t, d = x.shape
    return pl.pallas_call(
        _norm_matmul_kernel,
        grid=(t // tm, IN_COLS_PACKED // tn),
        in_specs=[
            pl.BlockSpec((tm, d), lambda i, j: (i, 0)),
            pl.BlockSpec((1, d), lambda i, j: (0, 0)),
            pl.BlockSpec((None, d, tn), lambda i, j: (layer, 0, j)),
        ],
        out_specs=[
            pl.BlockSpec((tm, tn), lambda i, j: (i, j)),
            pl.BlockSpec((tm, IN_COLS_PACKED - COL_SMALL), lambda i, j: (i, 0)),
        ],
        out_shape=[jax.ShapeDtypeStruct((t, IN_COLS_PACKED), BF16),
                   jax.ShapeDtypeStruct((t, IN_COLS_PACKED - COL_SMALL), F32)],
        scratch_shapes=[pltpu.VMEM((tm, d), BF16)],
        compiler_params=_cparams("parallel", "arbitrary"),
        name="norm_in_proj",
    )(x, gain.reshape(1, d), w)


def _sb_kernel(q_ref, k_ref, v_ref, o_ref, *, blk, scale):
    i = pl.program_id(2)
    heads = [slice(h * HEAD_DIM, (h + 1) * HEAD_DIM) for h in range(SB_HEADS_PER_STEP)]
    row = lax.broadcasted_iota(jnp.int32, (blk, blk), 0)
    col = lax.broadcasted_iota(jnp.int32, (blk, blk), 1)
    suffix = jnp.where(row > col, 1.0, 0.0).astype(BF16)
    lead = jnp.maximum(i - 1, 0)
    col_minus_row = (lax.broadcasted_iota(jnp.int32, (blk, 2 * blk), 1)
                     - lax.broadcasted_iota(jnp.int32, (blk, 2 * blk), 0))
    causal = col_minus_row < (i - lead) * blk

    def span(first_block, nblk, state, masked):
        rows = pl.ds(pl.multiple_of(first_block * blk, blk), nblk * blk)
        parts = [slice(b * blk, (b + 1) * blk) for b in range(nblk)]
        zs = [_dot_nt(q_ref[:, hs], k_ref[rows, hs]) * (scale * LOG2_E) for hs in heads]
        sps = []
        for z in zs:
            sp = jnp.maximum(z, 0.0) + jnp.log2(1.0 + jnp.exp2(-jnp.abs(z)))
            sps.append(jnp.where(causal, sp, 0.0) if masked else sp)
        tails = [[_dot_sel_rhs(sp[:, s], suffix) for s in parts] for sp in sps]
        out = []
        for hs, z, sp, tail, (carry, acc) in zip(heads, zs, sps, tails, state):
            sums = [jnp.sum(sp[:, s], axis=1, keepdims=True) for s in parts]
            offsets = [None] * nblk
            for b in reversed(range(nblk)):
                offsets[b] = carry
                carry = carry + sums[b]
            att = jnp.concatenate(
                [jnp.exp2(z[:, s] - sp[:, s] - tail[b] - offsets[b])
                 for b, s in enumerate(parts)], axis=1)
            if masked:
                att = jnp.where(causal, att, 0.0)
            out.append((carry, acc + _dot(att.astype(BF16), v_ref[rows, hs])))
        return tuple(out)

    init = (jnp.zeros((blk, 1), F32), jnp.zeros((blk, HEAD_DIM), F32))
    state = span(lead, 2, (init,) * len(heads), True)

    def alive(state):
        low = state[0][0]
        for carry, _ in state[1:]:
            low = jnp.minimum(low, carry)
        return jnp.min(low) < SB_DEAD_LOG2

    _, state = lax.while_loop(
        lambda carried: jnp.logical_and(carried[0] < lead // 2, alive(carried[1])),
        lambda carried: (carried[0] + 1, span(lead - 2 - 2 * carried[0], 2, carried[1], False)),
        (jnp.int32(0), state))
    odd_left = jnp.where(alive(state), lead % 2, 0)
    state = lax.fori_loop(0, odd_left, lambda n, st: span(0, 1, st, False), state)
    for hs, (_, acc) in zip(heads, state):
        o_ref[:, hs] = acc.astype(o_ref.dtype)


def _dot_sel_rhs(x, sel16):
    hi, lo = _split(x)
    return _dot(hi, sel16) + _dot(lo, sel16)


def _sb_attention(proj, batch, seq):
    blk = ATT_BLOCK
    nq = seq // blk
    width = SB_HEADS_PER_STEP * HEAD_DIM
    qb, kb, vb = COL_SB // width, (COL_SB + GROUP) // width, (COL_SB + 2 * GROUP) // width
    kern = functools.partial(_sb_kernel, blk=blk, scale=HEAD_DIM ** -0.5)
    return pl.pallas_call(
        kern,
        grid=(batch, N_HEADS // SB_HEADS_PER_STEP, nq),
        in_specs=[
            pl.BlockSpec((blk, width), lambda b, h, i: (b * nq + i, qb + h)),
            pl.BlockSpec((seq, width), lambda b, h, i: (b, kb + h)),
            pl.BlockSpec((seq, width), lambda b, h, i: (b, vb + h)),
        ],
        out_specs=pl.BlockSpec((blk, width), lambda b, h, i: (b * nq + i, h)),
        out_shape=jax.ShapeDtypeStruct((batch * seq, GROUP), BF16),
        compiler_params=_cparams("parallel", "parallel", "arbitrary"),
        name="sb_attention",
    )(proj, proj, proj)


def _diff_kernel(lam_ref, q_ref, k_ref, v_ref, g_ref, o_ref, *, blk, scale, lam_init):
    i = pl.program_id(2)
    half = HEAD_DIM // 2
    lane = lax.broadcasted_iota(jnp.int32, (blk, HEAD_DIM), 1)
    heads = [slice(h * HEAD_DIM, (h + 1) * HEAD_DIM) for h in range(DIFF_HEADS_PER_STEP)]
    qs = []
    for hs in heads:
        q = q_ref[:, hs].astype(F32) * scale
        qs.append(jnp.concatenate([jnp.where(lane < half, q, 0.0), jnp.where(lane >= half, q, 0.0)],
                                  axis=0).astype(BF16))
    lead = jnp.maximum(i - 1, 0)
    row = lax.broadcasted_iota(jnp.int32, (2 * blk, 2 * blk), 0) & (blk - 1)
    col = lax.broadcasted_iota(jnp.int32, (2 * blk, 2 * blk), 1)
    causal = col - row <= (i - lead) * blk

    def softmax(s, state, masked):
        m, l_lanes, acc = state
        if masked:
            s = jnp.where(causal, s, -jnp.inf)
        m_new = jnp.maximum(m, jnp.max(s, axis=1, keepdims=True))
        alpha = jnp.exp(m - m_new)
        p = jnp.exp(s - m_new)
        p_lanes = p[:, 0:HEAD_DIM]
        for c in range(1, s.shape[1] // HEAD_DIM):
            p_lanes = p_lanes + p[:, c * HEAD_DIM:(c + 1) * HEAD_DIM]
        return m_new, alpha * l_lanes + p_lanes, alpha * acc, p.astype(BF16)

    def span(first_block, nblk, states, masked):
        rows = pl.ds(pl.multiple_of(first_block * blk, blk), nblk * blk)
        scores = [_dot_nt(q, k_ref[rows, hs]) for q, hs in zip(qs, heads)]
        out = []
        for s, hs, state in zip(scores, heads, states):
            m, l_lanes, acc, p = softmax(s, state, masked)
            out.append((m, l_lanes, acc + _dot(p, v_ref[rows, hs])))
        return tuple(out)

    init = (jnp.full((2 * blk, 1), -jnp.inf, F32), jnp.zeros((2 * blk, HEAD_DIM), F32),
            jnp.zeros((2 * blk, HEAD_DIM), F32))
    states = span(lead, 2, (init,) * len(heads), True)
    states = lax.fori_loop(0, lead // 2, lambda n, st: span(2 * n, 2, st, False), states)
    states = lax.fori_loop(0, lead % 2, lambda n, st: span(lead - 1, 1, st, False), states)

    lam_v = lam_ref[...]
    d1 = jnp.sum(lam_v[0:1, :] * lam_v[1:2, :], axis=1, keepdims=True)
    d2 = jnp.sum(lam_v[2:3, :] * lam_v[3:4, :], axis=1, keepdims=True)
    lam = jnp.exp(d1) - jnp.exp(d2) + lam_init
    for hs, (_, l_lanes, acc) in zip(heads, states):
        norm = acc / jnp.sum(l_lanes, axis=1, keepdims=True)
        out = norm[:blk] - lam * norm[blk:]
        o_ref[:, hs] = (_rms(out, g_ref[...]) * (1.0 - lam_init)).astype(o_ref.dtype)


def _diff_attention(proj, lam_vecs, out_gain, lam_init, batch, seq):
    blk = ATT_BLOCK
    nq = seq // blk
    width = DIFF_HEADS_PER_STEP * HEAD_DIM
    qb, kb, vb = COL_DF // width, (COL_DF + GROUP) // width, (COL_DF + 2 * GROUP) // width
    kern = functools.partial(_diff_kernel, blk=blk, scale=(HEAD_DIM // 2) ** -0.5, lam_init=lam_init)
    return pl.pallas_call(
        kern,
        grid=(batch, N_HEADS // DIFF_HEADS_PER_STEP, nq),
        in_specs=[
            pl.BlockSpec(lam_vecs.shape, lambda b, h, i: (0, 0)),
            pl.BlockSpec((blk, width), lambda b, h, i: (b * nq + i, qb + h)),
            pl.BlockSpec((seq, width), lambda b, h, i: (b, kb + h)),
            pl.BlockSpec((seq, width), lambda b, h, i: (b, vb + h)),
            pl.BlockSpec((1, HEAD_DIM), lambda b, h, i: (0, 0)),
        ],
        out_specs=pl.BlockSpec((blk, width), lambda b, h, i: (b * nq + i, h)),
        out_shape=jax.ShapeDtypeStruct((batch * seq, GROUP), BF16),
        compiler_params=_cparams("parallel", "parallel", "arbitrary"),
        name="diff_attention",
    )(lam_vecs, proj, proj, proj, out_gain.reshape(1, HEAD_DIM))


def _chunk_iotas(width=CHUNK):
    row = lax.broadcasted_iota(jnp.int32, (CHUNK, width), 0)
    col = lax.broadcasted_iota(jnp.int32, (CHUNK, width), 1)
    return row, col


def _lower_left_blocks(row, col, level):
    same_pair = (row >> (level + 1)) == (col >> (level + 1))
    return same_pair & (((row >> level) & 1) == 1) & (((col >> level) & 1) == 0)


def _unit_lower_inverses(a_list, row, col):
    eye = jnp.where(row == col, 1.0, 0.0).astype(F32)
    invs = [eye - jnp.where(_lower_left_blocks(row, col, 0), a, 0.0) for a in a_list]
    for level in range(1, CHUNK.bit_length() - 1):
        mask = _lower_left_blocks(row, col, level)
        inv16 = [inv.astype(BF16) for inv in invs]
        left = [_dot_unit_left(i16, jnp.where(mask, a, 0.0)) for i16, a in zip(inv16, a_list)]
        invs = [inv - _dot_unit_right(x, i16) for inv, x, i16 in zip(invs, left, inv16)]
    return invs


def _gdn_kernel(x_ref, z_ref, sm_ref, cw_ref, hp_ref, gain_ref, o_ref,
                xs_ref, qn_ref, kn_ref, vn_ref, u_ref, w_ref, qg_ref, kd_ref, qk_ref, el_ref, st_ref, *, ts):
    t = pl.program_id(1)

    @pl.when(t == 0)
    def _():
        xs_ref[0:CONV_HALO, :] = jnp.zeros((CONV_HALO, 3 * GROUP), F32)
        st_ref[...] = jnp.zeros_like(st_ref)

    xs_ref[CONV_HALO:CONV_HALO + ts, :] = x_ref[...].astype(F32)

    for part, dst in enumerate((qn_ref, kn_ref, vn_ref)):
        for h in range(N_HEADS):
            c0 = part * GROUP + h * HEAD_DIM
            acc = jnp.zeros((ts, HEAD_DIM), F32)
            for j in range(CONV_TAPS):
                off = CONV_HALO - (CONV_TAPS - 1) + j
                acc = acc + xs_ref[off:off + ts, c0:c0 + HEAD_DIM] * cw_ref[j:j + 1, c0:c0 + HEAD_DIM]
            y = _silu(acc)
            if part < 2:
                y = y * lax.rsqrt(jnp.sum(y * y, axis=-1, keepdims=True) + NORM_EPS)
            if part == 0:
                y = y * (HEAD_DIM ** -0.5)
            dst[:, h * HEAD_DIM:(h + 1) * HEAD_DIM] = y

    xs_ref[0:CONV_HALO, :] = xs_ref[ts:ts + CONV_HALO, :]

    row, col = _chunk_iotas()
    strict = row > col
    row_w, col_w = _chunk_iotas(HEAD_DIM)
    incl_w = row_w >= col_w
    eye_w = row_w == col_w
    lower_ones = jnp.where(row >= col, 1.0, 0.0).astype(BF16)
    all_ones = jnp.ones((CHUNK, CHUNK), BF16)

    def prepare(pair, _):
        gates = []
        for sub in range(GDN_CHUNKS_PER_STEP):
            c = pair * GDN_CHUNKS_PER_STEP + sub
            r0 = pl.multiple_of(c * CHUNK, CHUNK)
            sm = sm_ref[pl.ds(r0, CHUNK), :]
            beta_all = _sigmoid(sm)
            g_all = -jnp.exp(hp_ref[0:1, :]) * _softplus(sm + hp_ref[1:2, :])
            g_b = jnp.concatenate(
                [jnp.broadcast_to(g_all[:, N_HEADS + h:N_HEADS + h + 1], (CHUNK, HEAD_DIM))
                 for h in range(N_HEADS)], axis=1)
            gc_all = _dot_sel(lower_ones, g_b)
            gates.append((c, r0, beta_all, gc_all))
        gc_rows = []
        for _, _, _, gc_all in gates:
            gc_diag = jnp.concatenate(
                [jnp.where(eye_w, gc_all[:, h * HEAD_DIM:(h + 1) * HEAD_DIM], 0.0) for h in range(N_HEADS)], axis=1)
            gc_rows.append(_dot_sel(all_ones, gc_diag))

        probs = []
        for (c, r0, beta_all, gc_all), gc_row_all in zip(gates, gc_rows):
            for h in range(N_HEADS):
                hs = slice(h * HEAD_DIM, (h + 1) * HEAD_DIM)
                beta = beta_all[:, h:h + 1]
                gc = gc_all[:, hs]
                decay = jnp.exp(jnp.where(incl_w, gc - gc_row_all[:, hs], -jnp.inf))[:, :CHUNK]
                q = qn_ref[pl.ds(r0, CHUNK), hs]
                k = kn_ref[pl.ds(r0, CHUNK), hs]
                v = vn_ref[pl.ds(r0, CHUNK), hs]
                kb = k * beta
                k16 = k.astype(BF16)
                a_mat = jnp.where(strict, _dot_nt(kb.astype(BF16), k16) * decay, 0.0)
                qk = (_dot_nt(q.astype(BF16), k16) * decay).astype(BF16)
                probs.append(dict(c=c, r0=r0, h=h, hs=hs, gc=gc, q=q, k=k, kb=kb, vb=v * beta, a=a_mat, qk=qk))

        invs = _unit_lower_inverses([p["a"] for p in probs], row, col)
        uws = [_dot_unit_left(inv.astype(BF16), jnp.concatenate([p["vb"], p["kb"] * jnp.exp(p["gc"])], axis=1))
               for inv, p in zip(invs, probs)]
        for p, uw in zip(probs, uws):
            r0, hs, gc, h = p["r0"], p["hs"], p["gc"], p["h"]
            g_last = gc[CHUNK - 1:CHUNK, :]
            u_ref[pl.ds(r0, CHUNK), hs] = uw[:, :HEAD_DIM]
            w_ref[pl.ds(r0, CHUNK), hs] = uw[:, HEAD_DIM:].astype(BF16)
            qg_ref[pl.ds(r0, CHUNK), hs] = (p["q"] * jnp.exp(gc)).astype(BF16)
            kd_ref[pl.ds(r0, CHUNK), hs] = (p["k"] * jnp.exp(g_last - gc)).astype(BF16)
            qk_ref[pl.ds(r0, CHUNK), h * HEAD_DIM:h * HEAD_DIM + CHUNK] = p["qk"]
            el_ref[pl.ds(pl.multiple_of(p["c"] * SUBLANES, SUBLANES), SUBLANES), hs] = jnp.broadcast_to(
                jnp.exp(g_last), (SUBLANES, HEAD_DIM))
        return 0

    lax.fori_loop(0, ts // (CHUNK * GDN_CHUNKS_PER_STEP), prepare, 0)

    def scan(c, _):
        r0 = pl.multiple_of(c * CHUNK, CHUNK)
        heads = [slice(h * HEAD_DIM, (h + 1) * HEAD_DIM) for h in range(N_HEADS)]
        states = [st_ref[h] for h in range(N_HEADS)]
        s16 = [s.astype(BF16) for s in states]
        ws = [_dot(w_ref[pl.ds(r0, CHUNK), hs], s) for hs, s in zip(heads, s16)]
        inter = [_dot(qg_ref[pl.ds(r0, CHUNK), hs], s) for hs, s in zip(heads, s16)]
        v_new = [(u_ref[pl.ds(r0, CHUNK), hs] - x).astype(BF16) for hs, x in zip(heads, ws)]
        upd = [_dot_tn(kd_ref[pl.ds(r0, CHUNK), hs], vn) for hs, vn in zip(heads, v_new)]
        for h, hs in enumerate(heads):
            e_last = el_ref[pl.ds(pl.multiple_of(c * SUBLANES, SUBLANES), SUBLANES), hs][0:1, :]
            st_ref[h] = states[h] * e_last + upd[h]
        intra = [_dot(qk_ref[pl.ds(r0, CHUNK), h * HEAD_DIM:h * HEAD_DIM + CHUNK], vn)
                 for h, vn in enumerate(v_new)]
        for h, hs in enumerate(heads):
            zg = z_ref[pl.ds(r0, CHUNK), hs].astype(F32)
            o_ref[pl.ds(r0, CHUNK), hs] = (_rms(inter[h] + intra[h], gain_ref[...]) * _silu(zg)).astype(o_ref.dtype)
        return 0

    lax.fori_loop(0, ts // CHUNK, scan, 0)


def _gdn(proj, small, conv_w, head_params, out_gain, batch, seq, *, ts):
    nt = seq // ts
    kern = functools.partial(_gdn_kernel, ts=ts)
    qkv_blk = COL_GD // (3 * GROUP)
    z_blk = (COL_GD + 3 * GROUP) // GROUP
    return pl.pallas_call(
        kern,
        grid=(batch, nt),
        in_specs=[
            pl.BlockSpec((ts, 3 * GROUP), lambda b, t: (b * nt + t, qkv_blk)),
            pl.BlockSpec((ts, GROUP), lambda b, t: (b * nt + t, z_blk)),
            pl.BlockSpec((ts, SMALL_W), lambda b, t: (b * nt + t, 0)),
            pl.BlockSpec((CONV_TAPS, 3 * GROUP), lambda b, t: (0, 0)),
            pl.BlockSpec(head_params.shape, lambda b, t: (0, 0)),
            pl.BlockSpec((1, HEAD_DIM), lambda b, t: (0, 0)),
        ],
        out_specs=pl.BlockSpec((ts, GROUP), lambda b, t: (b * nt + t, 0)),
        out_shape=jax.ShapeDtypeStruct((batch * seq, GROUP), BF16),
        scratch_shapes=[
            pltpu.VMEM((ts + CONV_HALO, 3 * GROUP), F32),
            pltpu.VMEM((ts, GROUP), F32),
            pltpu.VMEM((ts, GROUP), F32),
            pltpu.VMEM((ts, GROUP), F32),
            pltpu.VMEM((ts, GROUP), F32),
            pltpu.VMEM((ts, GROUP), BF16),
            pltpu.VMEM((ts, GROUP), BF16),
            pltpu.VMEM((ts, GROUP), BF16),
            pltpu.VMEM((ts, GROUP), BF16),
            pltpu.VMEM((ts // CHUNK * SUBLANES, GROUP), F32),
            pltpu.VMEM((N_HEADS, HEAD_DIM, HEAD_DIM), F32),
        ],
        compiler_params=_cparams("parallel", "arbitrary"),
        name="gated_deltanet",
    )(proj, proj, small, conv_w, head_params, out_gain.reshape(1, HEAD_DIM))


def _gla_kernel(qk_ref, v_ref, og_ref, sm_ref, w2_ref, gb_ref, gain_ref, o_ref,
                att_ref, qin_ref, kout_ref, el_ref, st_ref, *, ts):
    t = pl.program_id(1)

    @pl.when(t == 0)
    def _():
        st_ref[...] = jnp.zeros_like(st_ref)

    row, col = _chunk_iotas()
    lower_ones = jnp.where(row >= col, 1.0, 0.0).astype(BF16)
    eye = row == col
    row_w, lane = _chunk_iotas(HEAD_DIM)
    qk_scale = GLA_DK ** -0.5

    levels = []
    for level in range(CHUNK.bit_length() - 1):
        ref_row = ((row >> (level + 1)) << (level + 1)) + (1 << level)
        pick = jnp.where(col == ref_row, 1.0, 0.0).astype(BF16)
        upper_w = ((row_w >> level) & 1) == 1
        levels.append((pick, _lower_left_blocks(row, col, level), upper_w))

    n_pairs = N_HEADS * GLA_DK // HEAD_DIM
    heads_per_pair = HEAD_DIM // GLA_DK

    def prepare(step, _):
        cums = []
        for sub in range(GLA_CHUNKS_PER_STEP):
            c = step * GLA_CHUNKS_PER_STEP + sub
            r0 = pl.multiple_of(c * CHUNK, CHUNK)
            sm = sm_ref[pl.ds(r0, CHUNK), :].astype(BF16)
            gate = _dot(sm, w2_ref[...]) + gb_ref[...]
            log_a = (gate - _softplus(gate)) * (1.0 / GLA_TAU)
            cums.append((c, r0, _dot_sel(lower_ones, log_a)))
        probs = []
        for c, r0, b_all in cums:
            for pair in range(n_pairs):
                ps = slice(pair * HEAD_DIM, (pair + 1) * HEAD_DIM)
                b = b_all[:, ps]
                q2 = qk_ref[pl.ds(r0, CHUNK), ps].astype(F32) * qk_scale
                k2 = qk_ref[pl.ds(r0, CHUNK),
                            GROUP // 2 + pair * HEAD_DIM:GROUP // 2 + (pair + 1) * HEAD_DIM].astype(F32)
                probs.append(dict(c=c, r0=r0, pair=pair, ps=ps, b=b, q2=q2, k2=k2,
                                  b_refs=[_dot_sel(pick, b) for pick, _, _ in levels]))
        for p in probs:
            b, q2, k2 = p["b"], p["q2"], p["k2"]
            p["q_lv"] = [jnp.where(upper_w, q2 * jnp.exp(jnp.where(upper_w, b - b_ref, 0.0)), 0.0)
                         for (_, _, upper_w), b_ref in zip(levels, p["b_refs"])]
            p["k_lv"] = [jnp.where(upper_w, 0.0, k2 * jnp.exp(jnp.where(upper_w, 0.0, b_ref - b))).astype(BF16)
                         for (_, _, upper_w), b_ref in zip(levels, p["b_refs"])]
        for p in probs:
            r0, b, q2, k2 = p["r0"], p["b"], p["q2"], p["k2"]
            b_last = b[CHUNK - 1:CHUNK, :]
            q_in = (q2 * jnp.exp(b)).astype(BF16)
            k_out = (k2 * jnp.exp(b_last - b)).astype(BF16)
            diag_qk = q2 * k2
            for r in range(heads_per_pair):
                h = p["pair"] * heads_per_pair + r
                hs = slice(h * HEAD_DIM, (h + 1) * HEAD_DIM)
                mine = (lane >= r * GLA_DK) & (lane < (r + 1) * GLA_DK)
                att = jnp.where(eye, jnp.sum(jnp.where(mine, diag_qk, 0.0), axis=1, keepdims=True), 0.0)
                for (_, pair_mask, _), ql, kl in zip(levels, p["q_lv"], p["k_lv"]):
                    s = _dot_nt(jnp.where(mine, ql, 0.0).astype(BF16), kl)
                    att = att + jnp.where(pair_mask, s, 0.0)
                att_ref[pl.ds(r0, CHUNK), h * HEAD_DIM:h * HEAD_DIM + CHUNK] = att.astype(BF16)
                qin_ref[pl.ds(r0, CHUNK), hs] = jnp.where(mine, q_in, jnp.zeros_like(q_in))
                kout_ref[pl.ds(r0, CHUNK), hs] = jnp.where(mine, k_out, jnp.zeros_like(k_out))
            el_ref[pl.ds(pl.multiple_of(p["c"] * SUBLANES, SUBLANES), SUBLANES), p["ps"]] = jnp.broadcast_to(
                jnp.exp(b_last), (SUBLANES, HEAD_DIM))
        return 0

    lax.fori_loop(0, ts // (CHUNK * GLA_CHUNKS_PER_STEP), prepare, 0)

    def scan(c, _):
        r0 = pl.multiple_of(c * CHUNK, CHUNK)
        heads = [slice(h * HEAD_DIM, (h + 1) * HEAD_DIM) for h in range(N_HEADS)]
        states = [st_ref[h] for h in range(N_HEADS)]
        values = [v_ref[pl.ds(r0, CHUNK), hs] for hs in heads]
        inter = [_dot_nt(qin_ref[pl.ds(r0, CHUNK), hs], s.astype(BF16)) for hs, s in zip(heads, states)]
        intra = [_dot(att_ref[pl.ds(r0, CHUNK), h * HEAD_DIM:h * HEAD_DIM + CHUNK], v)
                 for h, v in enumerate(values)]
        upd = [_dot_tn(v, kout_ref[pl.ds(r0, CHUNK), hs]) for hs, v in zip(heads, values)]
        for h, hs in enumerate(heads):
            ps = slice((h // heads_per_pair) * HEAD_DIM, (h // heads_per_pair + 1) * HEAD_DIM)
            e_last = el_ref[pl.ds(pl.multiple_of(c * SUBLANES, SUBLANES), SUBLANES), ps][0:1, :]
            st_ref[h] = states[h] * e_last + upd[h]
            og = og_ref[pl.ds(r0, CHUNK), hs].astype(F32)
            o_ref[pl.ds(r0, CHUNK), hs] = (_rms(inter[h] + intra[h], gain_ref[...]) * _silu(og)).astype(o_ref.dtype)
        return 0

    lax.fori_loop(0, ts // CHUNK, scan, 0)


def _gla(proj, small, w2_padded, gate_bias, out_gain, batch, seq, *, ts):
    nt = seq // ts
    kern = functools.partial(_gla_kernel, ts=ts)
    qk_blk = COL_GL // GROUP
    return pl.pallas_call(
        kern,
        grid=(batch, nt),
        in_specs=[
            pl.BlockSpec((ts, GROUP), lambda b, t: (b * nt + t, qk_blk)),
            pl.BlockSpec((ts, GROUP), lambda b, t: (b * nt + t, qk_blk + 1)),
            pl.BlockSpec((ts, GROUP), lambda b, t: (b * nt + t, qk_blk + 2)),
            pl.BlockSpec((ts, SMALL_W), lambda b, t: (b * nt + t, 0)),
            pl.BlockSpec(w2_padded.shape, lambda b, t: (0, 0)),
            pl.BlockSpec((1, N_HEADS * GLA_DK), lambda b, t: (0, 0)),
            pl.BlockSpec((1, HEAD_DIM), lambda b, t: (0, 0)),
        ],
        out_specs=pl.BlockSpec((ts, GROUP), lambda b, t: (b * nt + t, 0)),
        out_shape=jax.ShapeDtypeStruct((batch * seq, GROUP), BF16),
        scratch_shapes=[
            pltpu.VMEM((ts, GROUP), BF16),
            pltpu.VMEM((ts, GROUP), BF16),
            pltpu.VMEM((ts, GROUP), BF16),
            pltpu.VMEM((ts // CHUNK * SUBLANES, N_HEADS * GLA_DK), F32),
            pltpu.VMEM((N_HEADS, HEAD_DIM, HEAD_DIM), F32),
        ],
        compiler_params=_cparams("parallel", "arbitrary"),
        name="gla",
    )(proj, proj, proj, small, w2_padded, gate_bias.reshape(1, -1), out_gain.reshape(1, HEAD_DIM))


def _out_proj_kernel(x_ref, m0_ref, m1_ref, m2_ref, m3_ref, w_ref, o_ref):
    acc = x_ref[...]
    for g, m_ref in enumerate((m0_ref, m1_ref, m2_ref, m3_ref)):
        acc = acc + _dot(m_ref[...], w_ref[g * GROUP:(g + 1) * GROUP, :])
    o_ref[...] = acc


def _out_proj(x, mixes, w_out, layer, *, tm):
    t, d = x.shape
    mix_spec = pl.BlockSpec((tm, GROUP), lambda i: (i, 0))
    return pl.pallas_call(
        _out_proj_kernel,
        grid=(t // tm,),
        in_specs=[pl.BlockSpec((tm, d), lambda i: (i, 0)), mix_spec, mix_spec, mix_spec, mix_spec,
                  pl.BlockSpec((None,) + w_out.shape[1:], lambda i: (layer, 0, 0))],
        out_specs=pl.BlockSpec((tm, d), lambda i: (i, 0)),
        out_shape=jax.ShapeDtypeStruct((t, d), F32),
        compiler_params=_cparams("parallel"),
        name="out_proj_residual",
    )(x, *mixes, w_out)


def _ffn_kernel(x_ref, g_ref, wg_ref, wu_ref, wd_ref, og_ref, o_ref, h_ref, acc_ref, *, norm_output):
    f = pl.program_id(1)

    @pl.when(f == 0)
    def _():
        h_ref[...] = _rms(x_ref[...], g_ref[...]).astype(BF16)
        acc_ref[...] = jnp.zeros_like(acc_ref)

    h = h_ref[...]
    hid = _silu(_dot(h, wg_ref[...])) * _dot(h, wu_ref[...])
    acc_ref[...] += _dot(hid.astype(BF16), wd_ref[...])

    @pl.when(f == pl.num_programs(1) - 1)
    def _():
        y = x_ref[...] + acc_ref[...]
        o_ref[...] = _rms(y, og_ref[...]) if norm_output else y


def _ffn(x, gain, w_gate, w_up, w_down, layer, out_gain, *, norm_output, tm, tf):
    t, d = x.shape
    hidden = w_gate.shape[2]
    return pl.pallas_call(
        functools.partial(_ffn_kernel, norm_output=norm_output),
        grid=(t // tm, hidden // tf),
        in_specs=[
            pl.BlockSpec((tm, d), lambda i, f: (i, 0)),
            pl.BlockSpec((1, d), lambda i, f: (0, 0)),
            pl.BlockSpec((None, d, tf), lambda i, f: (layer, 0, f)),
            pl.BlockSpec((None, d, tf), lambda i, f: (layer, 0, f)),
            pl.BlockSpec((None, tf, d), lambda i, f: (layer, f, 0)),
            pl.BlockSpec((1, d), lambda i, f: (0, 0)),
        ],
        out_specs=pl.BlockSpec((tm, d), lambda i, f: (i, 0)),
        out_shape=jax.ShapeDtypeStruct((t, d), F32),
        scratch_shapes=[pltpu.VMEM((tm, d), BF16), pltpu.VMEM((tm, d), F32)],
        compiler_params=_cparams("parallel", "arbitrary"),
        name="ffn_residual",
    )(x, gain.reshape(1, d), w_gate, w_up, w_down, out_gain.reshape(1, d))


def _pack_w_in(w_in):
    gd_small = COL_GD + 4 * GROUP
    df0 = gd_small + 2 * N_HEADS
    gl0 = df0 + 3 * GROUP
    rank0 = gl0 + 3 * GROUP
    pad = jnp.zeros(w_in.shape[:-1] + (IN_COLS_PACKED - COL_SMALL - 2 * N_HEADS - GLA_RANK,), BF16)
    parts = [w_in[..., :gd_small], w_in[..., df0:rank0], w_in[..., gd_small:df0], w_in[..., rank0:rank0 + GLA_RANK]]
    return jnp.concatenate([part.astype(BF16) for part in parts] + [pad], axis=-1)


def _mixer_layer(x, l, p, big, batch, seq):
    lam_init = 0.8 - 0.6 * math.exp(-0.3 * l)
    proj, small = _norm_in_proj(x, p["attn_norm"], big["w_in"], l, tm=1024, tn=1024)
    o_sb = _sb_attention(proj, batch, seq)
    head_params = jnp.zeros((SUBLANES, SMALL_W), F32)
    head_params = head_params.at[0, N_HEADS:2 * N_HEADS].set(p["gdn_a_log"])
    head_params = head_params.at[1, N_HEADS:2 * N_HEADS].set(p["gdn_dt_bias"])
    o_gd = _gdn(proj, small, p["gdn_conv_w"], head_params, p["gdn_out_norm"], batch, seq, ts=512)
    lam_vecs = jnp.stack([p["diff_lam_q1"], p["diff_lam_k1"], p["diff_lam_q2"], p["diff_lam_k2"]])
    o_df = _diff_attention(proj, lam_vecs, p["diff_out_norm"], lam_init, batch, seq)
    w2 = jnp.zeros((SMALL_W, N_HEADS * GLA_DK), F32).at[2 * N_HEADS:2 * N_HEADS + GLA_RANK].set(p["gla_gate_w2"])
    o_gl = _gla(proj, small, w2.astype(BF16), p["gla_gate_b"], p["gla_out_norm"], batch, seq, ts=512)
    return _out_proj(x, (o_sb, o_gd, o_df, o_gl), big["w_out"], l, tm=512)


def kernel(x, attn_norm, w_in, gdn_conv_w, gdn_a_log, gdn_dt_bias, gdn_out_norm, diff_lam_q1, diff_lam_k1,
           diff_lam_q2, diff_lam_k2, diff_out_norm, gla_gate_w2, gla_gate_b, gla_out_norm, w_out, ffn_norm,
           w_gate, w_up, w_down, final_norm):
    batch, seq, d = x.shape
    depth = w_in.shape[0]
    stacked = dict(attn_norm=attn_norm, gdn_conv_w=gdn_conv_w, gdn_a_log=gdn_a_log,
                   gdn_dt_bias=gdn_dt_bias, gdn_out_norm=gdn_out_norm, diff_lam_q1=diff_lam_q1,
                   diff_lam_k1=diff_lam_k1, diff_lam_q2=diff_lam_q2, diff_lam_k2=diff_lam_k2,
                   diff_out_norm=diff_out_norm, gla_gate_w2=gla_gate_w2, gla_gate_b=gla_gate_b,
                   gla_out_norm=gla_out_norm)
    big = dict(w_in=_pack_w_in(w_in), w_out=w_out.astype(BF16), w_gate=w_gate.astype(BF16),
               w_up=w_up.astype(BF16), w_down=w_down.astype(BF16))
    xt = x.reshape(batch * seq, d)
    for l in range(depth):
        p = {name: arr[l] for name, arr in stacked.items()}
        xt = _mixer_layer(xt, l, p, big, batch, seq)
        xt = _ffn(xt, ffn_norm[l], big["w_gate"], big["w_up"], big["w_down"], l, final_norm,
                  norm_output=(l == depth - 1), tm=512, tf=512)
    return xt.reshape(batch, seq, d)
```

```python
import functools
import math

import jax
import jax.numpy as jnp
from jax import lax
from jax.experimental import pallas as pl
from jax.experimental.pallas import tpu as pltpu

F32 = jnp.float32
BF16 = jnp.bfloat16

NORM_EPS = 1e-6
LOG2_E = math.log2(math.e)
SB_DEAD_LOG2 = 110.0 * LOG2_E
N_HEADS = 4
HEAD_DIM = 128
GROUP = N_HEADS * HEAD_DIM
CHUNK = 64
CONV_TAPS = 4
GLA_DK = 64
GLA_RANK = 16
GLA_TAU = 16.0
ATT_BLOCK = 256
CONV_HALO = 8
SUBLANES = 8
GDN_CHUNKS_PER_STEP = 4
GLA_CHUNKS_PER_STEP = 4
DIFF_HEADS_PER_STEP = 4
SB_HEADS_PER_STEP = 4
COL_SB = 0
COL_GD = 3 * GROUP
COL_DF = 7 * GROUP
COL_GL = 10 * GROUP
COL_SMALL = 13 * GROUP
SMALL_W = 128
IN_COLS_PACKED = 14 * GROUP
VMEM_LIMIT = 56 * 1024 * 1024


def _cparams(*sem):
    return pltpu.CompilerParams(dimension_semantics=sem, vmem_limit_bytes=VMEM_LIMIT)


def _dot(a, b):
    return jnp.dot(a, b, preferred_element_type=F32)


def _dot_nt(a, b):
    return lax.dot_general(a, b, (((1,), (1,)), ((), ())), preferred_element_type=F32)


def _dot_tn(a, b):
    return lax.dot_general(a, b, (((0,), (0,)), ((), ())), preferred_element_type=F32)


def _split(x):
    hi = x.astype(BF16)
    return hi, (x - hi.astype(F32)).astype(BF16)


def _dot_sel(sel16, x):
    hi, lo = _split(x)
    return _dot(sel16, hi) + _dot(sel16, lo)


def _dot_unit_left(unit16, x):
    hi, lo = _split(x)
    return _dot(unit16, hi) + _dot(unit16, lo)


def _dot_unit_right(x, unit16):
    hi, lo = _split(x)
    return _dot(hi, unit16) + _dot(lo, unit16)


def _rms(x, gain):
    return x * lax.rsqrt(jnp.mean(x * x, axis=-1, keepdims=True) + NORM_EPS) * gain


def _silu(x):
    return x / (1.0 + jnp.exp(-x))


def _sigmoid(x):
    return 1.0 / (1.0 + jnp.exp(-x))


def _softplus(x):
    return jnp.maximum(x, 0.0) + jnp.log(1.0 + jnp.exp(-jnp.abs(x)))


def _norm_matmul_kernel(x_ref, g_ref, w_ref, o_ref, small_ref, h_ref):
    j = pl.program_id(1)
    last = pl.num_programs(1) - 1

    @pl.when(j == 0)
    def _():
        h_ref[...] = _rms(x_ref[...], g_ref[...]).astype(BF16)

    y = _dot(h_ref[...], w_ref[...])
    o_ref[...] = y.astype(o_ref.dtype)

    @pl.when(j == last)
    def _():
        small_ref[...] = y[:, y.shape[1] - small_ref.shape[1]:]


def _norm_in_proj(x, gain, w, layer, *, tm, tn):
    t, d = x.shape
    return pl.pallas_call(
        _norm_matmul_kernel,
        grid=(t // tm, IN_COLS_PACKED // tn),
        in_specs=[
            pl.BlockSpec((tm, d), lambda i, j: (i, 0)),
            pl.BlockSpec((1, d), lambda i, j: (0, 0)),
            pl.BlockSpec((None, d, tn), lambda i, j: (layer, 0, j)),
        ],
        out_specs=[
            pl.BlockSpec((tm, tn), lambda i, j: (i, j)),
            pl.BlockSpec((tm, IN_COLS_PACKED - COL_SMALL), lambda i, j: (i, 0)),
        ],
        out_shape=[jax.ShapeDtypeStruct((t, IN_COLS_PACKED), BF16),
                   jax.ShapeDtypeStruct((t, IN_COLS_PACKED - COL_SMALL), F32)],
        scratch_shapes=[pltpu.VMEM((tm, d), BF16)],
        compiler_params=_cparams("parallel", "arbitrary"),
        name="norm_in_proj",
    )(x, gain.reshape(1, d), w)


def _sb_kernel(q_ref, k_ref, v_ref, o_ref, *, blk, scale):
    i = pl.program_id(2)
    heads = [slice(h * HEAD_DIM, (h + 1) * HEAD_DIM) for h in range(SB_HEADS_PER_STEP)]
    row = lax.broadcasted_iota(jnp.int32, (blk, blk), 0)
    col = lax.broadcasted_iota(jnp.int32, (blk, blk), 1)
    suffix = jnp.where(row > col, 1.0, 0.0).astype(BF16)
    lead = jnp.maximum(i - 1, 0)
    col_minus_row = (lax.broadcasted_iota(jnp.int32, (blk, 2 * blk), 1)
                     - lax.broadcasted_iota(jnp.int32, (blk, 2 * blk), 0))
    causal = col_minus_row < (i - lead) * blk

    def span(first_block, nblk, state, masked):
        rows = pl.ds(pl.multiple_of(first_block * blk, blk), nblk * blk)
        parts = [slice(b * blk, (b + 1) * blk) for b in range(nblk)]
        zs = [_dot_nt(q_ref[:, hs], k_ref[rows, hs]) * (scale * LOG2_E) for hs in heads]
        sps = []
        for z in zs:
            sp = jnp.maximum(z, 0.0) + jnp.log2(1.0 + jnp.exp2(-jnp.abs(z)))
            sps.append(jnp.where(causal, sp, 0.0) if masked else sp)
        tails = [[_dot_sel_rhs(sp[:, s], suffix) for s in parts] for sp in sps]
        out = []
        for hs, z, sp, tail, (carry, acc) in zip(heads, zs, sps, tails, state):
            sums = [jnp.sum(sp[:, s], axis=1, keepdims=True) for s in parts]
            offsets = [None] * nblk
            for b in reversed(range(nblk)):
                offsets[b] = carry
                carry = carry + sums[b]
            att = jnp.concatenate(
                [jnp.exp2(z[:, s] - sp[:, s] - tail[b] - offsets[b])
                 for b, s in enumerate(parts)], axis=1)
            if masked:
                att = jnp.where(causal, att, 0.0)
            out.append((carry, acc + _dot(att.astype(BF16), v_ref[rows, hs])))
        return tuple(out)

    init = (jnp.zeros((blk, 1), F32), jnp.zeros((blk, HEAD_DIM), F32))
    state = span(lead, 2, (init,) * len(heads), True)

    def alive(state):
        low = state[0][0]
        for carry, _ in state[1:]:
            low = jnp.minimum(low, carry)
        return jnp.min(low) < SB_DEAD_LOG2

    _, state = lax.while_loop(
        lambda carried: jnp.logical_and(carried[0] < lead // 2, alive(carried[1])),
        lambda carried: (carried[0] + 1, span(lead - 2 - 2 * carried[0], 2, carried[1], False)),
        (jnp.int32(0), state))
    odd_left = jnp.where(alive(state), lead % 2, 0)
    state = lax.fori_loop(0, odd_left, lambda n, st: span(0, 1, st, False), state)
    for hs, (_, acc) in zip(heads, state):
        o_ref[:, hs] = acc.astype(o_ref.dtype)


def _dot_sel_rhs(x, sel16):
    hi, lo = _split(x)
    return _dot(hi, sel16) + _dot(lo, sel16)


def _sb_attention(proj, batch, seq):
    blk = ATT_BLOCK
    nq = seq // blk
    width = SB_HEADS_PER_STEP * HEAD_DIM
    qb, kb, vb = COL_SB // width, (COL_SB + GROUP) // width, (COL_SB + 2 * GROUP) // width
    kern = functools.partial(_sb_kernel, blk=blk, scale=HEAD_DIM ** -0.5)
    return pl.pallas_call(
        kern,
        grid=(batch, N_HEADS // SB_HEADS_PER_STEP, nq),
        in_specs=[
            pl.BlockSpec((blk, width), lambda b, h, i: (b * nq + i, qb + h)),
            pl.BlockSpec((seq, width), lambda b, h, i: (b, kb + h)),
            pl.BlockSpec((seq, width), lambda b, h, i: (b, vb + h)),
        ],
        out_specs=pl.BlockSpec((blk, width), lambda b, h, i: (b * nq + i, h)),
        out_shape=jax.ShapeDtypeStruct((batch * seq, GROUP), BF16),
        compiler_params=_cparams("parallel", "parallel", "arbitrary"),
        name="sb_attention",
    )(proj, proj, proj)


def _diff_kernel(lam_ref, q_ref, k_ref, v_ref, g_ref, o_ref, *, blk, scale, lam_init):
    i = pl.program_id(2)
    half = HEAD_DIM // 2
    lane = lax.broadcasted_iota(jnp.int32, (blk, HEAD_DIM), 1)
    heads = [slice(h * HEAD_DIM, (h + 1) * HEAD_DIM) for h in range(DIFF_HEADS_PER_STEP)]
    qs = []
    for hs in heads:
        q = q_ref[:, hs].astype(F32) * scale
        qs.append(jnp.concatenate([jnp.where(lane < half, q, 0.0), jnp.where(lane >= half, q, 0.0)],
                                  axis=0).astype(BF16))
    lead = jnp.maximum(i - 1, 0)
    row = lax.broadcasted_iota(jnp.int32, (2 * blk, 2 * blk), 0) & (blk - 1)
    col = lax.broadcasted_iota(jnp.int32, (2 * blk, 2 * blk), 1)
    causal = col - row <= (i - lead) * blk

    def softmax(s, state, masked):
        m, l_lanes, acc = state
        if masked:
            s = jnp.where(causal, s, -jnp.inf)
        m_new = jnp.maximum(m, jnp.max(s, axis=1, keepdims=True))
        alpha = jnp.exp(m - m_new)
        p = jnp.exp(s - m_new)
        p_lanes = p[:, 0:HEAD_DIM]
        for c in range(1, s.shape[1] // HEAD_DIM):
            p_lanes = p_lanes + p[:, c * HEAD_DIM:(c + 1) * HEAD_DIM]
        return m_new, alpha * l_lanes + p_lanes, alpha * acc, p.astype(BF16)

    def span(first_block, nblk, states, masked):
        rows = pl.ds(pl.multiple_of(first_block * blk, blk), nblk * blk)
        out = []
        s_next = _dot_nt(qs[0], k_ref[rows, heads[0]])
        for h, (hs, state) in enumerate(zip(heads, states)):
            s = s_next
            if h + 1 < len(heads):
                s_next = _dot_nt(qs[h + 1], k_ref[rows, heads[h + 1]])
            m, l_lanes, acc, p = softmax(s, state, masked)
            out.append((m, l_lanes, acc + _dot(p, v_ref[rows, hs])))
        return tuple(out)

    init = (jnp.full((2 * blk, 1), -jnp.inf, F32), jnp.zeros((2 * blk, HEAD_DIM), F32),
            jnp.zeros((2 * blk, HEAD_DIM), F32))
    states = span(lead, 2, (init,) * len(heads), True)
    states = lax.fori_loop(0, lead // 2, lambda n, st: span(2 * n, 2, st, False), states)
    states = lax.fori_loop(0, lead % 2, lambda n, st: span(lead - 1, 1, st, False), states)

    lam_v = lam_ref[...]
    d1 = jnp.sum(lam_v[0:1, :] * lam_v[1:2, :], axis=1, keepdims=True)
    d2 = jnp.sum(lam_v[2:3, :] * lam_v[3:4, :], axis=1, keepdims=True)
    lam = jnp.exp(d1) - jnp.exp(d2) + lam_init
    for hs, (_, l_lanes, acc) in zip(heads, states):
        norm = acc / jnp.sum(l_lanes, axis=1, keepdims=True)
        out = norm[:blk] - lam * norm[blk:]
        o_ref[:, hs] = (_rms(out, g_ref[...]) * (1.0 - lam_init)).astype(o_ref.dtype)


def _diff_attention(proj, lam_vecs, out_gain, lam_init, batch, seq):
    blk = ATT_BLOCK
    nq = seq // blk
    width = DIFF_HEADS_PER_STEP * HEAD_DIM
    qb, kb, vb = COL_DF // width, (COL_DF + GROUP) // width, (COL_DF + 2 * GROUP) // width
    kern = functools.partial(_diff_kernel, blk=blk, scale=(HEAD_DIM // 2) ** -0.5, lam_init=lam_init)
    return pl.pallas_call(
        kern,
        grid=(batch, N_HEADS // DIFF_HEADS_PER_STEP, nq),
        in_specs=[
            pl.BlockSpec(lam_vecs.shape, lambda b, h, i: (0, 0)),
            pl.BlockSpec((blk, width), lambda b, h, i: (b * nq + i, qb + h)),
            pl.BlockSpec((seq, width), lambda b, h, i: (b, kb + h)),
            pl.BlockSpec((seq, width), lambda b, h, i: (b, vb + h)),
            pl.BlockSpec((1, HEAD_DIM), lambda b, h, i: (0, 0)),
        ],
        out_specs=pl.BlockSpec((blk, width), lambda b, h, i: (b * nq + i, h)),
        out_shape=jax.ShapeDtypeStruct((batch * seq, GROUP), BF16),
        compiler_params=_cparams("parallel", "parallel", "arbitrary"),
        name="diff_attention",
    )(lam_vecs, proj, proj, proj, out_gain.reshape(1, HEAD_DIM))


def _chunk_iotas(width=CHUNK):
    row = lax.broadcasted_iota(jnp.int32, (CHUNK, width), 0)
    col = lax.broadcasted_iota(jnp.int32, (CHUNK, width), 1)
    return row, col


def _lower_left_blocks(row, col, level):
    same_pair = (row >> (level + 1)) == (col >> (level + 1))
    return same_pair & (((row >> level) & 1) == 1) & (((col >> level) & 1) == 0)


def _unit_lower_inverses(a_list, row, col):
    eye = jnp.where(row == col, 1.0, 0.0).astype(F32)
    invs = [eye - jnp.where(_lower_left_blocks(row, col, 0), a, 0.0) for a in a_list]
    for level in range(1, CHUNK.bit_length() - 1):
        mask = _lower_left_blocks(row, col, level)
        inv16 = [inv.astype(BF16) for inv in invs]
        left = [_dot_unit_left(i16, jnp.where(mask, a, 0.0)) for i16, a in zip(inv16, a_list)]
        invs = [inv - _dot_unit_right(x, i16) for inv, x, i16 in zip(invs, left, inv16)]
    return invs


def _gdn_kernel(x_ref, z_ref, sm_ref, cw_ref, hp_ref, gain_ref, o_ref,
                xs_ref, qn_ref, kn_ref, vn_ref, u_ref, w_ref, qg_ref, kd_ref, qk_ref, el_ref, st_ref, *, ts):
    t = pl.program_id(1)

    @pl.when(t == 0)
    def _():
        xs_ref[0:CONV_HALO, :] = jnp.zeros((CONV_HALO, 3 * GROUP), F32)
        st_ref[...] = jnp.zeros_like(st_ref)

    xs_ref[CONV_HALO:CONV_HALO + ts, :] = x_ref[...].astype(F32)

    for part, dst in enumerate((qn_ref, kn_ref, vn_ref)):
        for h in range(N_HEADS):
            c0 = part * GROUP + h * HEAD_DIM
            acc = jnp.zeros((ts, HEAD_DIM), F32)
            for j in range(CONV_TAPS):
                off = CONV_HALO - (CONV_TAPS - 1) + j
                acc = acc + xs_ref[off:off + ts, c0:c0 + HEAD_DIM] * cw_ref[j:j + 1, c0:c0 + HEAD_DIM]
            y = _silu(acc)
            if part < 2:
                y = y * lax.rsqrt(jnp.sum(y * y, axis=-1, keepdims=True) + NORM_EPS)
            if part == 0:
                y = y * (HEAD_DIM ** -0.5)
            dst[:, h * HEAD_DIM:(h + 1) * HEAD_DIM] = y

    xs_ref[0:CONV_HALO, :] = xs_ref[ts:ts + CONV_HALO, :]

    row, col = _chunk_iotas()
    strict = row > col
    row_w, col_w = _chunk_iotas(HEAD_DIM)
    incl_w = row_w >= col_w
    eye_w = row_w == col_w
    lower_ones = jnp.where(row >= col, 1.0, 0.0).astype(BF16)
    all_ones = jnp.ones((CHUNK, CHUNK), BF16)

    def prepare(pair, _):
        gates = []
        for sub in range(GDN_CHUNKS_PER_STEP):
            c = pair * GDN_CHUNKS_PER_STEP + sub
            r0 = pl.multiple_of(c * CHUNK, CHUNK)
            sm = sm_ref[pl.ds(r0, CHUNK), :]
            beta_all = _sigmoid(sm)
            g_all = -jnp.exp(hp_ref[0:1, :]) * _softplus(sm + hp_ref[1:2, :])
            g_b = jnp.concatenate(
                [jnp.broadcast_to(g_all[:, N_HEADS + h:N_HEADS + h + 1], (CHUNK, HEAD_DIM))
                 for h in range(N_HEADS)], axis=1)
            gc_all = _dot_sel(lower_ones, g_b)
            gates.append((c, r0, beta_all, gc_all))
        gc_rows = []
        for _, _, _, gc_all in gates:
            gc_diag = jnp.concatenate(
                [jnp.where(eye_w, gc_all[:, h * HEAD_DIM:(h + 1) * HEAD_DIM], 0.0) for h in range(N_HEADS)], axis=1)
            gc_rows.append(_dot_sel(all_ones, gc_diag))

        probs = []
        for (c, r0, beta_all, gc_all), gc_row_all in zip(gates, gc_rows):
            for h in range(N_HEADS):
                hs = slice(h * HEAD_DIM, (h + 1) * HEAD_DIM)
                beta = beta_all[:, h:h + 1]
                gc = gc_all[:, hs]
                decay = jnp.exp(jnp.where(incl_w, gc - gc_row_all[:, hs], -jnp.inf))[:, :CHUNK]
                q = qn_ref[pl.ds(r0, CHUNK), hs]
                k = kn_ref[pl.ds(r0, CHUNK), hs]
                v = vn_ref[pl.ds(r0, CHUNK), hs]
                kb = k * beta
                k16 = k.astype(BF16)
                a_mat = jnp.where(strict, _dot_nt(kb.astype(BF16), k16) * decay, 0.0)
                qk = (_dot_nt(q.astype(BF16), k16) * decay).astype(BF16)
                probs.append(dict(c=c, r0=r0, h=h, hs=hs, gc=gc, q=q, k=k, kb=kb, vb=v * beta, a=a_mat, qk=qk))

        invs = _unit_lower_inverses([p["a"] for p in probs], row, col)
        uws = [_dot_unit_left(inv.astype(BF16), jnp.concatenate([p["vb"], p["kb"] * jnp.exp(p["gc"])], axis=1))
               for inv, p in zip(invs, probs)]
        for p, uw in zip(probs, uws):
            r0, hs, gc, h = p["r0"], p["hs"], p["gc"], p["h"]
            g_last = gc[CHUNK - 1:CHUNK, :]
            u_ref[pl.ds(r0, CHUNK), hs] = uw[:, :HEAD_DIM]
            w_ref[pl.ds(r0, CHUNK), hs] = uw[:, HEAD_DIM:].astype(BF16)
            qg_ref[pl.ds(r0, CHUNK), hs] = (p["q"] * jnp.exp(gc)).astype(BF16)
            kd_ref[pl.ds(r0, CHUNK), hs] = (p["k"] * jnp.exp(g_last - gc)).astype(BF16)
            qk_ref[pl.ds(r0, CHUNK), h * HEAD_DIM:h * HEAD_DIM + CHUNK] = p["qk"]
            el_ref[pl.ds(pl.multiple_of(p["c"] * SUBLANES, SUBLANES), SUBLANES), hs] = jnp.broadcast_to(
                jnp.exp(g_last), (SUBLANES, HEAD_DIM))
        return 0

    lax.fori_loop(0, ts // (CHUNK * GDN_CHUNKS_PER_STEP), prepare, 0)

    def scan(c, _):
        r0 = pl.multiple_of(c * CHUNK, CHUNK)
        heads = [slice(h * HEAD_DIM, (h + 1) * HEAD_DIM) for h in range(N_HEADS)]
        states = [st_ref[h] for h in range(N_HEADS)]
        s16 = [s.astype(BF16) for s in states]
        ws = [_dot(w_ref[pl.ds(r0, CHUNK), hs], s) for hs, s in zip(heads, s16)]
        inter = [_dot(qg_ref[pl.ds(r0, CHUNK), hs], s) for hs, s in zip(heads, s16)]
        v_new = [(u_ref[pl.ds(r0, CHUNK), hs] - x).astype(BF16) for hs, x in zip(heads, ws)]
        upd = [_dot_tn(kd_ref[pl.ds(r0, CHUNK), hs], vn) for hs, vn in zip(heads, v_new)]
        for h, hs in enumerate(heads):
            e_last = el_ref[pl.ds(pl.multiple_of(c * SUBLANES, SUBLANES), SUBLANES), hs][0:1, :]
            st_ref[h] = states[h] * e_last + upd[h]
        intra = [_dot(qk_ref[pl.ds(r0, CHUNK), h * HEAD_DIM:h * HEAD_DIM + CHUNK], vn)
                 for h, vn in enumerate(v_new)]
        for h, hs in enumerate(heads):
            zg = z_ref[pl.ds(r0, CHUNK), hs].astype(F32)
            o_ref[pl.ds(r0, CHUNK), hs] = (_rms(inter[h] + intra[h], gain_ref[...]) * _silu(zg)).astype(o_ref.dtype)
        return 0

    lax.fori_loop(0, ts // CHUNK, scan, 0)


def _gdn(proj, small, conv_w, head_params, out_gain, batch, seq, *, ts):
    nt = seq // ts
    kern = functools.partial(_gdn_kernel, ts=ts)
    qkv_blk = COL_GD // (3 * GROUP)
    z_blk = (COL_GD + 3 * GROUP) // GROUP
    return pl.pallas_call(
        kern,
        grid=(batch, nt),
        in_specs=[
            pl.BlockSpec((ts, 3 * GROUP), lambda b, t: (b * nt + t, qkv_blk)),
            pl.BlockSpec((ts, GROUP), lambda b, t: (b * nt + t, z_blk)),
            pl.BlockSpec((ts, SMALL_W), lambda b, t: (b * nt + t, 0)),
            pl.BlockSpec((CONV_TAPS, 3 * GROUP), lambda b, t: (0, 0)),
            pl.BlockSpec(head_params.shape, lambda b, t: (0, 0)),
            pl.BlockSpec((1, HEAD_DIM), lambda b, t: (0, 0)),
        ],
        out_specs=pl.BlockSpec((ts, GROUP), lambda b, t: (b * nt + t, 0)),
        out_shape=jax.ShapeDtypeStruct((batch * seq, GROUP), BF16),
        scratch_shapes=[
            pltpu.VMEM((ts + CONV_HALO, 3 * GROUP), F32),
            pltpu.VMEM((ts, GROUP), F32),
            pltpu.VMEM((ts, GROUP), F32),
            pltpu.VMEM((ts, GROUP), F32),
            pltpu.VMEM((ts, GROUP), F32),
            pltpu.VMEM((ts, GROUP), BF16),
            pltpu.VMEM((ts, GROUP), BF16),
            pltpu.VMEM((ts, GROUP), BF16),
            pltpu.VMEM((ts, GROUP), BF16),
            pltpu.VMEM((ts // CHUNK * SUBLANES, GROUP), F32),
            pltpu.VMEM((N_HEADS, HEAD_DIM, HEAD_DIM), F32),
        ],
        compiler_params=_cparams("parallel", "arbitrary"),
        name="gated_deltanet",
    )(proj, proj, small, conv_w, head_params, out_gain.reshape(1, HEAD_DIM))


def _gla_kernel(qk_ref, v_ref, og_ref, sm_ref, w2_ref, gb_ref, gain_ref, o_ref,
                att_ref, qin_ref, kout_ref, el_ref, st_ref, *, ts):
    t = pl.program_id(1)

    @pl.when(t == 0)
    def _():
        st_ref[...] = jnp.zeros_like(st_ref)

    row, col = _chunk_iotas()
    lower_ones = jnp.where(row >= col, 1.0, 0.0).astype(BF16)
    eye = row == col
    row_w, lane = _chunk_iotas(HEAD_DIM)
    qk_scale = GLA_DK ** -0.5

    levels = []
    for level in range(CHUNK.bit_length() - 1):
        ref_row = ((row >> (level + 1)) << (level + 1)) + (1 << level)
        pick = jnp.where(col == ref_row, 1.0, 0.0).astype(BF16)
        upper_w = ((row_w >> level) & 1) == 1
        levels.append((pick, _lower_left_blocks(row, col, level), upper_w))

    n_pairs = N_HEADS * GLA_DK // HEAD_DIM
    heads_per_pair = HEAD_DIM // GLA_DK

    def prepare(step, _):
        cums = []
        for sub in range(GLA_CHUNKS_PER_STEP):
            c = step * GLA_CHUNKS_PER_STEP + sub
            r0 = pl.multiple_of(c * CHUNK, CHUNK)
            sm = sm_ref[pl.ds(r0, CHUNK), :].astype(BF16)
            gate = _dot(sm, w2_ref[...]) + gb_ref[...]
            log_a = (gate - _softplus(gate)) * (1.0 / GLA_TAU)
            cums.append((c, r0, _dot_sel(lower_ones, log_a)))
        probs = []
        for c, r0, b_all in cums:
            for pair in range(n_pairs):
                ps = slice(pair * HEAD_DIM, (pair + 1) * HEAD_DIM)
                b = b_all[:, ps]
                q2 = qk_ref[pl.ds(r0, CHUNK), ps].astype(F32) * qk_scale
                k2 = qk_ref[pl.ds(r0, CHUNK),
                            GROUP // 2 + pair * HEAD_DIM:GROUP // 2 + (pair + 1) * HEAD_DIM].astype(F32)
                probs.append(dict(c=c, r0=r0, pair=pair, ps=ps, b=b, q2=q2, k2=k2,
                                  b_refs=[_dot_sel(pick, b) for pick, _, _ in levels]))
        for p in probs:
            b, q2, k2 = p["b"], p["q2"], p["k2"]
            p["q_lv"] = [jnp.where(upper_w, q2 * jnp.exp(jnp.where(upper_w, b - b_ref, 0.0)), 0.0)
                         for (_, _, upper_w), b_ref in zip(levels, p["b_refs"])]
            p["k_lv"] = [jnp.where(upper_w, 0.0, k2 * jnp.exp(jnp.where(upper_w, 0.0, b_ref - b))).astype(BF16)
                         for (_, _, upper_w), b_ref in zip(levels, p["b_refs"])]
        for p in probs:
            r0, b, q2, k2 = p["r0"], p["b"], p["q2"], p["k2"]
            b_last = b[CHUNK - 1:CHUNK, :]
            q_in = (q2 * jnp.exp(b)).astype(BF16)
            k_out = (k2 * jnp.exp(b_last - b)).astype(BF16)
            diag_qk = q2 * k2
            for r in range(heads_per_pair):
                h = p["pair"] * heads_per_pair + r
                hs = slice(h * HEAD_DIM, (h + 1) * HEAD_DIM)
                mine = (lane >= r * GLA_DK) & (lane < (r + 1) * GLA_DK)
                att = jnp.where(eye, jnp.sum(jnp.where(mine, diag_qk, 0.0), axis=1, keepdims=True), 0.0)
                for (_, pair_mask, _), ql, kl in zip(levels, p["q_lv"], p["k_lv"]):
                    s = _dot_nt(jnp.where(mine, ql, 0.0).astype(BF16), kl)
                    att = att + jnp.where(pair_mask, s, 0.0)
                att_ref[pl.ds(r0, CHUNK), h * HEAD_DIM:h * HEAD_DIM + CHUNK] = att.astype(BF16)
                qin_ref[pl.ds(r0, CHUNK), hs] = jnp.where(mine, q_in, jnp.zeros_like(q_in))
                kout_ref[pl.ds(r0, CHUNK), hs] = jnp.where(mine, k_out, jnp.zeros_like(k_out))
            el_ref[pl.ds(pl.multiple_of(p["c"] * SUBLANES, SUBLANES), SUBLANES), p["ps"]] = jnp.broadcast_to(
                jnp.exp(b_last), (SUBLANES, HEAD_DIM))
        return 0

    lax.fori_loop(0, ts // (CHUNK * GLA_CHUNKS_PER_STEP), prepare, 0)

    def scan(c, _):
        r0 = pl.multiple_of(c * CHUNK, CHUNK)
        heads = [slice(h * HEAD_DIM, (h + 1) * HEAD_DIM) for h in range(N_HEADS)]
        states = [st_ref[h] for h in range(N_HEADS)]
        values = [v_ref[pl.ds(r0, CHUNK), hs] for hs in heads]
        inter = [_dot_nt(qin_ref[pl.ds(r0, CHUNK), hs], s.astype(BF16)) for hs, s in zip(heads, states)]
        intra = [_dot(att_ref[pl.ds(r0, CHUNK), h * HEAD_DIM:h * HEAD_DIM + CHUNK], v)
                 for h, v in enumerate(values)]
        upd = [_dot_tn(v, kout_ref[pl.ds(r0, CHUNK), hs]) for hs, v in zip(heads, values)]
        for h, hs in enumerate(heads):
            ps = slice((h // heads_per_pair) * HEAD_DIM, (h // heads_per_pair + 1) * HEAD_DIM)
            e_last = el_ref[pl.ds(pl.multiple_of(c * SUBLANES, SUBLANES), SUBLANES), ps][0:1, :]
            st_ref[h] = states[h] * e_last + upd[h]
            og = og_ref[pl.ds(r0, CHUNK), hs].astype(F32)
            o_ref[pl.ds(r0, CHUNK), hs] = (_rms(inter[h] + intra[h], gain_ref[...]) * _silu(og)).astype(o_ref.dtype)
        return 0

    lax.fori_loop(0, ts // CHUNK, scan, 0)


def _gla(proj, small, w2_padded, gate_bias, out_gain, batch, seq, *, ts):
    nt = seq // ts
    kern = functools.partial(_gla_kernel, ts=ts)
    qk_blk = COL_GL // GROUP
    return pl.pallas_call(
        kern,
        grid=(batch, nt),
        in_specs=[
            pl.BlockSpec((ts, GROUP), lambda b, t: (b * nt + t, qk_blk)),
            pl.BlockSpec((ts, GROUP), lambda b, t: (b * nt + t, qk_blk + 1)),
            pl.BlockSpec((ts, GROUP), lambda b, t: (b * nt + t, qk_blk + 2)),
            pl.BlockSpec((ts, SMALL_W), lambda b, t: (b * nt + t, 0)),
            pl.BlockSpec(w2_padded.shape, lambda b, t: (0, 0)),
            pl.BlockSpec((1, N_HEADS * GLA_DK), lambda b, t: (0, 0)),
            pl.BlockSpec((1, HEAD_DIM), lambda b, t: (0, 0)),
        ],
        out_specs=pl.BlockSpec((ts, GROUP), lambda b, t: (b * nt + t, 0)),
        out_shape=jax.ShapeDtypeStruct((batch * seq, GROUP), BF16),
        scratch_shapes=[
            pltpu.VMEM((ts, GROUP), BF16),
            pltpu.VMEM((ts, GROUP), BF16),
            pltpu.VMEM((ts, GROUP), BF16),
            pltpu.VMEM((ts // CHUNK * SUBLANES, N_HEADS * GLA_DK), F32),
            pltpu.VMEM((N_HEADS, HEAD_DIM, HEAD_DIM), F32),
        ],
        compiler_params=_cparams("parallel", "arbitrary"),
        name="gla",
    )(proj, proj, proj, small, w2_padded, gate_bias.reshape(1, -1), out_gain.reshape(1, HEAD_DIM))


def _out_proj_kernel(x_ref, m0_ref, m1_ref, m2_ref, m3_ref, w_ref, o_ref):
    acc = x_ref[...]
    for g, m_ref in enumerate((m0_ref, m1_ref, m2_ref, m3_ref)):
        acc = acc + _dot(m_ref[...], w_ref[g * GROUP:(g + 1) * GROUP, :])
    o_ref[...] = acc


def _out_proj(x, mixes, w_out, layer, *, tm):
    t, d = x.shape
    mix_spec = pl.BlockSpec((tm, GROUP), lambda i: (i, 0))
    return pl.pallas_call(
        _out_proj_kernel,
        grid=(t // tm,),
        in_specs=[pl.BlockSpec((tm, d), lambda i: (i, 0)), mix_spec, mix_spec, mix_spec, mix_spec,
                  pl.BlockSpec((None,) + w_out.shape[1:], lambda i: (layer, 0, 0))],
        out_specs=pl.BlockSpec((tm, d), lambda i: (i, 0)),
        out_shape=jax.ShapeDtypeStruct((t, d), F32),
        compiler_params=_cparams("parallel"),
        name="out_proj_residual",
    )(x, *mixes, w_out)


def _ffn_kernel(x_ref, g_ref, wg_ref, wu_ref, wd_ref, og_ref, o_ref, h_ref, *, norm_output):
    f = pl.program_id(1)

    @pl.when(f == 0)
    def _():
        x = x_ref[...]
        h_ref[...] = _rms(x, g_ref[...]).astype(BF16)
        o_ref[...] = x

    h = h_ref[...]
    hid = _silu(_dot(h, wg_ref[...])) * _dot(h, wu_ref[...])
    o_ref[...] += _dot(hid.astype(BF16), wd_ref[...])

    if norm_output:
        @pl.when(f == pl.num_programs(1) - 1)
        def _():
            o_ref[...] = _rms(o_ref[...], og_ref[...])


def _ffn(x, gain, w_gate, w_up, w_down, layer, out_gain, *, norm_output, tm, tf):
    t, d = x.shape
    hidden = w_gate.shape[2]
    return pl.pallas_call(
        functools.partial(_ffn_kernel, norm_output=norm_output),
        grid=(t // tm, hidden // tf),
        in_specs=[
            pl.BlockSpec((tm, d), lambda i, f: (i, 0)),
            pl.BlockSpec((1, d), lambda i, f: (0, 0)),
            pl.BlockSpec((None, d, tf), lambda i, f: (layer, 0, f)),
            pl.BlockSpec((None, d, tf), lambda i, f: (layer, 0, f)),
            pl.BlockSpec((None, tf, d), lambda i, f: (layer, f, 0)),
            pl.BlockSpec((1, d), lambda i, f: (0, 0)),
        ],
        out_specs=pl.BlockSpec((tm, d), lambda i, f: (i, 0)),
        out_shape=jax.ShapeDtypeStruct((t, d), F32),
        scratch_shapes=[pltpu.VMEM((tm, d), BF16)],
        compiler_params=_cparams("parallel", "arbitrary"),
        name="ffn_residual",
    )(x, gain.reshape(1, d), w_gate, w_up, w_down, out_gain.reshape(1, d))


def _pack_w_in(w_in):
    gd_small = COL_GD + 4 * GROUP
    df0 = gd_small + 2 * N_HEADS
    gl0 = df0 + 3 * GROUP
    rank0 = gl0 + 3 * GROUP
    pad = jnp.zeros(w_in.shape[:-1] + (IN_COLS_PACKED - COL_SMALL - 2 * N_HEADS - GLA_RANK,), BF16)
    parts = [w_in[..., :gd_small], w_in[..., df0:rank0], w_in[..., gd_small:df0], w_in[..., rank0:rank0 + GLA_RANK]]
    return jnp.concatenate([part.astype(BF16) for part in parts] + [pad], axis=-1)


def _mixer_layer(x, l, p, big, batch, seq):
    lam_init = 0.8 - 0.6 * math.exp(-0.3 * l)
    proj, small = _norm_in_proj(x, p["attn_norm"], big["w_in"], l, tm=1024, tn=1024)
    o_sb = _sb_attention(proj, batch, seq)
    head_params = jnp.zeros((SUBLANES, SMALL_W), F32)
    head_params = head_params.at[0, N_HEADS:2 * N_HEADS].set(p["gdn_a_log"])
    head_params = head_params.at[1, N_HEADS:2 * N_HEADS].set(p["gdn_dt_bias"])
    o_gd = _gdn(proj, small, p["gdn_conv_w"], head_params, p["gdn_out_norm"], batch, seq, ts=512)
    lam_vecs = jnp.stack([p["diff_lam_q1"], p["diff_lam_k1"], p["diff_lam_q2"], p["diff_lam_k2"]])
    o_df = _diff_attention(proj, lam_vecs, p["diff_out_norm"], lam_init, batch, seq)
    w2 = jnp.zeros((SMALL_W, N_HEADS * GLA_DK), F32).at[2 * N_HEADS:2 * N_HEADS + GLA_RANK].set(p["gla_gate_w2"])
    o_gl = _gla(proj, small, w2.astype(BF16), p["gla_gate_b"], p["gla_out_norm"], batch, seq, ts=512)
    return _out_proj(x, (o_sb, o_gd, o_df, o_gl), big["w_out"], l, tm=512)


def kernel(x, attn_norm, w_in, gdn_conv_w, gdn_a_log, gdn_dt_bias, gdn_out_norm, diff_lam_q1, diff_lam_k1,
           diff_lam_q2, diff_lam_k2, diff_out_norm, gla_gate_w2, gla_gate_b, gla_out_norm, w_out, ffn_norm,
           w_gate, w_up, w_down, final_norm):
    batch, seq, d = x.shape
    depth = w_in.shape[0]
    stacked = dict(attn_norm=attn_norm, gdn_conv_w=gdn_conv_w, gdn_a_log=gdn_a_log,
                   gdn_dt_bias=gdn_dt_bias, gdn_out_norm=gdn_out_norm, diff_lam_q1=diff_lam_q1,
                   diff_lam_k1=diff_lam_k1, diff_lam_q2=diff_lam_q2, diff_lam_k2=diff_lam_k2,
                   diff_out_norm=diff_out_norm, gla_gate_w2=gla_gate_w2, gla_gate_b=gla_gate_b,
                   gla_out_norm=gla_out_norm)
    big = dict(w_in=_pack_w_in(w_in), w_out=w_out.astype(BF16), w_gate=w_gate.astype(BF16),
               w_up=w_up.astype(BF16), w_down=w_down.astype(BF16))
    xt = x.reshape(batch * seq, d)
    for l in range(depth):
        p = {name: arr[l] for name, arr in stacked.items()}
        xt = _mixer_layer(xt, l, p, big, batch, seq)
        xt = _ffn(xt, ffn_norm[l], big["w_gate"], big["w_up"], big["w_down"], l, final_norm,
                  norm_output=(l == depth - 1), tm=1024, tf=512)
    return xt.reshape(batch, seq, d)
```

```python
import functools
import math

import jax
import jax.numpy as jnp
from jax import lax
from jax.experimental import pallas as pl
from jax.experimental.pallas import tpu as pltpu

F32 = jnp.float32
BF16 = jnp.bfloat16

NORM_EPS = 1e-6
LOG2_E = math.log2(math.e)
SB_DEAD_LOG2 = 110.0 * LOG2_E
N_HEADS = 4
HEAD_DIM = 128
GROUP = N_HEADS * HEAD_DIM
CHUNK = 64
CONV_TAPS = 4
GLA_DK = 64
GLA_RANK = 16
GLA_TAU = 16.0
ATT_BLOCK = 256
CONV_HALO = 8
SUBLANES = 8
GDN_CHUNKS_PER_STEP = 4
GDN_SCAN_CHUNKS_PER_STEP = 4
GLA_CHUNKS_PER_STEP = 4
GLA_SCAN_CHUNKS_PER_STEP = 4
DIFF_HEADS_PER_STEP = 4
SB_HEADS_PER_STEP = 4
COL_SB = 0
COL_GD = 3 * GROUP
COL_DF = 7 * GROUP
COL_GL = 10 * GROUP
COL_SMALL = 13 * GROUP
SMALL_W = 128
IN_COLS_PACKED = 14 * GROUP
VMEM_LIMIT = 56 * 1024 * 1024


def _cparams(*sem):
    return pltpu.CompilerParams(dimension_semantics=sem, vmem_limit_bytes=VMEM_LIMIT)


def _dot(a, b):
    return jnp.dot(a, b, preferred_element_type=F32)


def _dot_nt(a, b):
    return lax.dot_general(a, b, (((1,), (1,)), ((), ())), preferred_element_type=F32)


def _dot_tn(a, b):
    return lax.dot_general(a, b, (((0,), (0,)), ((), ())), preferred_element_type=F32)


def _split(x):
    hi = x.astype(BF16)
    return hi, (x - hi.astype(F32)).astype(BF16)


def _dot_sel(sel16, x):
    hi, lo = _split(x)
    return _dot(sel16, hi) + _dot(sel16, lo)


def _dot_unit_left(unit16, x):
    hi, lo = _split(x)
    return _dot(unit16, hi) + _dot(unit16, lo)


def _dot_unit_right(x, unit16):
    hi, lo = _split(x)
    return _dot(hi, unit16) + _dot(lo, unit16)


def _rms(x, gain):
    return x * lax.rsqrt(jnp.mean(x * x, axis=-1, keepdims=True) + NORM_EPS) * gain


def _silu(x):
    return x / (1.0 + jnp.exp(-x))


def _sigmoid(x):
    return 1.0 / (1.0 + jnp.exp(-x))


def _softplus(x):
    return jnp.maximum(x, 0.0) + jnp.log(1.0 + jnp.exp(-jnp.abs(x)))


def _norm_matmul_kernel(x_ref, g_ref, w_ref, o_ref, small_ref, h_ref):
    j = pl.program_id(1)
    last = pl.num_programs(1) - 1

    @pl.when(j == 0)
    def _():
        h_ref[...] = _rms(x_ref[...], g_ref[...]).astype(BF16)

    y = _dot(h_ref[...], w_ref[...])
    o_ref[...] = y.astype(o_ref.dtype)

    @pl.when(j == last)
    def _():
        small_ref[...] = y[:, y.shape[1] - small_ref.shape[1]:]


def _norm_in_proj(x, gain, w, layer, *, tm, tn):
    t, d = x.shape
    return pl.pallas_call(
        _norm_matmul_kernel,
        grid=(t // tm, IN_COLS_PACKED // tn),
        in_specs=[
            pl.BlockSpec((tm, d), lambda i, j: (i, 0)),
            pl.BlockSpec((1, d), lambda i, j: (0, 0)),
            pl.BlockSpec((None, d, tn), lambda i, j: (layer, 0, j)),
        ],
        out_specs=[
            pl.BlockSpec((tm, tn), lambda i, j: (i, j)),
            pl.BlockSpec((tm, IN_COLS_PACKED - COL_SMALL), lambda i, j: (i, 0)),
        ],
        out_shape=[jax.ShapeDtypeStruct((t, IN_COLS_PACKED), BF16),
                   jax.ShapeDtypeStruct((t, IN_COLS_PACKED - COL_SMALL), F32)],
        scratch_shapes=[pltpu.VMEM((tm, d), BF16)],
        compiler_params=_cparams("parallel", "arbitrary"),
        name="norm_in_proj",
    )(x, gain.reshape(1, d), w)


def _sb_kernel(q_ref, k_ref, v_ref, o_ref, *, blk, scale):
    i = pl.program_id(2)
    heads = [slice(h * HEAD_DIM, (h + 1) * HEAD_DIM) for h in range(SB_HEADS_PER_STEP)]
    row = lax.broadcasted_iota(jnp.int32, (blk, blk), 0)
    col = lax.broadcasted_iota(jnp.int32, (blk, blk), 1)
    suffix = jnp.where(row > col, 1.0, 0.0).astype(BF16)
    lead = jnp.maximum(i - 1, 0)
    col_minus_row = (lax.broadcasted_iota(jnp.int32, (blk, 2 * blk), 1)
                     - lax.broadcasted_iota(jnp.int32, (blk, 2 * blk), 0))
    causal = col_minus_row < (i - lead) * blk

    def span(first_block, nblk, state, masked):
        rows = pl.ds(pl.multiple_of(first_block * blk, blk), nblk * blk)
        parts = [slice(b * blk, (b + 1) * blk) for b in range(nblk)]
        zs = [_dot_nt(q_ref[:, hs], k_ref[rows, hs]) * (scale * LOG2_E) for hs in heads]
        sps = []
        for z in zs:
            sp = jnp.maximum(z, 0.0) + jnp.log2(1.0 + jnp.exp2(-jnp.abs(z)))
            sps.append(jnp.where(causal, sp, 0.0) if masked else sp)
        tails = [[_dot_sel_rhs(sp[:, s], suffix) for s in parts] for sp in sps]
        out = []
        for hs, z, sp, tail, (carry, acc) in zip(heads, zs, sps, tails, state):
            sums = [jnp.sum(sp[:, s], axis=1, keepdims=True) for s in parts]
            offsets = [None] * nblk
            for b in reversed(range(nblk)):
                offsets[b] = carry
                carry = carry + sums[b]
            att = jnp.concatenate(
                [jnp.exp2(z[:, s] - sp[:, s] - tail[b] - offsets[b])
                 for b, s in enumerate(parts)], axis=1)
            if masked:
                att = jnp.where(causal, att, 0.0)
            out.append((carry, acc + _dot(att.astype(BF16), v_ref[rows, hs])))
        return tuple(out)

    init = (jnp.zeros((blk, 1), F32), jnp.zeros((blk, HEAD_DIM), F32))
    state = span(lead, 2, (init,) * len(heads), True)

    def alive(state):
        low = state[0][0]
        for carry, _ in state[1:]:
            low = jnp.minimum(low, carry)
        return jnp.min(low) < SB_DEAD_LOG2

    _, state = lax.while_loop(
        lambda carried: jnp.logical_and(carried[0] < lead // 2, alive(carried[1])),
        lambda carried: (carried[0] + 1, span(lead - 2 - 2 * carried[0], 2, carried[1], False)),
        (jnp.int32(0), state))
    odd_left = jnp.where(alive(state), lead % 2, 0)
    state = lax.fori_loop(0, odd_left, lambda n, st: span(0, 1, st, False), state)
    for hs, (_, acc) in zip(heads, state):
        o_ref[:, hs] = acc.astype(o_ref.dtype)


def _dot_sel_rhs(x, sel16):
    hi, lo = _split(x)
    return _dot(hi, sel16) + _dot(lo, sel16)


def _sb_attention(proj, batch, seq):
    blk = ATT_BLOCK
    nq = seq // blk
    width = SB_HEADS_PER_STEP * HEAD_DIM
    qb, kb, vb = COL_SB // width, (COL_SB + GROUP) // width, (COL_SB + 2 * GROUP) // width
    kern = functools.partial(_sb_kernel, blk=blk, scale=HEAD_DIM ** -0.5)
    return pl.pallas_call(
        kern,
        grid=(batch, N_HEADS // SB_HEADS_PER_STEP, nq),
        in_specs=[
            pl.BlockSpec((blk, width), lambda b, h, i: (b * nq + i, qb + h)),
            pl.BlockSpec((seq, width), lambda b, h, i: (b, kb + h)),
            pl.BlockSpec((seq, width), lambda b, h, i: (b, vb + h)),
        ],
        out_specs=pl.BlockSpec((blk, width), lambda b, h, i: (b * nq + i, h)),
        out_shape=jax.ShapeDtypeStruct((batch * seq, GROUP), BF16),
        compiler_params=_cparams("parallel", "parallel", "arbitrary"),
        name="sb_attention",
    )(proj, proj, proj)


def _diff_kernel(lam_ref, q_ref, k_ref, v_ref, g_ref, o_ref, *, blk, scale, lam_init):
    i = pl.program_id(2)
    half = HEAD_DIM // 2
    lane = lax.broadcasted_iota(jnp.int32, (blk, HEAD_DIM), 1)
    heads = [slice(h * HEAD_DIM, (h + 1) * HEAD_DIM) for h in range(DIFF_HEADS_PER_STEP)]
    qs = []
    for hs in heads:
        q = q_ref[:, hs].astype(F32) * scale
        qs.append(jnp.concatenate([jnp.where(lane < half, q, 0.0), jnp.where(lane >= half, q, 0.0)],
                                  axis=0).astype(BF16))
    lead = jnp.maximum(i - 1, 0)
    row = lax.broadcasted_iota(jnp.int32, (2 * blk, 2 * blk), 0) & (blk - 1)
    col = lax.broadcasted_iota(jnp.int32, (2 * blk, 2 * blk), 1)
    causal = col - row <= (i - lead) * blk

    def softmax(s, state, masked):
        m, l_lanes, acc = state
        if masked:
            s = jnp.where(causal, s, -jnp.inf)
        m_new = jnp.maximum(m, jnp.max(s, axis=1, keepdims=True))
        alpha = jnp.exp(m - m_new)
        p = jnp.exp(s - m_new)
        p_lanes = p[:, 0:HEAD_DIM]
        for c in range(1, s.shape[1] // HEAD_DIM):
            p_lanes = p_lanes + p[:, c * HEAD_DIM:(c + 1) * HEAD_DIM]
        return m_new, alpha * l_lanes + p_lanes, alpha * acc, p.astype(BF16)

    def span(first_block, nblk, states, masked):
        rows = pl.ds(pl.multiple_of(first_block * blk, blk), nblk * blk)
        out = []
        s_next = _dot_nt(qs[0], k_ref[rows, heads[0]])
        for h, (hs, state) in enumerate(zip(heads, states)):
            s = s_next
            if h + 1 < len(heads):
                s_next = _dot_nt(qs[h + 1], k_ref[rows, heads[h + 1]])
            m, l_lanes, acc, p = softmax(s, state, masked)
            out.append((m, l_lanes, acc + _dot(p, v_ref[rows, hs])))
        return tuple(out)

    init = (jnp.full((2 * blk, 1), -jnp.inf, F32), jnp.zeros((2 * blk, HEAD_DIM), F32),
            jnp.zeros((2 * blk, HEAD_DIM), F32))
    states = span(lead, 2, (init,) * len(heads), True)
    states = lax.fori_loop(0, lead // 2, lambda n, st: span(2 * n, 2, st, False), states)
    states = lax.fori_loop(0, lead % 2, lambda n, st: span(lead - 1, 1, st, False), states)

    lam_v = lam_ref[...]
    d1 = jnp.sum(lam_v[0:1, :] * lam_v[1:2, :], axis=1, keepdims=True)
    d2 = jnp.sum(lam_v[2:3, :] * lam_v[3:4, :], axis=1, keepdims=True)
    lam = jnp.exp(d1) - jnp.exp(d2) + lam_init
    for hs, (_, l_lanes, acc) in zip(heads, states):
        norm = acc / jnp.sum(l_lanes, axis=1, keepdims=True)
        out = norm[:blk] - lam * norm[blk:]
        o_ref[:, hs] = (_rms(out, g_ref[...]) * (1.0 - lam_init)).astype(o_ref.dtype)


def _diff_attention(proj, lam_vecs, out_gain, lam_init, batch, seq):
    blk = ATT_BLOCK
    nq = seq // blk
    width = DIFF_HEADS_PER_STEP * HEAD_DIM
    qb, kb, vb = COL_DF // width, (COL_DF + GROUP) // width, (COL_DF + 2 * GROUP) // width
    kern = functools.partial(_diff_kernel, blk=blk, scale=(HEAD_DIM // 2) ** -0.5, lam_init=lam_init)
    return pl.pallas_call(
        kern,
        grid=(batch, N_HEADS // DIFF_HEADS_PER_STEP, nq),
        in_specs=[
            pl.BlockSpec(lam_vecs.shape, lambda b, h, i: (0, 0)),
            pl.BlockSpec((blk, width), lambda b, h, i: (b * nq + i, qb + h)),
            pl.BlockSpec((seq, width), lambda b, h, i: (b, kb + h)),
            pl.BlockSpec((seq, width), lambda b, h, i: (b, vb + h)),
            pl.BlockSpec((1, HEAD_DIM), lambda b, h, i: (0, 0)),
        ],
        out_specs=pl.BlockSpec((blk, width), lambda b, h, i: (b * nq + i, h)),
        out_shape=jax.ShapeDtypeStruct((batch * seq, GROUP), BF16),
        compiler_params=_cparams("parallel", "parallel", "arbitrary"),
        name="diff_attention",
    )(lam_vecs, proj, proj, proj, out_gain.reshape(1, HEAD_DIM))


def _chunk_iotas(width=CHUNK):
    row = lax.broadcasted_iota(jnp.int32, (CHUNK, width), 0)
    col = lax.broadcasted_iota(jnp.int32, (CHUNK, width), 1)
    return row, col


def _lower_left_blocks(row, col, level):
    same_pair = (row >> (level + 1)) == (col >> (level + 1))
    return same_pair & (((row >> level) & 1) == 1) & (((col >> level) & 1) == 0)


def _unit_lower_inverses(a_list, row, col):
    eye = jnp.where(row == col, 1.0, 0.0).astype(F32)
    invs = [eye - jnp.where(_lower_left_blocks(row, col, 0), a, 0.0) for a in a_list]
    for level in range(1, CHUNK.bit_length() - 1):
        mask = _lower_left_blocks(row, col, level)
        inv16 = [inv.astype(BF16) for inv in invs]
        left = [_dot_unit_left(i16, jnp.where(mask, a, 0.0)) for i16, a in zip(inv16, a_list)]
        invs = [inv - _dot_unit_right(x, i16) for inv, x, i16 in zip(invs, left, inv16)]
    return invs


def _gdn_kernel(x_ref, z_ref, sm_ref, cw_ref, hp_ref, gain_ref, o_ref,
                xs_ref, qn_ref, kn_ref, vn_ref, u_ref, w_ref, qg_ref, kd_ref, qk_ref, el_ref, st_ref, *, ts):
    t = pl.program_id(1)

    @pl.when(t == 0)
    def _():
        xs_ref[0:CONV_HALO, :] = jnp.zeros((CONV_HALO, 3 * GROUP), F32)
        st_ref[...] = jnp.zeros_like(st_ref)

    xs_ref[CONV_HALO:CONV_HALO + ts, :] = x_ref[...].astype(F32)

    for part, dst in enumerate((qn_ref, kn_ref, vn_ref)):
        for h in range(N_HEADS):
            c0 = part * GROUP + h * HEAD_DIM
            acc = jnp.zeros((ts, HEAD_DIM), F32)
            for j in range(CONV_TAPS):
                off = CONV_HALO - (CONV_TAPS - 1) + j
                acc = acc + xs_ref[off:off + ts, c0:c0 + HEAD_DIM] * cw_ref[j:j + 1, c0:c0 + HEAD_DIM]
            y = _silu(acc)
            if part < 2:
                y = y * lax.rsqrt(jnp.sum(y * y, axis=-1, keepdims=True) + NORM_EPS)
            if part == 0:
                y = y * (HEAD_DIM ** -0.5)
            dst[:, h * HEAD_DIM:(h + 1) * HEAD_DIM] = y

    xs_ref[0:CONV_HALO, :] = xs_ref[ts:ts + CONV_HALO, :]

    row, col = _chunk_iotas()
    strict = row > col
    row_w, col_w = _chunk_iotas(HEAD_DIM)
    incl_w = row_w >= col_w
    eye_w = row_w == col_w
    lower_ones = jnp.where(row >= col, 1.0, 0.0).astype(BF16)
    all_ones = jnp.ones((CHUNK, CHUNK), BF16)

    def prepare(pair, _):
        gates = []
        for sub in range(GDN_CHUNKS_PER_STEP):
            c = pair * GDN_CHUNKS_PER_STEP + sub
            r0 = pl.multiple_of(c * CHUNK, CHUNK)
            sm = sm_ref[pl.ds(r0, CHUNK), :]
            beta_all = _sigmoid(sm)
            g_all = -jnp.exp(hp_ref[0:1, :]) * _softplus(sm + hp_ref[1:2, :])
            g_b = jnp.concatenate(
                [jnp.broadcast_to(g_all[:, N_HEADS + h:N_HEADS + h + 1], (CHUNK, HEAD_DIM))
                 for h in range(N_HEADS)], axis=1)
            gc_all = _dot_sel(lower_ones, g_b)
            gates.append((c, r0, beta_all, gc_all))
        gc_rows = []
        for _, _, _, gc_all in gates:
            gc_diag = jnp.concatenate(
                [jnp.where(eye_w, gc_all[:, h * HEAD_DIM:(h + 1) * HEAD_DIM], 0.0) for h in range(N_HEADS)], axis=1)
            gc_rows.append(_dot_sel(all_ones, gc_diag))

        probs = []
        for (c, r0, beta_all, gc_all), gc_row_all in zip(gates, gc_rows):
            for h in range(N_HEADS):
                hs = slice(h * HEAD_DIM, (h + 1) * HEAD_DIM)
                beta = beta_all[:, h:h + 1]
                gc = gc_all[:, hs]
                decay = jnp.exp(jnp.where(incl_w, gc - gc_row_all[:, hs], -jnp.inf))[:, :CHUNK]
                q = qn_ref[pl.ds(r0, CHUNK), hs]
                k = kn_ref[pl.ds(r0, CHUNK), hs]
                v = vn_ref[pl.ds(r0, CHUNK), hs]
                kb = k * beta
                k16 = k.astype(BF16)
                a_mat = jnp.where(strict, _dot_nt(kb.astype(BF16), k16) * decay, 0.0)
                qk = (_dot_nt(q.astype(BF16), k16) * decay).astype(BF16)
                probs.append(dict(c=c, r0=r0, h=h, hs=hs, gc=gc, q=q, k=k, kb=kb, vb=v * beta, a=a_mat, qk=qk))

        invs = _unit_lower_inverses([p["a"] for p in probs], row, col)
        uws = [_dot_unit_left(inv.astype(BF16), jnp.concatenate([p["vb"], p["kb"] * jnp.exp(p["gc"])], axis=1))
               for inv, p in zip(invs, probs)]
        for p, uw in zip(probs, uws):
            r0, hs, gc, h = p["r0"], p["hs"], p["gc"], p["h"]
            g_last = gc[CHUNK - 1:CHUNK, :]
            u_ref[pl.ds(r0, CHUNK), hs] = uw[:, :HEAD_DIM]
            w_ref[pl.ds(r0, CHUNK), hs] = uw[:, HEAD_DIM:].astype(BF16)
            qg_ref[pl.ds(r0, CHUNK), hs] = (p["q"] * jnp.exp(gc)).astype(BF16)
            kd_ref[pl.ds(r0, CHUNK), hs] = (p["k"] * jnp.exp(g_last - gc)).astype(BF16)
            qk_ref[pl.ds(r0, CHUNK), h * HEAD_DIM:h * HEAD_DIM + CHUNK] = p["qk"]
            el_ref[pl.ds(pl.multiple_of(p["c"] * SUBLANES, SUBLANES), SUBLANES), hs] = jnp.broadcast_to(
                jnp.exp(g_last), (SUBLANES, HEAD_DIM))
        return 0

    lax.fori_loop(0, ts // (CHUNK * GDN_CHUNKS_PER_STEP), prepare, 0)

    def scan(step, _):
        heads = [slice(h * HEAD_DIM, (h + 1) * HEAD_DIM) for h in range(N_HEADS)]
        states = [st_ref[h] for h in range(N_HEADS)]
        for sub in range(GDN_SCAN_CHUNKS_PER_STEP):
            c = step * GDN_SCAN_CHUNKS_PER_STEP + sub
            r0 = pl.multiple_of(c * CHUNK, CHUNK)
            s16 = [s.astype(BF16) for s in states]
            ws = [_dot(w_ref[pl.ds(r0, CHUNK), hs], s) for hs, s in zip(heads, s16)]
            inter = [_dot(qg_ref[pl.ds(r0, CHUNK), hs], s) for hs, s in zip(heads, s16)]
            v_new = [(u_ref[pl.ds(r0, CHUNK), hs] - x).astype(BF16) for hs, x in zip(heads, ws)]
            upd = [_dot_tn(kd_ref[pl.ds(r0, CHUNK), hs], vn) for hs, vn in zip(heads, v_new)]
            for h, hs in enumerate(heads):
                e_last = el_ref[pl.ds(pl.multiple_of(c * SUBLANES, SUBLANES), SUBLANES), hs][0:1, :]
                states[h] = states[h] * e_last + upd[h]
            intra = [_dot(qk_ref[pl.ds(r0, CHUNK), h * HEAD_DIM:h * HEAD_DIM + CHUNK], vn)
                     for h, vn in enumerate(v_new)]
            for h, hs in enumerate(heads):
                zg = z_ref[pl.ds(r0, CHUNK), hs].astype(F32)
                o_ref[pl.ds(r0, CHUNK), hs] = (
                    _rms(inter[h] + intra[h], gain_ref[...]) * _silu(zg)).astype(o_ref.dtype)
        for h in range(N_HEADS):
            st_ref[h] = states[h]
        return 0

    lax.fori_loop(0, ts // (CHUNK * GDN_SCAN_CHUNKS_PER_STEP), scan, 0)


def _gdn(proj, small, conv_w, head_params, out_gain, batch, seq, *, ts):
    nt = seq // ts
    kern = functools.partial(_gdn_kernel, ts=ts)
    qkv_blk = COL_GD // (3 * GROUP)
    z_blk = (COL_GD + 3 * GROUP) // GROUP
    return pl.pallas_call(
        kern,
        grid=(batch, nt),
        in_specs=[
            pl.BlockSpec((ts, 3 * GROUP), lambda b, t: (b * nt + t, qkv_blk)),
            pl.BlockSpec((ts, GROUP), lambda b, t: (b * nt + t, z_blk)),
            pl.BlockSpec((ts, SMALL_W), lambda b, t: (b * nt + t, 0)),
            pl.BlockSpec((CONV_TAPS, 3 * GROUP), lambda b, t: (0, 0)),
            pl.BlockSpec(head_params.shape, lambda b, t: (0, 0)),
            pl.BlockSpec((1, HEAD_DIM), lambda b, t: (0, 0)),
        ],
        out_specs=pl.BlockSpec((ts, GROUP), lambda b, t: (b * nt + t, 0)),
        out_shape=jax.ShapeDtypeStruct((batch * seq, GROUP), BF16),
        scratch_shapes=[
            pltpu.VMEM((ts + CONV_HALO, 3 * GROUP), F32),
            pltpu.VMEM((ts, GROUP), F32),
            pltpu.VMEM((ts, GROUP), F32),
            pltpu.VMEM((ts, GROUP), F32),
            pltpu.VMEM((ts, GROUP), F32),
            pltpu.VMEM((ts, GROUP), BF16),
            pltpu.VMEM((ts, GROUP), BF16),
            pltpu.VMEM((ts, GROUP), BF16),
            pltpu.VMEM((ts, GROUP), BF16),
            pltpu.VMEM((ts // CHUNK * SUBLANES, GROUP), F32),
            pltpu.VMEM((N_HEADS, HEAD_DIM, HEAD_DIM), F32),
        ],
        compiler_params=_cparams("parallel", "arbitrary"),
        name="gated_deltanet",
    )(proj, proj, small, conv_w, head_params, out_gain.reshape(1, HEAD_DIM))


def _gla_kernel(qk_ref, v_ref, og_ref, sm_ref, w2_ref, gb_ref, gain_ref, o_ref,
                att_ref, qin_ref, kout_ref, el_ref, st_ref, *, ts):
    t = pl.program_id(1)

    @pl.when(t == 0)
    def _():
        st_ref[...] = jnp.zeros_like(st_ref)

    row, col = _chunk_iotas()
    lower_ones = jnp.where(row >= col, 1.0, 0.0).astype(BF16)
    eye = row == col
    row_w, lane = _chunk_iotas(HEAD_DIM)
    qk_scale = GLA_DK ** -0.5

    levels = []
    for level in range(CHUNK.bit_length() - 1):
        ref_row = ((row >> (level + 1)) << (level + 1)) + (1 << level)
        pick = jnp.where(col == ref_row, 1.0, 0.0).astype(BF16)
        upper_w = ((row_w >> level) & 1) == 1
        levels.append((pick, _lower_left_blocks(row, col, level), upper_w))

    n_pairs = N_HEADS * GLA_DK // HEAD_DIM
    heads_per_pair = HEAD_DIM // GLA_DK

    def prepare(step, _):
        cums = []
        for sub in range(GLA_CHUNKS_PER_STEP):
            c = step * GLA_CHUNKS_PER_STEP + sub
            r0 = pl.multiple_of(c * CHUNK, CHUNK)
            sm = sm_ref[pl.ds(r0, CHUNK), :].astype(BF16)
            gate = _dot(sm, w2_ref[...]) + gb_ref[...]
            log_a = (gate - _softplus(gate)) * (1.0 / GLA_TAU)
            cums.append((c, r0, _dot_sel(lower_ones, log_a)))
        probs = []
        for c, r0, b_all in cums:
            for pair in range(n_pairs):
                ps = slice(pair * HEAD_DIM, (pair + 1) * HEAD_DIM)
                b = b_all[:, ps]
                q2 = qk_ref[pl.ds(r0, CHUNK), ps].astype(F32) * qk_scale
                k2 = qk_ref[pl.ds(r0, CHUNK),
                            GROUP // 2 + pair * HEAD_DIM:GROUP // 2 + (pair + 1) * HEAD_DIM].astype(F32)
                probs.append(dict(c=c, r0=r0, pair=pair, ps=ps, b=b, q2=q2, k2=k2,
                                  b_refs=[_dot_sel(pick, b) for pick, _, _ in levels]))
        for p in probs:
            b, q2, k2 = p["b"], p["q2"], p["k2"]
            p["q_lv"] = [jnp.where(upper_w, q2 * jnp.exp(jnp.where(upper_w, b - b_ref, 0.0)), 0.0)
                         for (_, _, upper_w), b_ref in zip(levels, p["b_refs"])]
            p["k_lv"] = [jnp.where(upper_w, 0.0, k2 * jnp.exp(jnp.where(upper_w, 0.0, b_ref - b))).astype(BF16)
                         for (_, _, upper_w), b_ref in zip(levels, p["b_refs"])]
        for p in probs:
            r0, b, q2, k2 = p["r0"], p["b"], p["q2"], p["k2"]
            b_last = b[CHUNK - 1:CHUNK, :]
            q_in = (q2 * jnp.exp(b)).astype(BF16)
            k_out = (k2 * jnp.exp(b_last - b)).astype(BF16)
            diag_qk = q2 * k2
            for r in range(heads_per_pair):
                h = p["pair"] * heads_per_pair + r
                hs = slice(h * HEAD_DIM, (h + 1) * HEAD_DIM)
                mine = (lane >= r * GLA_DK) & (lane < (r + 1) * GLA_DK)
                att = jnp.where(eye, jnp.sum(jnp.where(mine, diag_qk, 0.0), axis=1, keepdims=True), 0.0)
                for (_, pair_mask, _), ql, kl in zip(levels, p["q_lv"], p["k_lv"]):
                    s = _dot_nt(jnp.where(mine, ql, 0.0).astype(BF16), kl)
                    att = att + jnp.where(pair_mask, s, 0.0)
                att_ref[pl.ds(r0, CHUNK), h * HEAD_DIM:h * HEAD_DIM + CHUNK] = att.astype(BF16)
                qin_ref[pl.ds(r0, CHUNK), hs] = jnp.where(mine, q_in, jnp.zeros_like(q_in))
                kout_ref[pl.ds(r0, CHUNK), hs] = jnp.where(mine, k_out, jnp.zeros_like(k_out))
            el_ref[pl.ds(pl.multiple_of(p["c"] * SUBLANES, SUBLANES), SUBLANES), p["ps"]] = jnp.broadcast_to(
                jnp.exp(b_last), (SUBLANES, HEAD_DIM))
        return 0

    lax.fori_loop(0, ts // (CHUNK * GLA_CHUNKS_PER_STEP), prepare, 0)

    def scan(step, _):
        heads = [slice(h * HEAD_DIM, (h + 1) * HEAD_DIM) for h in range(N_HEADS)]
        chunks = []
        for sub in range(GLA_SCAN_CHUNKS_PER_STEP):
            c = step * GLA_SCAN_CHUNKS_PER_STEP + sub
            r0 = pl.multiple_of(c * CHUNK, CHUNK)
            values = [v_ref[pl.ds(r0, CHUNK), hs] for hs in heads]
            intra = [_dot(att_ref[pl.ds(r0, CHUNK), h * HEAD_DIM:h * HEAD_DIM + CHUNK], v)
                     for h, v in enumerate(values)]
            upd = [_dot_tn(v, kout_ref[pl.ds(r0, CHUNK), hs]) for hs, v in zip(heads, values)]
            chunks.append((c, r0, intra, upd))
        states = [st_ref[h] for h in range(N_HEADS)]
        for c, r0, intra, upd in chunks:
            inter = [_dot_nt(qin_ref[pl.ds(r0, CHUNK), hs], s.astype(BF16)) for hs, s in zip(heads, states)]
            for h, hs in enumerate(heads):
                ps = slice((h // heads_per_pair) * HEAD_DIM, (h // heads_per_pair + 1) * HEAD_DIM)
                e_last = el_ref[pl.ds(pl.multiple_of(c * SUBLANES, SUBLANES), SUBLANES), ps][0:1, :]
                states[h] = states[h] * e_last + upd[h]
                og = og_ref[pl.ds(r0, CHUNK), hs].astype(F32)
                o_ref[pl.ds(r0, CHUNK), hs] = (
                    _rms(inter[h] + intra[h], gain_ref[...]) * _silu(og)).astype(o_ref.dtype)
        for h in range(N_HEADS):
            st_ref[h] = states[h]
        return 0

    lax.fori_loop(0, ts // (CHUNK * GLA_SCAN_CHUNKS_PER_STEP), scan, 0)


def _gla(proj, small, w2_padded, gate_bias, out_gain, batch, seq, *, ts):
    nt = seq // ts
    kern = functools.partial(_gla_kernel, ts=ts)
    qk_blk = COL_GL // GROUP
    return pl.pallas_call(
        kern,
        grid=(batch, nt),
        in_specs=[
            pl.BlockSpec((ts, GROUP), lambda b, t: (b * nt + t, qk_blk)),
            pl.BlockSpec((ts, GROUP), lambda b, t: (b * nt + t, qk_blk + 1)),
            pl.BlockSpec((ts, GROUP), lambda b, t: (b * nt + t, qk_blk + 2)),
            pl.BlockSpec((ts, SMALL_W), lambda b, t: (b * nt + t, 0)),
            pl.BlockSpec(w2_padded.shape, lambda b, t: (0, 0)),
            pl.BlockSpec((1, N_HEADS * GLA_DK), lambda b, t: (0, 0)),
            pl.BlockSpec((1, HEAD_DIM), lambda b, t: (0, 0)),
        ],
        out_specs=pl.BlockSpec((ts, GROUP), lambda b, t: (b * nt + t, 0)),
        out_shape=jax.ShapeDtypeStruct((batch * seq, GROUP), BF16),
        scratch_shapes=[
            pltpu.VMEM((ts, GROUP), BF16),
            pltpu.VMEM((ts, GROUP), BF16),
            pltpu.VMEM((ts, GROUP), BF16),
            pltpu.VMEM((ts // CHUNK * SUBLANES, N_HEADS * GLA_DK), F32),
            pltpu.VMEM((N_HEADS, HEAD_DIM, HEAD_DIM), F32),
        ],
        compiler_params=_cparams("parallel", "arbitrary"),
        name="gla",
    )(proj, proj, proj, small, w2_padded, gate_bias.reshape(1, -1), out_gain.reshape(1, HEAD_DIM))


def _out_proj_kernel(x_ref, m0_ref, m1_ref, m2_ref, m3_ref, w_ref, o_ref):
    acc = x_ref[...]
    for g, m_ref in enumerate((m0_ref, m1_ref, m2_ref, m3_ref)):
        acc = acc + _dot(m_ref[...], w_ref[g * GROUP:(g + 1) * GROUP, :])
    o_ref[...] = acc


def _out_proj(x, mixes, w_out, layer, *, tm):
    t, d = x.shape
    mix_spec = pl.BlockSpec((tm, GROUP), lambda i: (i, 0))
    return pl.pallas_call(
        _out_proj_kernel,
        grid=(t // tm,),
        in_specs=[pl.BlockSpec((tm, d), lambda i: (i, 0)), mix_spec, mix_spec, mix_spec, mix_spec,
                  pl.BlockSpec((None,) + w_out.shape[1:], lambda i: (layer, 0, 0))],
        out_specs=pl.BlockSpec((tm, d), lambda i: (i, 0)),
        out_shape=jax.ShapeDtypeStruct((t, d), F32),
        compiler_params=_cparams("parallel"),
        name="out_proj_residual",
    )(x, *mixes, w_out)


def _ffn_kernel(x_ref, g_ref, wg_ref, wu_ref, wd_ref, og_ref, o_ref, h_ref, *, norm_output):
    f = pl.program_id(1)

    @pl.when(f == 0)
    def _():
        x = x_ref[...]
        h_ref[...] = _rms(x, g_ref[...]).astype(BF16)
        o_ref[...] = x

    h = h_ref[...]
    hid = _silu(_dot(h, wg_ref[...])) * _dot(h, wu_ref[...])
    o_ref[...] += _dot(hid.astype(BF16), wd_ref[...])

    if norm_output:
        @pl.when(f == pl.num_programs(1) - 1)
        def _():
            o_ref[...] = _rms(o_ref[...], og_ref[...])


def _ffn(x, gain, w_gate, w_up, w_down, layer, out_gain, *, norm_output, tm, tf):
    t, d = x.shape
    hidden = w_gate.shape[2]
    return pl.pallas_call(
        functools.partial(_ffn_kernel, norm_output=norm_output),
        grid=(t // tm, hidden // tf),
        in_specs=[
            pl.BlockSpec((tm, d), lambda i, f: (i, 0)),
            pl.BlockSpec((1, d), lambda i, f: (0, 0)),
            pl.BlockSpec((None, d, tf), lambda i, f: (layer, 0, f)),
            pl.BlockSpec((None, d, tf), lambda i, f: (layer, 0, f)),
            pl.BlockSpec((None, tf, d), lambda i, f: (layer, f, 0)),
            pl.BlockSpec((1, d), lambda i, f: (0, 0)),
        ],
        out_specs=pl.BlockSpec((tm, d), lambda i, f: (i, 0)),
        out_shape=jax.ShapeDtypeStruct((t, d), F32),
        scratch_shapes=[pltpu.VMEM((tm, d), BF16)],
        compiler_params=_cparams("parallel", "arbitrary"),
        name="ffn_residual",
    )(x, gain.reshape(1, d), w_gate, w_up, w_down, out_gain.reshape(1, d))


def _pack_w_in(w_in):
    gd_small = COL_GD + 4 * GROUP
    df0 = gd_small + 2 * N_HEADS
    gl0 = df0 + 3 * GROUP
    rank0 = gl0 + 3 * GROUP
    pad = jnp.zeros(w_in.shape[:-1] + (IN_COLS_PACKED - COL_SMALL - 2 * N_HEADS - GLA_RANK,), BF16)
    parts = [w_in[..., :gd_small], w_in[..., df0:rank0], w_in[..., gd_small:df0], w_in[..., rank0:rank0 + GLA_RANK]]
    return jnp.concatenate([part.astype(BF16) for part in parts] + [pad], axis=-1)


def _mixer_layer(x, l, p, big, batch, seq):
    lam_init = 0.8 - 0.6 * math.exp(-0.3 * l)
    proj, small = _norm_in_proj(x, p["attn_norm"], big["w_in"], l, tm=1024, tn=1024)
    o_sb = _sb_attention(proj, batch, seq)
    head_params = jnp.zeros((SUBLANES, SMALL_W), F32)
    head_params = head_params.at[0, N_HEADS:2 * N_HEADS].set(p["gdn_a_log"])
    head_params = head_params.at[1, N_HEADS:2 * N_HEADS].set(p["gdn_dt_bias"])
    o_gd = _gdn(proj, small, p["gdn_conv_w"], head_params, p["gdn_out_norm"], batch, seq, ts=512)
    lam_vecs = jnp.stack([p["diff_lam_q1"], p["diff_lam_k1"], p["diff_lam_q2"], p["diff_lam_k2"]])
    o_df = _diff_attention(proj, lam_vecs, p["diff_out_norm"], lam_init, batch, seq)
    w2 = jnp.zeros((SMALL_W, N_HEADS * GLA_DK), F32).at[2 * N_HEADS:2 * N_HEADS + GLA_RANK].set(p["gla_gate_w2"])
    o_gl = _gla(proj, small, w2.astype(BF16), p["gla_gate_b"], p["gla_out_norm"], batch, seq, ts=512)
    return _out_proj(x, (o_sb, o_gd, o_df, o_gl), big["w_out"], l, tm=512)


def kernel(x, attn_norm, w_in, gdn_conv_w, gdn_a_log, gdn_dt_bias, gdn_out_norm, diff_lam_q1, diff_lam_k1,
           diff_lam_q2, diff_lam_k2, diff_out_norm, gla_gate_w2, gla_gate_b, gla_out_norm, w_out, ffn_norm,
           w_gate, w_up, w_down, final_norm):
    batch, seq, d = x.shape
    depth = w_in.shape[0]
    stacked = dict(attn_norm=attn_norm, gdn_conv_w=gdn_conv_w, gdn_a_log=gdn_a_log,
                   gdn_dt_bias=gdn_dt_bias, gdn_out_norm=gdn_out_norm, diff_lam_q1=diff_lam_q1,
                   diff_lam_k1=diff_lam_k1, diff_lam_q2=diff_lam_q2, diff_lam_k2=diff_lam_k2,
                   diff_out_norm=diff_out_norm, gla_gate_w2=gla_gate_w2, gla_gate_b=gla_gate_b,
                   gla_out_norm=gla_out_norm)
    big = dict(w_in=_pack_w_in(w_in), w_out=w_out.astype(BF16), w_gate=w_gate.astype(BF16),
               w_up=w_up.astype(BF16), w_down=w_down.astype(BF16))
    xt = x.reshape(batch * seq, d)
    for l in range(depth):
        p = {name: arr[l] for name, arr in stacked.items()}
        xt = _mixer_layer(xt, l, p, big, batch, seq)
        xt = _ffn(xt, ffn_norm[l], big["w_gate"], big["w_up"], big["w_down"], l, final_norm,
                  norm_output=(l == depth - 1), tm=1024, tf=512)
    return xt.reshape(batch, seq, d)
```

```python
import functools
import math

import jax
import jax.numpy as jnp
from jax import lax
from jax.experimental import pallas as pl
from jax.experimental.pallas import tpu as pltpu

F32 = jnp.float32
BF16 = jnp.bfloat16

NORM_EPS = 1e-6
LOG2_E = math.log2(math.e)
SB_DEAD_LOG2 = 110.0 * LOG2_E
N_HEADS = 4
HEAD_DIM = 128
GROUP = N_HEADS * HEAD_DIM
CHUNK = 64
CONV_TAPS = 4
GLA_DK = 64
GLA_RANK = 16
GLA_TAU = 16.0
CONV_HALO = 8
SUBLANES = 8

ATT_BLOCK = 256
DIFF_HEADS_PER_STEP = 4
SB_HEADS_PER_STEP = 4
SCAN_TILE = 512
GDN_CHUNKS_PER_STEP = 4
GDN_SCAN_CHUNKS_PER_STEP = 4
GLA_CHUNKS_PER_STEP = 4
GLA_SCAN_CHUNKS_PER_STEP = 4
IN_PROJ_TILE = (1024, 1024)
OUT_PROJ_ROWS = 512
FFN_TILE = (1024, 512)
VMEM_LIMIT = 56 * 1024 * 1024

COL_SB = 0
COL_GD = 3 * GROUP
COL_DF = 7 * GROUP
COL_GL = 10 * GROUP
COL_SMALL = 13 * GROUP
SMALL_W = 128
IN_COLS_PACKED = 14 * GROUP


def _cparams(*sem):
    return pltpu.CompilerParams(dimension_semantics=sem, vmem_limit_bytes=VMEM_LIMIT)


def _dot(a, b):
    return jnp.dot(a, b, preferred_element_type=F32)


def _dot_nt(a, b):
    return lax.dot_general(a, b, (((1,), (1,)), ((), ())), preferred_element_type=F32)


def _dot_tn(a, b):
    return lax.dot_general(a, b, (((0,), (0,)), ((), ())), preferred_element_type=F32)


def _split(x):
    hi = x.astype(BF16)
    return hi, (x - hi.astype(F32)).astype(BF16)


def _dot_sel(sel16, x):
    hi, lo = _split(x)
    return _dot(sel16, hi) + _dot(sel16, lo)


def _dot_unit_left(unit16, x):
    hi, lo = _split(x)
    return _dot(unit16, hi) + _dot(unit16, lo)


def _dot_unit_right(x, unit16):
    hi, lo = _split(x)
    return _dot(hi, unit16) + _dot(lo, unit16)


def _rms(x, gain):
    return x * lax.rsqrt(jnp.mean(x * x, axis=-1, keepdims=True) + NORM_EPS) * gain


def _silu(x):
    return x / (1.0 + jnp.exp(-x))


def _sigmoid(x):
    return 1.0 / (1.0 + jnp.exp(-x))


def _softplus(x):
    return jnp.maximum(x, 0.0) + jnp.log(1.0 + jnp.exp(-jnp.abs(x)))


def _norm_matmul_kernel(x_ref, g_ref, w_ref, o_ref, small_ref, h_ref):
    j = pl.program_id(1)
    last = pl.num_programs(1) - 1

    @pl.when(j == 0)
    def _():
        h_ref[...] = _rms(x_ref[...], g_ref[...]).astype(BF16)

    y = _dot(h_ref[...], w_ref[...])
    o_ref[...] = y.astype(o_ref.dtype)

    @pl.when(j == last)
    def _():
        small_ref[...] = y[:, y.shape[1] - small_ref.shape[1]:]


def _norm_in_proj(x, gain, w, layer, *, tm, tn):
    t, d = x.shape
    return pl.pallas_call(
        _norm_matmul_kernel,
        grid=(t // tm, IN_COLS_PACKED // tn),
        in_specs=[
            pl.BlockSpec((tm, d), lambda i, j: (i, 0)),
            pl.BlockSpec((1, d), lambda i, j: (0, 0)),
            pl.BlockSpec((None, d, tn), lambda i, j: (layer, 0, j)),
        ],
        out_specs=[
            pl.BlockSpec((tm, tn), lambda i, j: (i, j)),
            pl.BlockSpec((tm, IN_COLS_PACKED - COL_SMALL), lambda i, j: (i, 0)),
        ],
        out_shape=[jax.ShapeDtypeStruct((t, IN_COLS_PACKED), BF16),
                   jax.ShapeDtypeStruct((t, IN_COLS_PACKED - COL_SMALL), F32)],
        scratch_shapes=[pltpu.VMEM((tm, d), BF16)],
        compiler_params=_cparams("parallel", "arbitrary"),
        name="norm_in_proj",
    )(x, gain.reshape(1, d), w)


def _sb_kernel(q_ref, k_ref, v_ref, o_ref, *, blk, scale):
    i = pl.program_id(2)
    heads = [slice(h * HEAD_DIM, (h + 1) * HEAD_DIM) for h in range(SB_HEADS_PER_STEP)]
    row = lax.broadcasted_iota(jnp.int32, (blk, blk), 0)
    col = lax.broadcasted_iota(jnp.int32, (blk, blk), 1)
    suffix = jnp.where(row > col, 1.0, 0.0).astype(BF16)
    lead = jnp.maximum(i - 1, 0)
    col_minus_row = (lax.broadcasted_iota(jnp.int32, (blk, 2 * blk), 1)
                     - lax.broadcasted_iota(jnp.int32, (blk, 2 * blk), 0))
    causal = col_minus_row < (i - lead) * blk

    def span(first_block, nblk, state, masked):
        rows = pl.ds(pl.multiple_of(first_block * blk, blk), nblk * blk)
        parts = [slice(b * blk, (b + 1) * blk) for b in range(nblk)]
        zs = [_dot_nt(q_ref[:, hs], k_ref[rows, hs]) * (scale * LOG2_E) for hs in heads]
        sps = []
        for z in zs:
            sp = jnp.maximum(z, 0.0) + jnp.log2(1.0 + jnp.exp2(-jnp.abs(z)))
            sps.append(jnp.where(causal, sp, 0.0) if masked else sp)
        tails = [[_dot_sel_rhs(sp[:, s], suffix) for s in parts] for sp in sps]
        out = []
        for hs, z, sp, tail, (carry, acc) in zip(heads, zs, sps, tails, state):
            sums = [jnp.sum(sp[:, s], axis=1, keepdims=True) for s in parts]
            offsets = [None] * nblk
            for b in reversed(range(nblk)):
                offsets[b] = carry
                carry = carry + sums[b]
            att = jnp.concatenate(
                [jnp.exp2(z[:, s] - sp[:, s] - tail[b] - offsets[b])
                 for b, s in enumerate(parts)], axis=1)
            if masked:
                att = jnp.where(causal, att, 0.0)
            out.append((carry, acc + _dot(att.astype(BF16), v_ref[rows, hs])))
        return tuple(out)

    init = (jnp.zeros((blk, 1), F32), jnp.zeros((blk, HEAD_DIM), F32))
    state = span(lead, 2, (init,) * len(heads), True)

    def alive(state):
        low = state[0][0]
        for carry, _ in state[1:]:
            low = jnp.minimum(low, carry)
        return jnp.min(low) < SB_DEAD_LOG2

    _, state = lax.while_loop(
        lambda carried: jnp.logical_and(carried[0] < lead // 2, alive(carried[1])),
        lambda carried: (carried[0] + 1, span(lead - 2 - 2 * carried[0], 2, carried[1], False)),
        (jnp.int32(0), state))
    odd_left = jnp.where(alive(state), lead % 2, 0)
    state = lax.fori_loop(0, odd_left, lambda n, st: span(0, 1, st, False), state)
    for hs, (_, acc) in zip(heads, state):
        o_ref[:, hs] = acc.astype(o_ref.dtype)


def _dot_sel_rhs(x, sel16):
    hi, lo = _split(x)
    return _dot(hi, sel16) + _dot(lo, sel16)


def _sb_attention(proj, batch, seq):
    blk = ATT_BLOCK
    nq = seq // blk
    width = SB_HEADS_PER_STEP * HEAD_DIM
    qb, kb, vb = COL_SB // width, (COL_SB + GROUP) // width, (COL_SB + 2 * GROUP) // width
    kern = functools.partial(_sb_kernel, blk=blk, scale=HEAD_DIM ** -0.5)
    return pl.pallas_call(
        kern,
        grid=(batch, N_HEADS // SB_HEADS_PER_STEP, nq),
        in_specs=[
            pl.BlockSpec((blk, width), lambda b, h, i: (b * nq + i, qb + h)),
            pl.BlockSpec((seq, width), lambda b, h, i: (b, kb + h)),
            pl.BlockSpec((seq, width), lambda b, h, i: (b, vb + h)),
        ],
        out_specs=pl.BlockSpec((blk, width), lambda b, h, i: (b * nq + i, h)),
        out_shape=jax.ShapeDtypeStruct((batch * seq, GROUP), BF16),
        compiler_params=_cparams("parallel", "parallel", "arbitrary"),
        name="sb_attention",
    )(proj, proj, proj)


def _diff_kernel(lam_ref, q_ref, k_ref, v_ref, g_ref, o_ref, *, blk, scale, lam_init):
    i = pl.program_id(2)
    half = HEAD_DIM // 2
    lane = lax.broadcasted_iota(jnp.int32, (blk, HEAD_DIM), 1)
    heads = [slice(h * HEAD_DIM, (h + 1) * HEAD_DIM) for h in range(DIFF_HEADS_PER_STEP)]
    qs = []
    for hs in heads:
        q = q_ref[:, hs].astype(F32) * scale
        qs.append(jnp.concatenate([jnp.where(lane < half, q, 0.0), jnp.where(lane >= half, q, 0.0)],
                                  axis=0).astype(BF16))
    lead = jnp.maximum(i - 1, 0)
    row = lax.broadcasted_iota(jnp.int32, (2 * blk, 2 * blk), 0) & (blk - 1)
    col = lax.broadcasted_iota(jnp.int32, (2 * blk, 2 * blk), 1)
    causal = col - row <= (i - lead) * blk

    def softmax(s, state, masked):
        m, l_lanes, acc = state
        if masked:
            s = jnp.where(causal, s, -jnp.inf)
        m_new = jnp.maximum(m, jnp.max(s, axis=1, keepdims=True))
        alpha = jnp.exp(m - m_new)
        p = jnp.exp(s - m_new)
        p_lanes = p[:, 0:HEAD_DIM]
        for c in range(1, s.shape[1] // HEAD_DIM):
            p_lanes = p_lanes + p[:, c * HEAD_DIM:(c + 1) * HEAD_DIM]
        return m_new, alpha * l_lanes + p_lanes, alpha * acc, p.astype(BF16)

    def span(first_block, nblk, states, masked):
        rows = pl.ds(pl.multiple_of(first_block * blk, blk), nblk * blk)
        out = []
        s_next = _dot_nt(qs[0], k_ref[rows, heads[0]])
        for h, (hs, state) in enumerate(zip(heads, states)):
            s = s_next
            if h + 1 < len(heads):
                s_next = _dot_nt(qs[h + 1], k_ref[rows, heads[h + 1]])
            m, l_lanes, acc, p = softmax(s, state, masked)
            out.append((m, l_lanes, acc + _dot(p, v_ref[rows, hs])))
        return tuple(out)

    init = (jnp.full((2 * blk, 1), -jnp.inf, F32), jnp.zeros((2 * blk, HEAD_DIM), F32),
            jnp.zeros((2 * blk, HEAD_DIM), F32))
    states = span(lead, 2, (init,) * len(heads), True)
    states = lax.fori_loop(0, lead // 2, lambda n, st: span(2 * n, 2, st, False), states)
    states = lax.fori_loop(0, lead % 2, lambda n, st: span(lead - 1, 1, st, False), states)

    lam_v = lam_ref[...]
    d1 = jnp.sum(lam_v[0:1, :] * lam_v[1:2, :], axis=1, keepdims=True)
    d2 = jnp.sum(lam_v[2:3, :] * lam_v[3:4, :], axis=1, keepdims=True)
    lam = jnp.exp(d1) - jnp.exp(d2) + lam_init
    for hs, (_, l_lanes, acc) in zip(heads, states):
        norm = acc / jnp.sum(l_lanes, axis=1, keepdims=True)
        out = norm[:blk] - lam * norm[blk:]
        o_ref[:, hs] = (_rms(out, g_ref[...]) * (1.0 - lam_init)).astype(o_ref.dtype)


def _diff_attention(proj, lam_vecs, out_gain, lam_init, batch, seq):
    blk = ATT_BLOCK
    nq = seq // blk
    width = DIFF_HEADS_PER_STEP * HEAD_DIM
    qb, kb, vb = COL_DF // width, (COL_DF + GROUP) // width, (COL_DF + 2 * GROUP) // width
    kern = functools.partial(_diff_kernel, blk=blk, scale=(HEAD_DIM // 2) ** -0.5, lam_init=lam_init)
    return pl.pallas_call(
        kern,
        grid=(batch, N_HEADS // DIFF_HEADS_PER_STEP, nq),
        in_specs=[
            pl.BlockSpec(lam_vecs.shape, lambda b, h, i: (0, 0)),
            pl.BlockSpec((blk, width), lambda b, h, i: (b * nq + i, qb + h)),
            pl.BlockSpec((seq, width), lambda b, h, i: (b, kb + h)),
            pl.BlockSpec((seq, width), lambda b, h, i: (b, vb + h)),
            pl.BlockSpec((1, HEAD_DIM), lambda b, h, i: (0, 0)),
        ],
        out_specs=pl.BlockSpec((blk, width), lambda b, h, i: (b * nq + i, h)),
        out_shape=jax.ShapeDtypeStruct((batch * seq, GROUP), BF16),
        compiler_params=_cparams("parallel", "parallel", "arbitrary"),
        name="diff_attention",
    )(lam_vecs, proj, proj, proj, out_gain.reshape(1, HEAD_DIM))


def _chunk_iotas(width=CHUNK):
    row = lax.broadcasted_iota(jnp.int32, (CHUNK, width), 0)
    col = lax.broadcasted_iota(jnp.int32, (CHUNK, width), 1)
    return row, col


def _lower_left_blocks(row, col, level):
    same_pair = (row >> (level + 1)) == (col >> (level + 1))
    return same_pair & (((row >> level) & 1) == 1) & (((col >> level) & 1) == 0)


def _unit_lower_inverses(a_list, row, col):
    eye = jnp.where(row == col, 1.0, 0.0).astype(F32)
    invs = [eye - jnp.where(_lower_left_blocks(row, col, 0), a, 0.0) for a in a_list]
    for level in range(1, CHUNK.bit_length() - 1):
        mask = _lower_left_blocks(row, col, level)
        inv16 = [inv.astype(BF16) for inv in invs]
        left = [_dot_unit_left(i16, jnp.where(mask, a, 0.0)) for i16, a in zip(inv16, a_list)]
        invs = [inv - _dot_unit_right(x, i16) for inv, x, i16 in zip(invs, left, inv16)]
    return invs


def _gdn_kernel(x_ref, z_ref, sm_ref, cw_ref, hp_ref, gain_ref, o_ref,
                xs_ref, qn_ref, kn_ref, vn_ref, u_ref, w_ref, qg_ref, kd_ref, qk_ref, el_ref, st_ref, *, ts):
    t = pl.program_id(1)

    @pl.when(t == 0)
    def _():
        xs_ref[0:CONV_HALO, :] = jnp.zeros((CONV_HALO, 3 * GROUP), F32)
        st_ref[...] = jnp.zeros_like(st_ref)

    xs_ref[CONV_HALO:CONV_HALO + ts, :] = x_ref[...].astype(F32)

    for part, dst in enumerate((qn_ref, kn_ref, vn_ref)):
        for h in range(N_HEADS):
            c0 = part * GROUP + h * HEAD_DIM
            acc = jnp.zeros((ts, HEAD_DIM), F32)
            for j in range(CONV_TAPS):
                off = CONV_HALO - (CONV_TAPS - 1) + j
                acc = acc + xs_ref[off:off + ts, c0:c0 + HEAD_DIM] * cw_ref[j:j + 1, c0:c0 + HEAD_DIM]
            y = _silu(acc)
            if part < 2:
                y = y * lax.rsqrt(jnp.sum(y * y, axis=-1, keepdims=True) + NORM_EPS)
            if part == 0:
                y = y * (HEAD_DIM ** -0.5)
            dst[:, h * HEAD_DIM:(h + 1) * HEAD_DIM] = y

    xs_ref[0:CONV_HALO, :] = xs_ref[ts:ts + CONV_HALO, :]

    row, col = _chunk_iotas()
    strict = row > col
    row_w, col_w = _chunk_iotas(HEAD_DIM)
    incl_w = row_w >= col_w
    eye_w = row_w == col_w
    lower_ones = jnp.where(row >= col, 1.0, 0.0).astype(BF16)
    all_ones = jnp.ones((CHUNK, CHUNK), BF16)

    def prepare(pair, _):
        gates = []
        for sub in range(GDN_CHUNKS_PER_STEP):
            c = pair * GDN_CHUNKS_PER_STEP + sub
            r0 = pl.multiple_of(c * CHUNK, CHUNK)
            sm = sm_ref[pl.ds(r0, CHUNK), :]
            beta_all = _sigmoid(sm)
            g_all = -jnp.exp(hp_ref[0:1, :]) * _softplus(sm + hp_ref[1:2, :])
            g_b = jnp.concatenate(
                [jnp.broadcast_to(g_all[:, N_HEADS + h:N_HEADS + h + 1], (CHUNK, HEAD_DIM))
                 for h in range(N_HEADS)], axis=1)
            gc_all = _dot_sel(lower_ones, g_b)
            gates.append((c, r0, beta_all, gc_all))
        gc_rows = []
        for _, _, _, gc_all in gates:
            gc_diag = jnp.concatenate(
                [jnp.where(eye_w, gc_all[:, h * HEAD_DIM:(h + 1) * HEAD_DIM], 0.0) for h in range(N_HEADS)], axis=1)
            gc_rows.append(_dot_sel(all_ones, gc_diag))

        probs = []
        for (c, r0, beta_all, gc_all), gc_row_all in zip(gates, gc_rows):
            for h in range(N_HEADS):
                hs = slice(h * HEAD_DIM, (h + 1) * HEAD_DIM)
                beta = beta_all[:, h:h + 1]
                gc = gc_all[:, hs]
                decay = jnp.exp(jnp.where(incl_w, gc - gc_row_all[:, hs], -jnp.inf))[:, :CHUNK]
                q = qn_ref[pl.ds(r0, CHUNK), hs]
                k = kn_ref[pl.ds(r0, CHUNK), hs]
                v = vn_ref[pl.ds(r0, CHUNK), hs]
                kb = k * beta
                k16 = k.astype(BF16)
                a_mat = jnp.where(strict, _dot_nt(kb.astype(BF16), k16) * decay, 0.0)
                qk = (_dot_nt(q.astype(BF16), k16) * decay).astype(BF16)
                probs.append(dict(c=c, r0=r0, h=h, hs=hs, gc=gc, q=q, k=k, kb=kb, vb=v * beta, a=a_mat, qk=qk))

        invs = _unit_lower_inverses([p["a"] for p in probs], row, col)
        uws = [_dot_unit_left(inv.astype(BF16), jnp.concatenate([p["vb"], p["kb"] * jnp.exp(p["gc"])], axis=1))
               for inv, p in zip(invs, probs)]
        for p, uw in zip(probs, uws):
            r0, hs, gc, h = p["r0"], p["hs"], p["gc"], p["h"]
            g_last = gc[CHUNK - 1:CHUNK, :]
            u_ref[pl.ds(r0, CHUNK), hs] = uw[:, :HEAD_DIM]
            w_ref[pl.ds(r0, CHUNK), hs] = uw[:, HEAD_DIM:].astype(BF16)
            qg_ref[pl.ds(r0, CHUNK), hs] = (p["q"] * jnp.exp(gc)).astype(BF16)
            kd_ref[pl.ds(r0, CHUNK), hs] = (p["k"] * jnp.exp(g_last - gc)).astype(BF16)
            qk_ref[pl.ds(r0, CHUNK), h * HEAD_DIM:h * HEAD_DIM + CHUNK] = p["qk"]
            el_ref[pl.ds(pl.multiple_of(p["c"] * SUBLANES, SUBLANES), SUBLANES), hs] = jnp.broadcast_to(
                jnp.exp(g_last), (SUBLANES, HEAD_DIM))
        return 0

    lax.fori_loop(0, ts // (CHUNK * GDN_CHUNKS_PER_STEP), prepare, 0)

    def scan(step, _):
        heads = [slice(h * HEAD_DIM, (h + 1) * HEAD_DIM) for h in range(N_HEADS)]
        states = [st_ref[h] for h in range(N_HEADS)]
        for sub in range(GDN_SCAN_CHUNKS_PER_STEP):
            c = step * GDN_SCAN_CHUNKS_PER_STEP + sub
            r0 = pl.multiple_of(c * CHUNK, CHUNK)
            s16 = [s.astype(BF16) for s in states]
            ws = [_dot(w_ref[pl.ds(r0, CHUNK), hs], s) for hs, s in zip(heads, s16)]
            inter = [_dot(qg_ref[pl.ds(r0, CHUNK), hs], s) for hs, s in zip(heads, s16)]
            v_new = [(u_ref[pl.ds(r0, CHUNK), hs] - x).astype(BF16) for hs, x in zip(heads, ws)]
            upd = [_dot_tn(kd_ref[pl.ds(r0, CHUNK), hs], vn) for hs, vn in zip(heads, v_new)]
            for h, hs in enumerate(heads):
                e_last = el_ref[pl.ds(pl.multiple_of(c * SUBLANES, SUBLANES), SUBLANES), hs][0:1, :]
                states[h] = states[h] * e_last + upd[h]
            intra = [_dot(qk_ref[pl.ds(r0, CHUNK), h * HEAD_DIM:h * HEAD_DIM + CHUNK], vn)
                     for h, vn in enumerate(v_new)]
            for h, hs in enumerate(heads):
                zg = z_ref[pl.ds(r0, CHUNK), hs].astype(F32)
                o_ref[pl.ds(r0, CHUNK), hs] = (
                    _rms(inter[h] + intra[h], gain_ref[...]) * _silu(zg)).astype(o_ref.dtype)
        for h in range(N_HEADS):
            st_ref[h] = states[h]
        return 0

    lax.fori_loop(0, ts // (CHUNK * GDN_SCAN_CHUNKS_PER_STEP), scan, 0)


def _gdn(proj, small, conv_w, head_params, out_gain, batch, seq, *, ts):
    nt = seq // ts
    kern = functools.partial(_gdn_kernel, ts=ts)
    qkv_blk = COL_GD // (3 * GROUP)
    z_blk = (COL_GD + 3 * GROUP) // GROUP
    return pl.pallas_call(
        kern,
        grid=(batch, nt),
        in_specs=[
            pl.BlockSpec((ts, 3 * GROUP), lambda b, t: (b * nt + t, qkv_blk)),
            pl.BlockSpec((ts, GROUP), lambda b, t: (b * nt + t, z_blk)),
            pl.BlockSpec((ts, SMALL_W), lambda b, t: (b * nt + t, 0)),
            pl.BlockSpec((CONV_TAPS, 3 * GROUP), lambda b, t: (0, 0)),
            pl.BlockSpec(head_params.shape, lambda b, t: (0, 0)),
            pl.BlockSpec((1, HEAD_DIM), lambda b, t: (0, 0)),
        ],
        out_specs=pl.BlockSpec((ts, GROUP), lambda b, t: (b * nt + t, 0)),
        out_shape=jax.ShapeDtypeStruct((batch * seq, GROUP), BF16),
        scratch_shapes=[
            pltpu.VMEM((ts + CONV_HALO, 3 * GROUP), F32),
            pltpu.VMEM((ts, GROUP), F32),
            pltpu.VMEM((ts, GROUP), F32),
            pltpu.VMEM((ts, GROUP), F32),
            pltpu.VMEM((ts, GROUP), F32),
            pltpu.VMEM((ts, GROUP), BF16),
            pltpu.VMEM((ts, GROUP), BF16),
            pltpu.VMEM((ts, GROUP), BF16),
            pltpu.VMEM((ts, GROUP), BF16),
            pltpu.VMEM((ts // CHUNK * SUBLANES, GROUP), F32),
            pltpu.VMEM((N_HEADS, HEAD_DIM, HEAD_DIM), F32),
        ],
        compiler_params=_cparams("parallel", "arbitrary"),
        name="gated_deltanet",
    )(proj, proj, small, conv_w, head_params, out_gain.reshape(1, HEAD_DIM))


def _gla_kernel(qk_ref, v_ref, og_ref, sm_ref, w2_ref, gb_ref, gain_ref, o_ref,
                att_ref, qin_ref, kout_ref, el_ref, st_ref, *, ts):
    t = pl.program_id(1)

    @pl.when(t == 0)
    def _():
        st_ref[...] = jnp.zeros_like(st_ref)

    row, col = _chunk_iotas()
    lower_ones = jnp.where(row >= col, 1.0, 0.0).astype(BF16)
    eye = row == col
    row_w, lane = _chunk_iotas(HEAD_DIM)
    qk_scale = GLA_DK ** -0.5

    levels = []
    for level in range(CHUNK.bit_length() - 1):
        ref_row = ((row >> (level + 1)) << (level + 1)) + (1 << level)
        pick = jnp.where(col == ref_row, 1.0, 0.0).astype(BF16)
        upper_w = ((row_w >> level) & 1) == 1
        levels.append((pick, _lower_left_blocks(row, col, level), upper_w))

    n_pairs = N_HEADS * GLA_DK // HEAD_DIM
    heads_per_pair = HEAD_DIM // GLA_DK

    def prepare(step, _):
        cums = []
        for sub in range(GLA_CHUNKS_PER_STEP):
            c = step * GLA_CHUNKS_PER_STEP + sub
            r0 = pl.multiple_of(c * CHUNK, CHUNK)
            sm = sm_ref[pl.ds(r0, CHUNK), :].astype(BF16)
            gate = _dot(sm, w2_ref[...]) + gb_ref[...]
            log_a = (gate - _softplus(gate)) * (1.0 / GLA_TAU)
            cums.append((c, r0, _dot_sel(lower_ones, log_a)))
        probs = []
        for c, r0, b_all in cums:
            for pair in range(n_pairs):
                ps = slice(pair * HEAD_DIM, (pair + 1) * HEAD_DIM)
                b = b_all[:, ps]
                q2 = qk_ref[pl.ds(r0, CHUNK), ps].astype(F32) * qk_scale
                k2 = qk_ref[pl.ds(r0, CHUNK),
                            GROUP // 2 + pair * HEAD_DIM:GROUP // 2 + (pair + 1) * HEAD_DIM].astype(F32)
                probs.append(dict(c=c, r0=r0, pair=pair, ps=ps, b=b, q2=q2, k2=k2,
                                  b_refs=[_dot_sel(pick, b) for pick, _, _ in levels]))
        for p in probs:
            b, q2, k2 = p["b"], p["q2"], p["k2"]
            p["q_lv"] = [jnp.where(upper_w, q2 * jnp.exp(jnp.where(upper_w, b - b_ref, 0.0)), 0.0)
                         for (_, _, upper_w), b_ref in zip(levels, p["b_refs"])]
            p["k_lv"] = [jnp.where(upper_w, 0.0, k2 * jnp.exp(jnp.where(upper_w, 0.0, b_ref - b))).astype(BF16)
                         for (_, _, upper_w), b_ref in zip(levels, p["b_refs"])]
        for p in probs:
            r0, b, q2, k2 = p["r0"], p["b"], p["q2"], p["k2"]
            b_last = b[CHUNK - 1:CHUNK, :]
            q_in = (q2 * jnp.exp(b)).astype(BF16)
            k_out = (k2 * jnp.exp(b_last - b)).astype(BF16)
            diag_qk = q2 * k2
            for r in range(heads_per_pair):
                h = p["pair"] * heads_per_pair + r
                hs = slice(h * HEAD_DIM, (h + 1) * HEAD_DIM)
                mine = (lane >= r * GLA_DK) & (lane < (r + 1) * GLA_DK)
                att = jnp.where(eye, jnp.sum(jnp.where(mine, diag_qk, 0.0), axis=1, keepdims=True), 0.0)
                for (_, pair_mask, _), ql, kl in zip(levels, p["q_lv"], p["k_lv"]):
                    s = _dot_nt(jnp.where(mine, ql, 0.0).astype(BF16), kl)
                    att = att + jnp.where(pair_mask, s, 0.0)
                att_ref[pl.ds(r0, CHUNK), h * HEAD_DIM:h * HEAD_DIM + CHUNK] = att.astype(BF16)
                qin_ref[pl.ds(r0, CHUNK), hs] = jnp.where(mine, q_in, jnp.zeros_like(q_in))
                kout_ref[pl.ds(r0, CHUNK), hs] = jnp.where(mine, k_out, jnp.zeros_like(k_out))
            el_ref[pl.ds(pl.multiple_of(p["c"] * SUBLANES, SUBLANES), SUBLANES), p["ps"]] = jnp.broadcast_to(
                jnp.exp(b_last), (SUBLANES, HEAD_DIM))
        return 0

    lax.fori_loop(0, ts // (CHUNK * GLA_CHUNKS_PER_STEP), prepare, 0)

    def scan(step, _):
        heads = [slice(h * HEAD_DIM, (h + 1) * HEAD_DIM) for h in range(N_HEADS)]
        chunks = []
        for sub in range(GLA_SCAN_CHUNKS_PER_STEP):
            c = step * GLA_SCAN_CHUNKS_PER_STEP + sub
            r0 = pl.multiple_of(c * CHUNK, CHUNK)
            values = [v_ref[pl.ds(r0, CHUNK), hs] for hs in heads]
            intra = [_dot(att_ref[pl.ds(r0, CHUNK), h * HEAD_DIM:h * HEAD_DIM + CHUNK], v)
                     for h, v in enumerate(values)]
            upd = [_dot_tn(v, kout_ref[pl.ds(r0, CHUNK), hs]) for hs, v in zip(heads, values)]
            chunks.append((c, r0, intra, upd))
        states = [st_ref[h] for h in range(N_HEADS)]
        for c, r0, intra, upd in chunks:
            inter = [_dot_nt(qin_ref[pl.ds(r0, CHUNK), hs], s.astype(BF16)) for hs, s in zip(heads, states)]
            for h, hs in enumerate(heads):
                ps = slice((h // heads_per_pair) * HEAD_DIM, (h // heads_per_pair + 1) * HEAD_DIM)
                e_last = el_ref[pl.ds(pl.multiple_of(c * SUBLANES, SUBLANES), SUBLANES), ps][0:1, :]
                states[h] = states[h] * e_last + upd[h]
                og = og_ref[pl.ds(r0, CHUNK), hs].astype(F32)
                o_ref[pl.ds(r0, CHUNK), hs] = (
                    _rms(inter[h] + intra[h], gain_ref[...]) * _silu(og)).astype(o_ref.dtype)
        for h in range(N_HEADS):
            st_ref[h] = states[h]
        return 0

    lax.fori_loop(0, ts // (CHUNK * GLA_SCAN_CHUNKS_PER_STEP), scan, 0)


def _gla(proj, small, w2_padded, gate_bias, out_gain, batch, seq, *, ts):
    nt = seq // ts
    kern = functools.partial(_gla_kernel, ts=ts)
    qk_blk = COL_GL // GROUP
    return pl.pallas_call(
        kern,
        grid=(batch, nt),
        in_specs=[
            pl.BlockSpec((ts, GROUP), lambda b, t: (b * nt + t, qk_blk)),
            pl.BlockSpec((ts, GROUP), lambda b, t: (b * nt + t, qk_blk + 1)),
            pl.BlockSpec((ts, GROUP), lambda b, t: (b * nt + t, qk_blk + 2)),
            pl.BlockSpec((ts, SMALL_W), lambda b, t: (b * nt + t, 0)),
            pl.BlockSpec(w2_padded.shape, lambda b, t: (0, 0)),
            pl.BlockSpec((1, N_HEADS * GLA_DK), lambda b, t: (0, 0)),
            pl.BlockSpec((1, HEAD_DIM), lambda b, t: (0, 0)),
        ],
        out_specs=pl.BlockSpec((ts, GROUP), lambda b, t: (b * nt + t, 0)),
        out_shape=jax.ShapeDtypeStruct((batch * seq, GROUP), BF16),
        scratch_shapes=[
            pltpu.VMEM((ts, GROUP), BF16),
            pltpu.VMEM((ts, GROUP), BF16),
            pltpu.VMEM((ts, GROUP), BF16),
            pltpu.VMEM((ts // CHUNK * SUBLANES, N_HEADS * GLA_DK), F32),
            pltpu.VMEM((N_HEADS, HEAD_DIM, HEAD_DIM), F32),
        ],
        compiler_params=_cparams("parallel", "arbitrary"),
        name="gla",
    )(proj, proj, proj, small, w2_padded, gate_bias.reshape(1, -1), out_gain.reshape(1, HEAD_DIM))


def _out_proj_kernel(x_ref, m0_ref, m1_ref, m2_ref, m3_ref, w_ref, o_ref):
    acc = x_ref[...]
    for g, m_ref in enumerate((m0_ref, m1_ref, m2_ref, m3_ref)):
        acc = acc + _dot(m_ref[...], w_ref[g * GROUP:(g + 1) * GROUP, :])
    o_ref[...] = acc


def _out_proj(x, mixes, w_out, layer, *, tm):
    t, d = x.shape
    mix_spec = pl.BlockSpec((tm, GROUP), lambda i: (i, 0))
    return pl.pallas_call(
        _out_proj_kernel,
        grid=(t // tm,),
        in_specs=[pl.BlockSpec((tm, d), lambda i: (i, 0)), mix_spec, mix_spec, mix_spec, mix_spec,
                  pl.BlockSpec((None,) + w_out.shape[1:], lambda i: (layer, 0, 0))],
        out_specs=pl.BlockSpec((tm, d), lambda i: (i, 0)),
        out_shape=jax.ShapeDtypeStruct((t, d), F32),
        compiler_params=_cparams("parallel"),
        name="out_proj_residual",
    )(x, *mixes, w_out)


def _ffn_kernel(x_ref, g_ref, wg_ref, wu_ref, wd_ref, og_ref, o_ref, h_ref, *, norm_output):
    f = pl.program_id(1)

    @pl.when(f == 0)
    def _():
        x = x_ref[...]
        h_ref[...] = _rms(x, g_ref[...]).astype(BF16)
        o_ref[...] = x

    h = h_ref[...]
    hid = _silu(_dot(h, wg_ref[...])) * _dot(h, wu_ref[...])
    o_ref[...] += _dot(hid.astype(BF16), wd_ref[...])

    if norm_output:
        @pl.when(f == pl.num_programs(1) - 1)
        def _():
            o_ref[...] = _rms(o_ref[...], og_ref[...])


def _ffn(x, gain, w_gate, w_up, w_down, layer, out_gain, *, norm_output, tm, tf):
    t, d = x.shape
    hidden = w_gate.shape[2]
    return pl.pallas_call(
        functools.partial(_ffn_kernel, norm_output=norm_output),
        grid=(t // tm, hidden // tf),
        in_specs=[
            pl.BlockSpec((tm, d), lambda i, f: (i, 0)),
            pl.BlockSpec((1, d), lambda i, f: (0, 0)),
            pl.BlockSpec((None, d, tf), lambda i, f: (layer, 0, f)),
            pl.BlockSpec((None, d, tf), lambda i, f: (layer, 0, f)),
            pl.BlockSpec((None, tf, d), lambda i, f: (layer, f, 0)),
            pl.BlockSpec((1, d), lambda i, f: (0, 0)),
        ],
        out_specs=pl.BlockSpec((tm, d), lambda i, f: (i, 0)),
        out_shape=jax.ShapeDtypeStruct((t, d), F32),
        scratch_shapes=[pltpu.VMEM((tm, d), BF16)],
        compiler_params=_cparams("parallel", "arbitrary"),
        name="ffn_residual",
    )(x, gain.reshape(1, d), w_gate, w_up, w_down, out_gain.reshape(1, d))


def _pack_w_in(w_in):
    gd_small = COL_GD + 4 * GROUP
    df0 = gd_small + 2 * N_HEADS
    gl0 = df0 + 3 * GROUP
    rank0 = gl0 + 3 * GROUP
    pad = jnp.zeros(w_in.shape[:-1] + (IN_COLS_PACKED - COL_SMALL - 2 * N_HEADS - GLA_RANK,), BF16)
    parts = [w_in[..., :gd_small], w_in[..., df0:rank0], w_in[..., gd_small:df0], w_in[..., rank0:rank0 + GLA_RANK]]
    return jnp.concatenate([part.astype(BF16) for part in parts] + [pad], axis=-1)


def _mixer_layer(x, l, p, big, batch, seq):
    lam_init = 0.8 - 0.6 * math.exp(-0.3 * l)
    proj, small = _norm_in_proj(x, p["attn_norm"], big["w_in"], l, tm=IN_PROJ_TILE[0], tn=IN_PROJ_TILE[1])
    o_sb = _sb_attention(proj, batch, seq)
    head_params = jnp.zeros((SUBLANES, SMALL_W), F32)
    head_params = head_params.at[0, N_HEADS:2 * N_HEADS].set(p["gdn_a_log"])
    head_params = head_params.at[1, N_HEADS:2 * N_HEADS].set(p["gdn_dt_bias"])
    o_gd = _gdn(proj, small, p["gdn_conv_w"], head_params, p["gdn_out_norm"], batch, seq, ts=SCAN_TILE)
    lam_vecs = jnp.stack([p["diff_lam_q1"], p["diff_lam_k1"], p["diff_lam_q2"], p["diff_lam_k2"]])
    o_df = _diff_attention(proj, lam_vecs, p["diff_out_norm"], lam_init, batch, seq)
    w2 = jnp.zeros((SMALL_W, N_HEADS * GLA_DK), F32).at[2 * N_HEADS:2 * N_HEADS + GLA_RANK].set(p["gla_gate_w2"])
    o_gl = _gla(proj, small, w2.astype(BF16), p["gla_gate_b"], p["gla_out_norm"], batch, seq, ts=SCAN_TILE)
    return _out_proj(x, (o_sb, o_gd, o_df, o_gl), big["w_out"], l, tm=OUT_PROJ_ROWS)


def kernel(x, attn_norm, w_in, gdn_conv_w, gdn_a_log, gdn_dt_bias, gdn_out_norm, diff_lam_q1, diff_lam_k1,
           diff_lam_q2, diff_lam_k2, diff_out_norm, gla_gate_w2, gla_gate_b, gla_out_norm, w_out, ffn_norm,
           w_gate, w_up, w_down, final_norm):
    batch, seq, d = x.shape
    depth = w_in.shape[0]
    tokens = batch * seq
    assert seq % (2 * ATT_BLOCK) == 0 and seq % SCAN_TILE == 0, (seq, ATT_BLOCK, SCAN_TILE)
    assert tokens % IN_PROJ_TILE[0] == 0 and tokens % FFN_TILE[0] == 0 and tokens % OUT_PROJ_ROWS == 0, tokens
    assert w_gate.shape[2] % FFN_TILE[1] == 0, w_gate.shape
    stacked = dict(attn_norm=attn_norm, gdn_conv_w=gdn_conv_w, gdn_a_log=gdn_a_log,
                   gdn_dt_bias=gdn_dt_bias, gdn_out_norm=gdn_out_norm, diff_lam_q1=diff_lam_q1,
                   diff_lam_k1=diff_lam_k1, diff_lam_q2=diff_lam_q2, diff_lam_k2=diff_lam_k2,
                   diff_out_norm=diff_out_norm, gla_gate_w2=gla_gate_w2, gla_gate_b=gla_gate_b,
                   gla_out_norm=gla_out_norm)
    big = dict(w_in=_pack_w_in(w_in), w_out=w_out.astype(BF16), w_gate=w_gate.astype(BF16),
               w_up=w_up.astype(BF16), w_down=w_down.astype(BF16))
    xt = x.reshape(tokens, d)
    for l in range(depth):
        p = {name: arr[l] for name, arr in stacked.items()}
        xt = _mixer_layer(xt, l, p, big, batch, seq)
        xt = _ffn(xt, ffn_norm[l], big["w_gate"], big["w_up"], big["w_down"], l, final_norm,
                  norm_output=(l == depth - 1), tm=FFN_TILE[0], tf=FFN_TILE[1])
    return xt.reshape(batch, seq, d)
```

```python
import functools
import math

import jax
import jax.numpy as jnp
from jax import lax
from jax.experimental import pallas as pl
from jax.experimental.pallas import tpu as pltpu

F32 = jnp.float32
BF16 = jnp.bfloat16

NORM_EPS = 1e-6
LOG2_E = math.log2(math.e)
SB_DEAD_LOG2 = 110.0 * LOG2_E
N_HEADS = 4
HEAD_DIM = 128
GROUP = N_HEADS * HEAD_DIM
CHUNK = 64
CONV_TAPS = 4
GLA_DK = 64
GLA_RANK = 16
GLA_TAU = 16.0
CONV_HALO = 8
SUBLANES = 8

ATT_BLOCK = 256
DIFF_HEADS_PER_STEP = 4
SB_HEADS_PER_STEP = 4
SCAN_TILE = 512
GDN_CHUNKS_PER_STEP = 4
GDN_SCAN_CHUNKS_PER_STEP = 4
GLA_CHUNKS_PER_STEP = 4
GLA_SCAN_CHUNKS_PER_STEP = 4
IN_PROJ_TILE = (1024, 1024)
OUT_PROJ_ROWS = 512
FFN_TILE = (1024, 512)
VMEM_LIMIT = 56 * 1024 * 1024

COL_SB = 0
COL_GD = 3 * GROUP
COL_DF = 7 * GROUP
COL_GL = 10 * GROUP
COL_SMALL = 13 * GROUP
SMALL_W = 128
IN_COLS_PACKED = 14 * GROUP


def _cparams(*sem):
    return pltpu.CompilerParams(dimension_semantics=sem, vmem_limit_bytes=VMEM_LIMIT)


def _dot(a, b):
    return jnp.dot(a, b, preferred_element_type=F32)


def _dot_nt(a, b):
    return lax.dot_general(a, b, (((1,), (1,)), ((), ())), preferred_element_type=F32)


def _dot_tn(a, b):
    return lax.dot_general(a, b, (((0,), (0,)), ((), ())), preferred_element_type=F32)


def _split(x):
    hi = x.astype(BF16)
    return hi, (x - hi.astype(F32)).astype(BF16)


def _dot_sel(sel16, x):
    hi, lo = _split(x)
    return _dot(sel16, hi) + _dot(sel16, lo)


def _dot_unit_left(unit16, x):
    hi, lo = _split(x)
    return _dot(unit16, hi) + _dot(unit16, lo)


def _dot_unit_right(x, unit16):
    hi, lo = _split(x)
    return _dot(hi, unit16) + _dot(lo, unit16)


def _rms(x, gain):
    return x * lax.rsqrt(jnp.mean(x * x, axis=-1, keepdims=True) + NORM_EPS) * gain


def _silu(x):
    return x / (1.0 + jnp.exp(-x))


def _sigmoid(x):
    return 1.0 / (1.0 + jnp.exp(-x))


def _softplus(x):
    return jnp.maximum(x, 0.0) + jnp.log(1.0 + jnp.exp(-jnp.abs(x)))


def _norm_matmul_kernel(x_ref, g_ref, w_ref, o_ref, small_ref, h_ref):
    j = pl.program_id(1)
    last = pl.num_programs(1) - 1

    @pl.when(j == 0)
    def _():
        h = _rms(x_ref[...], g_ref[...]).astype(BF16)
        h_ref[...] = h
        o_ref[...] = _dot(h, w_ref[...]).astype(o_ref.dtype)

    @pl.when(j > 0)
    def _():
        y = _dot(h_ref[...], w_ref[...])
        o_ref[...] = y.astype(o_ref.dtype)

        @pl.when(j == last)
        def _():
            small_ref[...] = y[:, y.shape[1] - small_ref.shape[1]:]


def _norm_in_proj(x, gain, w, layer, *, tm, tn):
    t, d = x.shape
    return pl.pallas_call(
        _norm_matmul_kernel,
        grid=(t // tm, IN_COLS_PACKED // tn),
        in_specs=[
            pl.BlockSpec((tm, d), lambda i, j: (i, 0)),
            pl.BlockSpec((1, d), lambda i, j: (0, 0)),
            pl.BlockSpec((None, d, tn), lambda i, j: (layer, 0, j)),
        ],
        out_specs=[
            pl.BlockSpec((tm, tn), lambda i, j: (i, j)),
            pl.BlockSpec((tm, IN_COLS_PACKED - COL_SMALL), lambda i, j: (i, 0)),
        ],
        out_shape=[jax.ShapeDtypeStruct((t, IN_COLS_PACKED), BF16),
                   jax.ShapeDtypeStruct((t, IN_COLS_PACKED - COL_SMALL), F32)],
        scratch_shapes=[pltpu.VMEM((tm, d), BF16)],
        compiler_params=_cparams("parallel", "arbitrary"),
        name="norm_in_proj",
    )(x, gain.reshape(1, d), w)


def _sb_kernel(q_ref, k_ref, v_ref, o_ref, *, blk, scale):
    i = pl.program_id(2)
    heads = [slice(h * HEAD_DIM, (h + 1) * HEAD_DIM) for h in range(SB_HEADS_PER_STEP)]
    row = lax.broadcasted_iota(jnp.int32, (blk, blk), 0)
    col = lax.broadcasted_iota(jnp.int32, (blk, blk), 1)
    suffix = jnp.where(row > col, 1.0, 0.0).astype(BF16)
    lead = jnp.maximum(i - 1, 0)
    col_minus_row = (lax.broadcasted_iota(jnp.int32, (blk, 2 * blk), 1)
                     - lax.broadcasted_iota(jnp.int32, (blk, 2 * blk), 0))
    causal = col_minus_row < (i - lead) * blk

    def span(first_block, nblk, state, masked):
        rows = pl.ds(pl.multiple_of(first_block * blk, blk), nblk * blk)
        parts = [slice(b * blk, (b + 1) * blk) for b in range(nblk)]
        zs = [_dot_nt(q_ref[:, hs], k_ref[rows, hs]) * (scale * LOG2_E) for hs in heads]
        sps = []
        for z in zs:
            sp = jnp.maximum(z, 0.0) + jnp.log2(1.0 + jnp.exp2(-jnp.abs(z)))
            sps.append(jnp.where(causal, sp, 0.0) if masked else sp)
        tails = [[_dot_sel_rhs(sp[:, s], suffix) for s in parts] for sp in sps]
        out = []
        for hs, z, sp, tail, (carry, acc) in zip(heads, zs, sps, tails, state):
            sums = [jnp.sum(sp[:, s], axis=1, keepdims=True) for s in parts]
            offsets = [None] * nblk
            for b in reversed(range(nblk)):
                offsets[b] = carry
                carry = carry + sums[b]
            att = jnp.concatenate(
                [jnp.exp2(z[:, s] - sp[:, s] - tail[b] - offsets[b])
                 for b, s in enumerate(parts)], axis=1)
            if masked:
                att = jnp.where(causal, att, 0.0)
            out.append((carry, acc + _dot(att.astype(BF16), v_ref[rows, hs])))
        return tuple(out)

    init = (jnp.zeros((blk, 1), F32), jnp.zeros((blk, HEAD_DIM), F32))
    state = span(lead, 2, (init,) * len(heads), True)

    def alive(state):
        low = state[0][0]
        for carry, _ in state[1:]:
            low = jnp.minimum(low, carry)
        return jnp.min(low) < SB_DEAD_LOG2

    _, state = lax.while_loop(
        lambda carried: jnp.logical_and(carried[0] < lead // 2, alive(carried[1])),
        lambda carried: (carried[0] + 1, span(lead - 2 - 2 * carried[0], 2, carried[1], False)),
        (jnp.int32(0), state))
    odd_left = jnp.where(alive(state), lead % 2, 0)
    state = lax.fori_loop(0, odd_left, lambda n, st: span(0, 1, st, False), state)
    for hs, (_, acc) in zip(heads, state):
        o_ref[:, hs] = acc.astype(o_ref.dtype)


def _dot_sel_rhs(x, sel16):
    hi, lo = _split(x)
    return _dot(hi, sel16) + _dot(lo, sel16)


def _sb_attention(proj, batch, seq):
    blk = ATT_BLOCK
    nq = seq // blk
    width = SB_HEADS_PER_STEP * HEAD_DIM
    qb, kb, vb = COL_SB // width, (COL_SB + GROUP) // width, (COL_SB + 2 * GROUP) // width
    kern = functools.partial(_sb_kernel, blk=blk, scale=HEAD_DIM ** -0.5)
    return pl.pallas_call(
        kern,
        grid=(batch, N_HEADS // SB_HEADS_PER_STEP, nq),
        in_specs=[
            pl.BlockSpec((blk, width), lambda b, h, i: (b * nq + i, qb + h)),
            pl.BlockSpec((seq, width), lambda b, h, i: (b, kb + h)),
            pl.BlockSpec((seq, width), lambda b, h, i: (b, vb + h)),
        ],
        out_specs=pl.BlockSpec((blk, width), lambda b, h, i: (b * nq + i, h)),
        out_shape=jax.ShapeDtypeStruct((batch * seq, GROUP), BF16),
        compiler_params=_cparams("parallel", "parallel", "arbitrary"),
        name="sb_attention",
    )(proj, proj, proj)


def _diff_kernel(lam_ref, q_ref, k_ref, v_ref, g_ref, o_ref, *, blk, scale, lam_init):
    i = pl.program_id(2)
    half = HEAD_DIM // 2
    lane = lax.broadcasted_iota(jnp.int32, (blk, HEAD_DIM), 1)
    heads = [slice(h * HEAD_DIM, (h + 1) * HEAD_DIM) for h in range(DIFF_HEADS_PER_STEP)]
    qs = []
    for hs in heads:
        q = q_ref[:, hs].astype(F32) * scale
        qs.append(jnp.concatenate([jnp.where(lane < half, q, 0.0), jnp.where(lane >= half, q, 0.0)],
                                  axis=0).astype(BF16))
    lead = jnp.maximum(i - 1, 0)
    row = lax.broadcasted_iota(jnp.int32, (2 * blk, 2 * blk), 0) & (blk - 1)
    col = lax.broadcasted_iota(jnp.int32, (2 * blk, 2 * blk), 1)
    causal = col - row <= (i - lead) * blk

    def softmax(s, state, masked):
        m, l_lanes, acc = state
        if masked:
            s = jnp.where(causal, s, -jnp.inf)
        m_new = jnp.maximum(m, jnp.max(s, axis=1, keepdims=True))
        alpha = jnp.exp(m - m_new)
        p = jnp.exp(s - m_new)
        p_lanes = p[:, 0:HEAD_DIM]
        for c in range(1, s.shape[1] // HEAD_DIM):
            p_lanes = p_lanes + p[:, c * HEAD_DIM:(c + 1) * HEAD_DIM]
        return m_new, alpha * l_lanes + p_lanes, alpha * acc, p.astype(BF16)

    def span(first_block, nblk, states, masked):
        rows = pl.ds(pl.multiple_of(first_block * blk, blk), nblk * blk)
        out = []
        s_next = _dot_nt(qs[0], k_ref[rows, heads[0]])
        for h, (hs, state) in enumerate(zip(heads, states)):
            s = s_next
            if h + 1 < len(heads):
                s_next = _dot_nt(qs[h + 1], k_ref[rows, heads[h + 1]])
            m, l_lanes, acc, p = softmax(s, state, masked)
            out.append((m, l_lanes, acc + _dot(p, v_ref[rows, hs])))
        return tuple(out)

    init = (jnp.full((2 * blk, 1), -jnp.inf, F32), jnp.zeros((2 * blk, HEAD_DIM), F32),
            jnp.zeros((2 * blk, HEAD_DIM), F32))
    states = span(lead, 2, (init,) * len(heads), True)
    states = lax.fori_loop(0, lead // 2, lambda n, st: span(2 * n, 2, st, False), states)
    states = lax.fori_loop(0, lead % 2, lambda n, st: span(lead - 1, 1, st, False), states)

    lam_v = lam_ref[...]
    d1 = jnp.sum(lam_v[0:1, :] * lam_v[1:2, :], axis=1, keepdims=True)
    d2 = jnp.sum(lam_v[2:3, :] * lam_v[3:4, :], axis=1, keepdims=True)
    lam = jnp.exp(d1) - jnp.exp(d2) + lam_init
    for hs, (_, l_lanes, acc) in zip(heads, states):
        norm = acc / jnp.sum(l_lanes, axis=1, keepdims=True)
        out = norm[:blk] - lam * norm[blk:]
        o_ref[:, hs] = (_rms(out, g_ref[...]) * (1.0 - lam_init)).astype(o_ref.dtype)


def _diff_attention(proj, lam_vecs, out_gain, lam_init, batch, seq):
    blk = ATT_BLOCK
    nq = seq // blk
    width = DIFF_HEADS_PER_STEP * HEAD_DIM
    qb, kb, vb = COL_DF // width, (COL_DF + GROUP) // width, (COL_DF + 2 * GROUP) // width
    kern = functools.partial(_diff_kernel, blk=blk, scale=(HEAD_DIM // 2) ** -0.5, lam_init=lam_init)
    return pl.pallas_call(
        kern,
        grid=(batch, N_HEADS // DIFF_HEADS_PER_STEP, nq),
        in_specs=[
            pl.BlockSpec(lam_vecs.shape, lambda b, h, i: (0, 0)),
            pl.BlockSpec((blk, width), lambda b, h, i: (b * nq + i, qb + h)),
            pl.BlockSpec((seq, width), lambda b, h, i: (b, kb + h)),
            pl.BlockSpec((seq, width), lambda b, h, i: (b, vb + h)),
            pl.BlockSpec((1, HEAD_DIM), lambda b, h, i: (0, 0)),
        ],
        out_specs=pl.BlockSpec((blk, width), lambda b, h, i: (b * nq + i, h)),
        out_shape=jax.ShapeDtypeStruct((batch * seq, GROUP), BF16),
        compiler_params=_cparams("parallel", "parallel", "arbitrary"),
        name="diff_attention",
    )(lam_vecs, proj, proj, proj, out_gain.reshape(1, HEAD_DIM))


def _chunk_iotas(width=CHUNK):
    row = lax.broadcasted_iota(jnp.int32, (CHUNK, width), 0)
    col = lax.broadcasted_iota(jnp.int32, (CHUNK, width), 1)
    return row, col


def _lower_left_blocks(row, col, level):
    same_pair = (row >> (level + 1)) == (col >> (level + 1))
    return same_pair & (((row >> level) & 1) == 1) & (((col >> level) & 1) == 0)


def _unit_lower_inverses(a_list, row, col):
    eye = jnp.where(row == col, 1.0, 0.0).astype(F32)
    invs = [eye - jnp.where(_lower_left_blocks(row, col, 0), a, 0.0) for a in a_list]
    for level in range(1, CHUNK.bit_length() - 1):
        mask = _lower_left_blocks(row, col, level)
        inv16 = [inv.astype(BF16) for inv in invs]
        left = [_dot_unit_left(i16, jnp.where(mask, a, 0.0)) for i16, a in zip(inv16, a_list)]
        invs = [inv - _dot_unit_right(x, i16) for inv, x, i16 in zip(invs, left, inv16)]
    return invs


def _gdn_kernel(x_ref, z_ref, sm_ref, cw_ref, hp_ref, gain_ref, o_ref,
                xs_ref, qn_ref, kn_ref, vn_ref, u_ref, w_ref, qg_ref, kd_ref, qk_ref, el_ref, st_ref, *, ts):
    t = pl.program_id(1)

    @pl.when(t == 0)
    def _():
        xs_ref[0:CONV_HALO, :] = jnp.zeros((CONV_HALO, 3 * GROUP), F32)
        st_ref[...] = jnp.zeros_like(st_ref)

    xs_ref[CONV_HALO:CONV_HALO + ts, :] = x_ref[...].astype(F32)

    for part, dst in enumerate((qn_ref, kn_ref, vn_ref)):
        for h in range(N_HEADS):
            c0 = part * GROUP + h * HEAD_DIM
            acc = jnp.zeros((ts, HEAD_DIM), F32)
            for j in range(CONV_TAPS):
                off = CONV_HALO - (CONV_TAPS - 1) + j
                acc = acc + xs_ref[off:off + ts, c0:c0 + HEAD_DIM] * cw_ref[j:j + 1, c0:c0 + HEAD_DIM]
            y = _silu(acc)
            if part < 2:
                y = y * lax.rsqrt(jnp.sum(y * y, axis=-1, keepdims=True) + NORM_EPS)
            if part == 0:
                y = y * (HEAD_DIM ** -0.5)
            dst[:, h * HEAD_DIM:(h + 1) * HEAD_DIM] = y

    xs_ref[0:CONV_HALO, :] = xs_ref[ts:ts + CONV_HALO, :]

    row, col = _chunk_iotas()
    strict = row > col
    row_w, col_w = _chunk_iotas(HEAD_DIM)
    incl_w = row_w >= col_w
    eye_w = row_w == col_w
    lower_ones = jnp.where(row >= col, 1.0, 0.0).astype(BF16)
    all_ones = jnp.ones((CHUNK, CHUNK), BF16)

    def prepare(pair, _):
        gates = []
        for sub in range(GDN_CHUNKS_PER_STEP):
            c = pair * GDN_CHUNKS_PER_STEP + sub
            r0 = pl.multiple_of(c * CHUNK, CHUNK)
            sm = sm_ref[pl.ds(r0, CHUNK), :]
            beta_all = _sigmoid(sm)
            g_all = -jnp.exp(hp_ref[0:1, :]) * _softplus(sm + hp_ref[1:2, :])
            g_b = jnp.concatenate(
                [jnp.broadcast_to(g_all[:, N_HEADS + h:N_HEADS + h + 1], (CHUNK, HEAD_DIM))
                 for h in range(N_HEADS)], axis=1)
            gc_all = _dot_sel(lower_ones, g_b)
            gates.append((c, r0, beta_all, gc_all))
        gc_rows = []
        for _, _, _, gc_all in gates:
            gc_diag = jnp.concatenate(
                [jnp.where(eye_w, gc_all[:, h * HEAD_DIM:(h + 1) * HEAD_DIM], 0.0) for h in range(N_HEADS)], axis=1)
            gc_rows.append(_dot_sel(all_ones, gc_diag))

        probs = []
        for (c, r0, beta_all, gc_all), gc_row_all in zip(gates, gc_rows):
            for h in range(N_HEADS):
                hs = slice(h * HEAD_DIM, (h + 1) * HEAD_DIM)
                beta = beta_all[:, h:h + 1]
                gc = gc_all[:, hs]
                decay = jnp.exp(jnp.where(incl_w, gc - gc_row_all[:, hs], -jnp.inf))[:, :CHUNK]
                q = qn_ref[pl.ds(r0, CHUNK), hs]
                k = kn_ref[pl.ds(r0, CHUNK), hs]
                v = vn_ref[pl.ds(r0, CHUNK), hs]
                kb = k * beta
                k16 = k.astype(BF16)
                a_mat = jnp.where(strict, _dot_nt(kb.astype(BF16), k16) * decay, 0.0)
                qk = (_dot_nt(q.astype(BF16), k16) * decay).astype(BF16)
                probs.append(dict(c=c, r0=r0, h=h, hs=hs, gc=gc, q=q, k=k, kb=kb, vb=v * beta, a=a_mat, qk=qk))

        invs = _unit_lower_inverses([p["a"] for p in probs], row, col)
        uws = [_dot_unit_left(inv.astype(BF16), jnp.concatenate([p["vb"], p["kb"] * jnp.exp(p["gc"])], axis=1))
               for inv, p in zip(invs, probs)]
        for p, uw in zip(probs, uws):
            r0, hs, gc, h = p["r0"], p["hs"], p["gc"], p["h"]
            g_last = gc[CHUNK - 1:CHUNK, :]
            u_ref[pl.ds(r0, CHUNK), hs] = uw[:, :HEAD_DIM]
            w_ref[pl.ds(r0, CHUNK), hs] = uw[:, HEAD_DIM:].astype(BF16)
            qg_ref[pl.ds(r0, CHUNK), hs] = (p["q"] * jnp.exp(gc)).astype(BF16)
            kd_ref[pl.ds(r0, CHUNK), hs] = (p["k"] * jnp.exp(g_last - gc)).astype(BF16)
            qk_ref[pl.ds(r0, CHUNK), h * HEAD_DIM:h * HEAD_DIM + CHUNK] = p["qk"]
            el_ref[pl.ds(pl.multiple_of(p["c"] * SUBLANES, SUBLANES), SUBLANES), hs] = jnp.broadcast_to(
                jnp.exp(g_last), (SUBLANES, HEAD_DIM))
        return 0

    lax.fori_loop(0, ts // (CHUNK * GDN_CHUNKS_PER_STEP), prepare, 0)

    def scan(step, _):
        heads = [slice(h * HEAD_DIM, (h + 1) * HEAD_DIM) for h in range(N_HEADS)]
        states = [st_ref[h] for h in range(N_HEADS)]
        for sub in range(GDN_SCAN_CHUNKS_PER_STEP):
            c = step * GDN_SCAN_CHUNKS_PER_STEP + sub
            r0 = pl.multiple_of(c * CHUNK, CHUNK)
            s16 = [s.astype(BF16) for s in states]
            ws = [_dot(w_ref[pl.ds(r0, CHUNK), hs], s) for hs, s in zip(heads, s16)]
            inter = [_dot(qg_ref[pl.ds(r0, CHUNK), hs], s) for hs, s in zip(heads, s16)]
            v_new = [(u_ref[pl.ds(r0, CHUNK), hs] - x).astype(BF16) for hs, x in zip(heads, ws)]
            upd = [_dot_tn(kd_ref[pl.ds(r0, CHUNK), hs], vn) for hs, vn in zip(heads, v_new)]
            for h, hs in enumerate(heads):
                e_last = el_ref[pl.ds(pl.multiple_of(c * SUBLANES, SUBLANES), SUBLANES), hs][0:1, :]
                states[h] = states[h] * e_last + upd[h]
            intra = [_dot(qk_ref[pl.ds(r0, CHUNK), h * HEAD_DIM:h * HEAD_DIM + CHUNK], vn)
                     for h, vn in enumerate(v_new)]
            for h, hs in enumerate(heads):
                zg = z_ref[pl.ds(r0, CHUNK), hs].astype(F32)
                o_ref[pl.ds(r0, CHUNK), hs] = (
                    _rms(inter[h] + intra[h], gain_ref[...]) * _silu(zg)).astype(o_ref.dtype)
        for h in range(N_HEADS):
            st_ref[h] = states[h]
        return 0

    lax.fori_loop(0, ts // (CHUNK * GDN_SCAN_CHUNKS_PER_STEP), scan, 0)


def _gdn(proj, small, conv_w, head_params, out_gain, batch, seq, *, ts):
    nt = seq // ts
    kern = functools.partial(_gdn_kernel, ts=ts)
    qkv_blk = COL_GD // (3 * GROUP)
    z_blk = (COL_GD + 3 * GROUP) // GROUP
    return pl.pallas_call(
        kern,
        grid=(batch, nt),
        in_specs=[
            pl.BlockSpec((ts, 3 * GROUP), lambda b, t: (b * nt + t, qkv_blk)),
            pl.BlockSpec((ts, GROUP), lambda b, t: (b * nt + t, z_blk)),
            pl.BlockSpec((ts, SMALL_W), lambda b, t: (b * nt + t, 0)),
            pl.BlockSpec((CONV_TAPS, 3 * GROUP), lambda b, t: (0, 0)),
            pl.BlockSpec(head_params.shape, lambda b, t: (0, 0)),
            pl.BlockSpec((1, HEAD_DIM), lambda b, t: (0, 0)),
        ],
        out_specs=pl.BlockSpec((ts, GROUP), lambda b, t: (b * nt + t, 0)),
        out_shape=jax.ShapeDtypeStruct((batch * seq, GROUP), BF16),
        scratch_shapes=[
            pltpu.VMEM((ts + CONV_HALO, 3 * GROUP), F32),
            pltpu.VMEM((ts, GROUP), F32),
            pltpu.VMEM((ts, GROUP), F32),
            pltpu.VMEM((ts, GROUP), F32),
            pltpu.VMEM((ts, GROUP), F32),
            pltpu.VMEM((ts, GROUP), BF16),
            pltpu.VMEM((ts, GROUP), BF16),
            pltpu.VMEM((ts, GROUP), BF16),
            pltpu.VMEM((ts, GROUP), BF16),
            pltpu.VMEM((ts // CHUNK * SUBLANES, GROUP), F32),
            pltpu.VMEM((N_HEADS, HEAD_DIM, HEAD_DIM), F32),
        ],
        compiler_params=_cparams("parallel", "arbitrary"),
        name="gated_deltanet",
    )(proj, proj, small, conv_w, head_params, out_gain.reshape(1, HEAD_DIM))


def _gla_kernel(qk_ref, v_ref, og_ref, sm_ref, w2_ref, gb_ref, gain_ref, o_ref,
                att_ref, qin_ref, kout_ref, el_ref, st_ref, *, ts):
    t = pl.program_id(1)

    @pl.when(t == 0)
    def _():
        st_ref[...] = jnp.zeros_like(st_ref)

    row, col = _chunk_iotas()
    lower_ones = jnp.where(row >= col, 1.0, 0.0).astype(BF16)
    eye = row == col
    row_w, lane = _chunk_iotas(HEAD_DIM)
    qk_scale = GLA_DK ** -0.5

    levels = []
    for level in range(CHUNK.bit_length() - 1):
        ref_row = ((row >> (level + 1)) << (level + 1)) + (1 << level)
        pick = jnp.where(col == ref_row, 1.0, 0.0).astype(BF16)
        upper_w = ((row_w >> level) & 1) == 1
        levels.append((pick, _lower_left_blocks(row, col, level), upper_w))

    n_pairs = N_HEADS * GLA_DK // HEAD_DIM
    heads_per_pair = HEAD_DIM // GLA_DK

    def prepare(step, _):
        cums = []
        for sub in range(GLA_CHUNKS_PER_STEP):
            c = step * GLA_CHUNKS_PER_STEP + sub
            r0 = pl.multiple_of(c * CHUNK, CHUNK)
            sm = sm_ref[pl.ds(r0, CHUNK), :].astype(BF16)
            gate = _dot(sm, w2_ref[...]) + gb_ref[...]
            log_a = (gate - _softplus(gate)) * (1.0 / GLA_TAU)
            cums.append((c, r0, _dot_sel(lower_ones, log_a)))
        probs = []
        for c, r0, b_all in cums:
            for pair in range(n_pairs):
                ps = slice(pair * HEAD_DIM, (pair + 1) * HEAD_DIM)
                b = b_all[:, ps]
                q2 = qk_ref[pl.ds(r0, CHUNK), ps].astype(F32) * qk_scale
                k2 = qk_ref[pl.ds(r0, CHUNK),
                            GROUP // 2 + pair * HEAD_DIM:GROUP // 2 + (pair + 1) * HEAD_DIM].astype(F32)
                probs.append(dict(c=c, r0=r0, pair=pair, ps=ps, b=b, q2=q2, k2=k2,
                                  b_refs=[_dot_sel(pick, b) for pick, _, _ in levels]))
        for p in probs:
            b, q2, k2 = p["b"], p["q2"], p["k2"]
            p["q_lv"] = [jnp.where(upper_w, q2 * jnp.exp(jnp.where(upper_w, b - b_ref, 0.0)), 0.0)
                         for (_, _, upper_w), b_ref in zip(levels, p["b_refs"])]
            p["k_lv"] = [jnp.where(upper_w, 0.0, k2 * jnp.exp(jnp.where(upper_w, 0.0, b_ref - b))).astype(BF16)
                         for (_, _, upper_w), b_ref in zip(levels, p["b_refs"])]
        for p in probs:
            r0, b, q2, k2 = p["r0"], p["b"], p["q2"], p["k2"]
            b_last = b[CHUNK - 1:CHUNK, :]
            q_in = (q2 * jnp.exp(b)).astype(BF16)
            k_out = (k2 * jnp.exp(b_last - b)).astype(BF16)
            diag_qk = q2 * k2
            for r in range(heads_per_pair):
                h = p["pair"] * heads_per_pair + r
                hs = slice(h * HEAD_DIM, (h + 1) * HEAD_DIM)
                mine = (lane >= r * GLA_DK) & (lane < (r + 1) * GLA_DK)
                att = jnp.where(eye, jnp.sum(jnp.where(mine, diag_qk, 0.0), axis=1, keepdims=True), 0.0)
                for (_, pair_mask, _), ql, kl in zip(levels, p["q_lv"], p["k_lv"]):
                    s = _dot_nt(jnp.where(mine, ql, 0.0).astype(BF16), kl)
                    att = att + jnp.where(pair_mask, s, 0.0)
                att_ref[pl.ds(r0, CHUNK), h * HEAD_DIM:h * HEAD_DIM + CHUNK] = att.astype(BF16)
                qin_ref[pl.ds(r0, CHUNK), hs] = jnp.where(mine, q_in, jnp.zeros_like(q_in))
                kout_ref[pl.ds(r0, CHUNK), hs] = jnp.where(mine, k_out, jnp.zeros_like(k_out))
            el_ref[pl.ds(pl.multiple_of(p["c"] * SUBLANES, SUBLANES), SUBLANES), p["ps"]] = jnp.broadcast_to(
                jnp.exp(b_last), (SUBLANES, HEAD_DIM))
        return 0

    lax.fori_loop(0, ts // (CHUNK * GLA_CHUNKS_PER_STEP), prepare, 0)

    def scan(step, _):
        heads = [slice(h * HEAD_DIM, (h + 1) * HEAD_DIM) for h in range(N_HEADS)]
        chunks = []
        for sub in range(GLA_SCAN_CHUNKS_PER_STEP):
            c = step * GLA_SCAN_CHUNKS_PER_STEP + sub
            r0 = pl.multiple_of(c * CHUNK, CHUNK)
            values = [v_ref[pl.ds(r0, CHUNK), hs] for hs in heads]
            intra = [_dot(att_ref[pl.ds(r0, CHUNK), h * HEAD_DIM:h * HEAD_DIM + CHUNK], v)
                     for h, v in enumerate(values)]
            upd = [_dot_tn(v, kout_ref[pl.ds(r0, CHUNK), hs]) for hs, v in zip(heads, values)]
            chunks.append((c, r0, intra, upd))
        states = [st_ref[h] for h in range(N_HEADS)]
        for c, r0, intra, upd in chunks:
            inter = [_dot_nt(qin_ref[pl.ds(r0, CHUNK), hs], s.astype(BF16)) for hs, s in zip(heads, states)]
            for h, hs in enumerate(heads):
                ps = slice((h // heads_per_pair) * HEAD_DIM, (h // heads_per_pair + 1) * HEAD_DIM)
                e_last = el_ref[pl.ds(pl.multiple_of(c * SUBLANES, SUBLANES), SUBLANES), ps][0:1, :]
                states[h] = states[h] * e_last + upd[h]
                og = og_ref[pl.ds(r0, CHUNK), hs].astype(F32)
                o_ref[pl.ds(r0, CHUNK), hs] = (
                    _rms(inter[h] + intra[h], gain_ref[...]) * _silu(og)).astype(o_ref.dtype)
        for h in range(N_HEADS):
            st_ref[h] = states[h]
        return 0

    lax.fori_loop(0, ts // (CHUNK * GLA_SCAN_CHUNKS_PER_STEP), scan, 0)


def _gla(proj, small, w2_padded, gate_bias, out_gain, batch, seq, *, ts):
    nt = seq // ts
    kern = functools.partial(_gla_kernel, ts=ts)
    qk_blk = COL_GL // GROUP
    return pl.pallas_call(
        kern,
        grid=(batch, nt),
        in_specs=[
            pl.BlockSpec((ts, GROUP), lambda b, t: (b * nt + t, qk_blk)),
            pl.BlockSpec((ts, GROUP), lambda b, t: (b * nt + t, qk_blk + 1)),
            pl.BlockSpec((ts, GROUP), lambda b, t: (b * nt + t, qk_blk + 2)),
            pl.BlockSpec((ts, SMALL_W), lambda b, t: (b * nt + t, 0)),
            pl.BlockSpec(w2_padded.shape, lambda b, t: (0, 0)),
            pl.BlockSpec((1, N_HEADS * GLA_DK), lambda b, t: (0, 0)),
            pl.BlockSpec((1, HEAD_DIM), lambda b, t: (0, 0)),
        ],
        out_specs=pl.BlockSpec((ts, GROUP), lambda b, t: (b * nt + t, 0)),
        out_shape=jax.ShapeDtypeStruct((batch * seq, GROUP), BF16),
        scratch_shapes=[
            pltpu.VMEM((ts, GROUP), BF16),
            pltpu.VMEM((ts, GROUP), BF16),
            pltpu.VMEM((ts, GROUP), BF16),
            pltpu.VMEM((ts // CHUNK * SUBLANES, N_HEADS * GLA_DK), F32),
            pltpu.VMEM((N_HEADS, HEAD_DIM, HEAD_DIM), F32),
        ],
        compiler_params=_cparams("parallel", "arbitrary"),
        name="gla",
    )(proj, proj, proj, small, w2_padded, gate_bias.reshape(1, -1), out_gain.reshape(1, HEAD_DIM))


def _out_proj_kernel(x_ref, m0_ref, m1_ref, m2_ref, m3_ref, w_ref, o_ref):
    acc = x_ref[...]
    for g, m_ref in enumerate((m0_ref, m1_ref, m2_ref, m3_ref)):
        acc = acc + _dot(m_ref[...], w_ref[g * GROUP:(g + 1) * GROUP, :])
    o_ref[...] = acc


def _out_proj(x, mixes, w_out, layer, *, tm):
    t, d = x.shape
    mix_spec = pl.BlockSpec((tm, GROUP), lambda i: (i, 0))
    return pl.pallas_call(
        _out_proj_kernel,
        grid=(t // tm,),
        in_specs=[pl.BlockSpec((tm, d), lambda i: (i, 0)), mix_spec, mix_spec, mix_spec, mix_spec,
                  pl.BlockSpec((None,) + w_out.shape[1:], lambda i: (layer, 0, 0))],
        out_specs=pl.BlockSpec((tm, d), lambda i: (i, 0)),
        out_shape=jax.ShapeDtypeStruct((t, d), F32),
        compiler_params=_cparams("parallel"),
        name="out_proj_residual",
    )(x, *mixes, w_out)


def _ffn_kernel(x_ref, g_ref, wg_ref, wu_ref, wd_ref, og_ref, o_ref, h_ref, *, norm_output):
    f = pl.program_id(1)

    def swiglu_tile(h):
        hid = _silu(_dot(h, wg_ref[...])) * _dot(h, wu_ref[...])
        return _dot(hid.astype(BF16), wd_ref[...])

    @pl.when(f == 0)
    def _():
        x = x_ref[...]
        h = _rms(x, g_ref[...]).astype(BF16)
        h_ref[...] = h
        o_ref[...] = x + swiglu_tile(h)

    @pl.when(f > 0)
    def _():
        o_ref[...] += swiglu_tile(h_ref[...])

    if norm_output:
        @pl.when(f == pl.num_programs(1) - 1)
        def _():
            o_ref[...] = _rms(o_ref[...], og_ref[...])


def _ffn(x, gain, w_gate, w_up, w_down, layer, out_gain, *, norm_output, tm, tf):
    t, d = x.shape
    hidden = w_gate.shape[2]
    return pl.pallas_call(
        functools.partial(_ffn_kernel, norm_output=norm_output),
        grid=(t // tm, hidden // tf),
        in_specs=[
            pl.BlockSpec((tm, d), lambda i, f: (i, 0)),
            pl.BlockSpec((1, d), lambda i, f: (0, 0)),
            pl.BlockSpec((None, d, tf), lambda i, f: (layer, 0, f)),
            pl.BlockSpec((None, d, tf), lambda i, f: (layer, 0, f)),
            pl.BlockSpec((None, tf, d), lambda i, f: (layer, f, 0)),
            pl.BlockSpec((1, d), lambda i, f: (0, 0)),
        ],
        out_specs=pl.BlockSpec((tm, d), lambda i, f: (i, 0)),
        out_shape=jax.ShapeDtypeStruct((t, d), F32),
        scratch_shapes=[pltpu.VMEM((tm, d), BF16)],
        compiler_params=_cparams("parallel", "arbitrary"),
        name="ffn_residual",
    )(x, gain.reshape(1, d), w_gate, w_up, w_down, out_gain.reshape(1, d))


def _pack_w_in(w_in):
    gd_small = COL_GD + 4 * GROUP
    df0 = gd_small + 2 * N_HEADS
    gl0 = df0 + 3 * GROUP
    rank0 = gl0 + 3 * GROUP
    pad = jnp.zeros(w_in.shape[:-1] + (IN_COLS_PACKED - COL_SMALL - 2 * N_HEADS - GLA_RANK,), BF16)
    parts = [w_in[..., :gd_small], w_in[..., df0:rank0], w_in[..., gd_small:df0], w_in[..., rank0:rank0 + GLA_RANK]]
    return jnp.concatenate([part.astype(BF16) for part in parts] + [pad], axis=-1)


def _mixer_layer(x, l, p, big, batch, seq):
    lam_init = 0.8 - 0.6 * math.exp(-0.3 * l)
    proj, small = _norm_in_proj(x, p["attn_norm"], big["w_in"], l, tm=IN_PROJ_TILE[0], tn=IN_PROJ_TILE[1])
    o_sb = _sb_attention(proj, batch, seq)
    head_params = jnp.zeros((SUBLANES, SMALL_W), F32)
    head_params = head_params.at[0, N_HEADS:2 * N_HEADS].set(p["gdn_a_log"])
    head_params = head_params.at[1, N_HEADS:2 * N_HEADS].set(p["gdn_dt_bias"])
    o_gd = _gdn(proj, small, p["gdn_conv_w"], head_params, p["gdn_out_norm"], batch, seq, ts=SCAN_TILE)
    lam_vecs = jnp.stack([p["diff_lam_q1"], p["diff_lam_k1"], p["diff_lam_q2"], p["diff_lam_k2"]])
    o_df = _diff_attention(proj, lam_vecs, p["diff_out_norm"], lam_init, batch, seq)
    w2 = jnp.zeros((SMALL_W, N_HEADS * GLA_DK), F32).at[2 * N_HEADS:2 * N_HEADS + GLA_RANK].set(p["gla_gate_w2"])
    o_gl = _gla(proj, small, w2.astype(BF16), p["gla_gate_b"], p["gla_out_norm"], batch, seq, ts=SCAN_TILE)
    return _out_proj(x, (o_sb, o_gd, o_df, o_gl), big["w_out"], l, tm=OUT_PROJ_ROWS)


def kernel(x, attn_norm, w_in, gdn_conv_w, gdn_a_log, gdn_dt_bias, gdn_out_norm, diff_lam_q1, diff_lam_k1,
           diff_lam_q2, diff_lam_k2, diff_out_norm, gla_gate_w2, gla_gate_b, gla_out_norm, w_out, ffn_norm,
           w_gate, w_up, w_down, final_norm):
    batch, seq, d = x.shape
    depth = w_in.shape[0]
    tokens = batch * seq
    assert seq % (2 * ATT_BLOCK) == 0 and seq % SCAN_TILE == 0, (seq, ATT_BLOCK, SCAN_TILE)
    assert tokens % IN_PROJ_TILE[0] == 0 and tokens % FFN_TILE[0] == 0 and tokens % OUT_PROJ_ROWS == 0, tokens
    assert w_gate.shape[2] % FFN_TILE[1] == 0, w_gate.shape
    stacked = dict(attn_norm=attn_norm, gdn_conv_w=gdn_conv_w, gdn_a_log=gdn_a_log,
                   gdn_dt_bias=gdn_dt_bias, gdn_out_norm=gdn_out_norm, diff_lam_q1=diff_lam_q1,
                   diff_lam_k1=diff_lam_k1, diff_lam_q2=diff_lam_q2, diff_lam_k2=diff_lam_k2,
                   diff_out_norm=diff_out_norm, gla_gate_w2=gla_gate_w2, gla_gate_b=gla_gate_b,
                   gla_out_norm=gla_out_norm)
    big = dict(w_in=_pack_w_in(w_in), w_out=w_out.astype(BF16), w_gate=w_gate.astype(BF16),
               w_up=w_up.astype(BF16), w_down=w_down.astype(BF16))
    xt = x.reshape(tokens, d)
    for l in range(depth):
        p = {name: arr[l] for name, arr in stacked.items()}
        xt = _mixer_layer(xt, l, p, big, batch, seq)
        xt = _ffn(xt, ffn_norm[l], big["w_gate"], big["w_up"], big["w_down"], l, final_norm,
                  norm_output=(l == depth - 1), tm=FFN_TILE[0], tf=FFN_TILE[1])
    return xt.reshape(batch, seq, d)
```

```python
import functools
import math

import jax
import jax.numpy as jnp
from jax import lax
from jax.experimental import pallas as pl
from jax.experimental.pallas import tpu as pltpu

F32 = jnp.float32
BF16 = jnp.bfloat16

NORM_EPS = 1e-6
LOG2_E = math.log2(math.e)
SB_DEAD_LOG2 = 110.0 * LOG2_E
N_HEADS = 4
HEAD_DIM = 128
GROUP = N_HEADS * HEAD_DIM
CHUNK = 64
CONV_TAPS = 4
GLA_DK = 64
GLA_RANK = 16
GLA_TAU = 16.0
CONV_HALO = 16
SUBLANES = 8

ATT_BLOCK = 256
DIFF_HEADS_PER_STEP = 4
SB_HEADS_PER_STEP = 4
SCAN_TILE = 512
GDN_CONV_CHUNKS_PER_STEP = 4
GDN_CHUNKS_PER_STEP = 4
GDN_SCAN_CHUNKS_PER_STEP = 4
GLA_CHUNKS_PER_STEP = 4
GLA_SCAN_CHUNKS_PER_STEP = 4
IN_PROJ_TILE = (1024, 1024)
OUT_PROJ_ROWS = 512
FFN_TILE = (1024, 512)
VMEM_LIMIT = 56 * 1024 * 1024

COL_SB = 0
COL_GD = 3 * GROUP
COL_DF = 7 * GROUP
COL_GL = 10 * GROUP
COL_SMALL = 13 * GROUP
SMALL_W = 128
IN_COLS_PACKED = 14 * GROUP


def _cparams(*sem):
    return pltpu.CompilerParams(dimension_semantics=sem, vmem_limit_bytes=VMEM_LIMIT)


def _dot(a, b):
    return jnp.dot(a, b, preferred_element_type=F32)


def _dot_nt(a, b):
    return lax.dot_general(a, b, (((1,), (1,)), ((), ())), preferred_element_type=F32)


def _dot_tn(a, b):
    return lax.dot_general(a, b, (((0,), (0,)), ((), ())), preferred_element_type=F32)


def _split(x):
    hi = x.astype(BF16)
    return hi, (x - hi.astype(F32)).astype(BF16)


def _dot_sel(sel16, x):
    hi, lo = _split(x)
    return _dot(sel16, hi) + _dot(sel16, lo)


def _dot_unit_left(unit16, x):
    hi, lo = _split(x)
    return _dot(unit16, hi) + _dot(unit16, lo)


def _dot_unit_right(x, unit16):
    hi, lo = _split(x)
    return _dot(hi, unit16) + _dot(lo, unit16)


def _rms(x, gain):
    return x * lax.rsqrt(jnp.mean(x * x, axis=-1, keepdims=True) + NORM_EPS) * gain


def _silu(x):
    return x / (1.0 + jnp.exp(-x))


def _sigmoid(x):
    return 1.0 / (1.0 + jnp.exp(-x))


def _softplus(x):
    return jnp.maximum(x, 0.0) + jnp.log(1.0 + jnp.exp(-jnp.abs(x)))


def _norm_matmul_kernel(x_ref, g_ref, w_ref, o_ref, small_ref, h_ref):
    j = pl.program_id(1)
    last = pl.num_programs(1) - 1

    @pl.when(j == 0)
    def _():
        h = _rms(x_ref[...], g_ref[...]).astype(BF16)
        h_ref[...] = h
        o_ref[...] = _dot(h, w_ref[...]).astype(o_ref.dtype)

    @pl.when(j > 0)
    def _():
        y = _dot(h_ref[...], w_ref[...])
        o_ref[...] = y.astype(o_ref.dtype)

        @pl.when(j == last)
        def _():
            small_ref[...] = y[:, y.shape[1] - small_ref.shape[1]:]


def _norm_in_proj(x, gain, w, layer, *, tm, tn):
    t, d = x.shape
    return pl.pallas_call(
        _norm_matmul_kernel,
        grid=(t // tm, IN_COLS_PACKED // tn),
        in_specs=[
            pl.BlockSpec((tm, d), lambda i, j: (i, 0)),
            pl.BlockSpec((1, d), lambda i, j: (0, 0)),
            pl.BlockSpec((None, d, tn), lambda i, j: (layer, 0, j)),
        ],
        out_specs=[
            pl.BlockSpec((tm, tn), lambda i, j: (i, j)),
            pl.BlockSpec((tm, IN_COLS_PACKED - COL_SMALL), lambda i, j: (i, 0)),
        ],
        out_shape=[jax.ShapeDtypeStruct((t, IN_COLS_PACKED), BF16),
                   jax.ShapeDtypeStruct((t, IN_COLS_PACKED - COL_SMALL), F32)],
        scratch_shapes=[pltpu.VMEM((tm, d), BF16)],
        compiler_params=_cparams("parallel", "arbitrary"),
        name="norm_in_proj",
    )(x, gain.reshape(1, d), w)


def _sb_kernel(q_ref, k_ref, v_ref, o_ref, *, blk, scale):
    i = pl.program_id(2)
    heads = [slice(h * HEAD_DIM, (h + 1) * HEAD_DIM) for h in range(SB_HEADS_PER_STEP)]
    row = lax.broadcasted_iota(jnp.int32, (blk, blk), 0)
    col = lax.broadcasted_iota(jnp.int32, (blk, blk), 1)
    suffix = jnp.where(row > col, 1.0, 0.0).astype(BF16)
    lead = jnp.maximum(i - 1, 0)
    col_minus_row = (lax.broadcasted_iota(jnp.int32, (blk, 2 * blk), 1)
                     - lax.broadcasted_iota(jnp.int32, (blk, 2 * blk), 0))
    causal = col_minus_row < (i - lead) * blk

    def span(first_block, nblk, state, masked):
        rows = pl.ds(pl.multiple_of(first_block * blk, blk), nblk * blk)
        parts = [slice(b * blk, (b + 1) * blk) for b in range(nblk)]
        zs = [_dot_nt(q_ref[:, hs], k_ref[rows, hs]) * (scale * LOG2_E) for hs in heads]
        sps = []
        for z in zs:
            sp = jnp.maximum(z, 0.0) + jnp.log2(1.0 + jnp.exp2(-jnp.abs(z)))
            sps.append(jnp.where(causal, sp, 0.0) if masked else sp)
        tails = [[_dot_sel_rhs(sp[:, s], suffix) for s in parts] for sp in sps]
        out = []
        for hs, z, sp, tail, (carry, acc) in zip(heads, zs, sps, tails, state):
            sums = [jnp.sum(sp[:, s], axis=1, keepdims=True) for s in parts]
            offsets = [None] * nblk
            for b in reversed(range(nblk)):
                offsets[b] = carry
                carry = carry + sums[b]
            att = jnp.concatenate(
                [jnp.exp2(z[:, s] - sp[:, s] - tail[b] - offsets[b])
                 for b, s in enumerate(parts)], axis=1)
            if masked:
                att = jnp.where(causal, att, 0.0)
            out.append((carry, acc + _dot(att.astype(BF16), v_ref[rows, hs])))
        return tuple(out)

    init = (jnp.zeros((blk, 1), F32), jnp.zeros((blk, HEAD_DIM), F32))
    state = span(lead, 2, (init,) * len(heads), True)

    def alive(state):
        low = state[0][0]
        for carry, _ in state[1:]:
            low = jnp.minimum(low, carry)
        return jnp.min(low) < SB_DEAD_LOG2

    _, state = lax.while_loop(
        lambda carried: jnp.logical_and(carried[0] < lead // 2, alive(carried[1])),
        lambda carried: (carried[0] + 1, span(lead - 2 - 2 * carried[0], 2, carried[1], False)),
        (jnp.int32(0), state))
    odd_left = jnp.where(alive(state), lead % 2, 0)
    state = lax.fori_loop(0, odd_left, lambda n, st: span(0, 1, st, False), state)
    for hs, (_, acc) in zip(heads, state):
        o_ref[:, hs] = acc.astype(o_ref.dtype)


def _dot_sel_rhs(x, sel16):
    hi, lo = _split(x)
    return _dot(hi, sel16) + _dot(lo, sel16)


def _sb_attention(proj, batch, seq):
    blk = ATT_BLOCK
    nq = seq // blk
    width = SB_HEADS_PER_STEP * HEAD_DIM
    qb, kb, vb = COL_SB // width, (COL_SB + GROUP) // width, (COL_SB + 2 * GROUP) // width
    kern = functools.partial(_sb_kernel, blk=blk, scale=HEAD_DIM ** -0.5)
    return pl.pallas_call(
        kern,
        grid=(batch, N_HEADS // SB_HEADS_PER_STEP, nq),
        in_specs=[
            pl.BlockSpec((blk, width), lambda b, h, i: (b * nq + i, qb + h)),
            pl.BlockSpec((seq, width), lambda b, h, i: (b, kb + h)),
            pl.BlockSpec((seq, width), lambda b, h, i: (b, vb + h)),
        ],
        out_specs=pl.BlockSpec((blk, width), lambda b, h, i: (b * nq + i, h)),
        out_shape=jax.ShapeDtypeStruct((batch * seq, GROUP), BF16),
        compiler_params=_cparams("parallel", "parallel", "arbitrary"),
        name="sb_attention",
    )(proj, proj, proj)


def _diff_kernel(lam_ref, q_ref, k_ref, v_ref, g_ref, o_ref, *, blk, scale, lam_init):
    i = pl.program_id(2)
    half = HEAD_DIM // 2
    lane = lax.broadcasted_iota(jnp.int32, (blk, HEAD_DIM), 1)
    heads = [slice(h * HEAD_DIM, (h + 1) * HEAD_DIM) for h in range(DIFF_HEADS_PER_STEP)]
    qs = []
    for hs in heads:
        q = q_ref[:, hs].astype(F32) * scale
        qs.append(jnp.concatenate([jnp.where(lane < half, q, 0.0), jnp.where(lane >= half, q, 0.0)],
                                  axis=0).astype(BF16))
    lead = jnp.maximum(i - 1, 0)
    row = lax.broadcasted_iota(jnp.int32, (2 * blk, 2 * blk), 0) & (blk - 1)
    col = lax.broadcasted_iota(jnp.int32, (2 * blk, 2 * blk), 1)
    causal = col - row <= (i - lead) * blk

    def softmax(s, state, masked):
        m, l_lanes, acc = state
        if masked:
            s = jnp.where(causal, s, -jnp.inf)
        m_new = jnp.maximum(m, jnp.max(s, axis=1, keepdims=True))
        alpha = jnp.exp(m - m_new)
        p = jnp.exp(s - m_new)
        p_lanes = p[:, 0:HEAD_DIM]
        for c in range(1, s.shape[1] // HEAD_DIM):
            p_lanes = p_lanes + p[:, c * HEAD_DIM:(c + 1) * HEAD_DIM]
        return m_new, alpha * l_lanes + p_lanes, alpha * acc, p.astype(BF16)

    def span(first_block, nblk, states, masked):
        rows = pl.ds(pl.multiple_of(first_block * blk, blk), nblk * blk)
        out = []
        s_next = _dot_nt(qs[0], k_ref[rows, heads[0]])
        for h, (hs, state) in enumerate(zip(heads, states)):
            s = s_next
            if h + 1 < len(heads):
                s_next = _dot_nt(qs[h + 1], k_ref[rows, heads[h + 1]])
            m, l_lanes, acc, p = softmax(s, state, masked)
            out.append((m, l_lanes, acc + _dot(p, v_ref[rows, hs])))
        return tuple(out)

    init = (jnp.full((2 * blk, 1), -jnp.inf, F32), jnp.zeros((2 * blk, HEAD_DIM), F32),
            jnp.zeros((2 * blk, HEAD_DIM), F32))
    states = span(lead, 2, (init,) * len(heads), True)
    states = lax.fori_loop(0, lead // 2, lambda n, st: span(2 * n, 2, st, False), states)
    states = lax.fori_loop(0, lead % 2, lambda n, st: span(lead - 1, 1, st, False), states)

    lam_v = lam_ref[...]
    d1 = jnp.sum(lam_v[0:1, :] * lam_v[1:2, :], axis=1, keepdims=True)
    d2 = jnp.sum(lam_v[2:3, :] * lam_v[3:4, :], axis=1, keepdims=True)
    lam = jnp.exp(d1) - jnp.exp(d2) + lam_init
    for hs, (_, l_lanes, acc) in zip(heads, states):
        norm = acc / jnp.sum(l_lanes, axis=1, keepdims=True)
        out = norm[:blk] - lam * norm[blk:]
        o_ref[:, hs] = (_rms(out, g_ref[...]) * (1.0 - lam_init)).astype(o_ref.dtype)


def _diff_attention(proj, lam_vecs, out_gain, lam_init, batch, seq):
    blk = ATT_BLOCK
    nq = seq // blk
    width = DIFF_HEADS_PER_STEP * HEAD_DIM
    qb, kb, vb = COL_DF // width, (COL_DF + GROUP) // width, (COL_DF + 2 * GROUP) // width
    kern = functools.partial(_diff_kernel, blk=blk, scale=(HEAD_DIM // 2) ** -0.5, lam_init=lam_init)
    return pl.pallas_call(
        kern,
        grid=(batch, N_HEADS // DIFF_HEADS_PER_STEP, nq),
        in_specs=[
            pl.BlockSpec(lam_vecs.shape, lambda b, h, i: (0, 0)),
            pl.BlockSpec((blk, width), lambda b, h, i: (b * nq + i, qb + h)),
            pl.BlockSpec((seq, width), lambda b, h, i: (b, kb + h)),
            pl.BlockSpec((seq, width), lambda b, h, i: (b, vb + h)),
            pl.BlockSpec((1, HEAD_DIM), lambda b, h, i: (0, 0)),
        ],
        out_specs=pl.BlockSpec((blk, width), lambda b, h, i: (b * nq + i, h)),
        out_shape=jax.ShapeDtypeStruct((batch * seq, GROUP), BF16),
        compiler_params=_cparams("parallel", "parallel", "arbitrary"),
        name="diff_attention",
    )(lam_vecs, proj, proj, proj, out_gain.reshape(1, HEAD_DIM))


def _chunk_iotas(width=CHUNK):
    row = lax.broadcasted_iota(jnp.int32, (CHUNK, width), 0)
    col = lax.broadcasted_iota(jnp.int32, (CHUNK, width), 1)
    return row, col


def _lower_left_blocks(row, col, level):
    same_pair = (row >> (level + 1)) == (col >> (level + 1))
    return same_pair & (((row >> level) & 1) == 1) & (((col >> level) & 1) == 0)


def _unit_lower_inverses(a_list, row, col):
    eye = jnp.where(row == col, 1.0, 0.0).astype(F32)
    invs = [eye - jnp.where(_lower_left_blocks(row, col, 0), a, 0.0) for a in a_list]
    for level in range(1, CHUNK.bit_length() - 1):
        mask = _lower_left_blocks(row, col, level)
        inv16 = [inv.astype(BF16) for inv in invs]
        left = [_dot_unit_left(i16, jnp.where(mask, a, 0.0)) for i16, a in zip(inv16, a_list)]
        invs = [inv - _dot_unit_right(x, i16) for inv, x, i16 in zip(invs, left, inv16)]
    return invs


def _gdn_kernel(x_ref, z_ref, sm_ref, cw_ref, hp_ref, gain_ref, o_ref,
                xs_ref, qn_ref, kn_ref, vn_ref, u_ref, w_ref, qg_ref, kd_ref, qk_ref, el_ref, st_ref, *, ts):
    t = pl.program_id(1)

    @pl.when(t == 0)
    def _():
        xs_ref[0:CONV_HALO, :] = jnp.zeros((CONV_HALO, 3 * GROUP), BF16)
        st_ref[...] = jnp.zeros_like(st_ref)

    xs_ref[CONV_HALO:CONV_HALO + ts, :] = x_ref[...]

    win = CHUNK + CONV_HALO
    shift_row = lax.broadcasted_iota(jnp.int32, ((CONV_TAPS - 1) * CHUNK, win), 0)
    shift_col = lax.broadcasted_iota(jnp.int32, ((CONV_TAPS - 1) * CHUNK, win), 1)
    tap = shift_row >> (CHUNK.bit_length() - 1)
    shifts = jnp.where(shift_col == (shift_row & (CHUNK - 1)) + CONV_HALO - (CONV_TAPS - 1) + tap,
                       1.0, 0.0).astype(BF16)

    def conv(step, _):
        starts = [pl.multiple_of((step * GDN_CONV_CHUNKS_PER_STEP + sub) * CHUNK, CHUNK)
                  for sub in range(GDN_CONV_CHUNKS_PER_STEP)]
        windows = [xs_ref[pl.ds(r0, win), :] for r0 in starts]
        shifted = [_dot(shifts, window) for window in windows]
        for r0, window, taps in zip(starts, windows, shifted):
            acc = window[CONV_HALO:].astype(F32) * cw_ref[CONV_TAPS - 1:CONV_TAPS, :]
            for j in range(CONV_TAPS - 1):
                acc = acc + taps[j * CHUNK:(j + 1) * CHUNK] * cw_ref[j:j + 1, :]
            y = _silu(acc)
            for part, dst in enumerate((qn_ref, kn_ref, vn_ref)):
                for h in range(N_HEADS):
                    hs = slice(h * HEAD_DIM, (h + 1) * HEAD_DIM)
                    yb = y[:, part * GROUP + h * HEAD_DIM:part * GROUP + (h + 1) * HEAD_DIM]
                    if part < 2:
                        yb = yb * lax.rsqrt(jnp.sum(yb * yb, axis=-1, keepdims=True) + NORM_EPS)
                    if part == 0:
                        yb = yb * (HEAD_DIM ** -0.5)
                    dst[pl.ds(r0, CHUNK), hs] = yb
        return 0

    lax.fori_loop(0, ts // (CHUNK * GDN_CONV_CHUNKS_PER_STEP), conv, 0)
    xs_ref[0:CONV_HALO, :] = xs_ref[ts:ts + CONV_HALO, :]

    row, col = _chunk_iotas()
    strict = row > col
    row_w, col_w = _chunk_iotas(HEAD_DIM)
    incl_w = row_w >= col_w
    eye_w = row_w == col_w
    lower_ones = jnp.where(row >= col, 1.0, 0.0).astype(BF16)
    all_ones = jnp.ones((CHUNK, CHUNK), BF16)

    def prepare(pair, _):
        gates = []
        for sub in range(GDN_CHUNKS_PER_STEP):
            c = pair * GDN_CHUNKS_PER_STEP + sub
            r0 = pl.multiple_of(c * CHUNK, CHUNK)
            sm = sm_ref[pl.ds(r0, CHUNK), :]
            beta_all = _sigmoid(sm)
            g_all = -jnp.exp(hp_ref[0:1, :]) * _softplus(sm + hp_ref[1:2, :])
            g_b = jnp.concatenate(
                [jnp.broadcast_to(g_all[:, N_HEADS + h:N_HEADS + h + 1], (CHUNK, HEAD_DIM))
                 for h in range(N_HEADS)], axis=1)
            gc_all = _dot_sel(lower_ones, g_b)
            gates.append((c, r0, beta_all, gc_all))
        gc_rows = []
        for _, _, _, gc_all in gates:
            gc_diag = jnp.concatenate(
                [jnp.where(eye_w, gc_all[:, h * HEAD_DIM:(h + 1) * HEAD_DIM], 0.0) for h in range(N_HEADS)], axis=1)
            gc_rows.append(_dot_sel(all_ones, gc_diag))

        probs = []
        for (c, r0, beta_all, gc_all), gc_row_all in zip(gates, gc_rows):
            for h in range(N_HEADS):
                hs = slice(h * HEAD_DIM, (h + 1) * HEAD_DIM)
                beta = beta_all[:, h:h + 1]
                gc = gc_all[:, hs]
                decay = jnp.exp(jnp.where(incl_w, gc - gc_row_all[:, hs], -jnp.inf))[:, :CHUNK]
                q = qn_ref[pl.ds(r0, CHUNK), hs]
                k = kn_ref[pl.ds(r0, CHUNK), hs]
                v = vn_ref[pl.ds(r0, CHUNK), hs]
                kb = k * beta
                k16 = k.astype(BF16)
                a_mat = jnp.where(strict, _dot_nt(kb.astype(BF16), k16) * decay, 0.0)
                qk = (_dot_nt(q.astype(BF16), k16) * decay).astype(BF16)
                probs.append(dict(c=c, r0=r0, h=h, hs=hs, gc=gc, q=q, k=k, kb=kb, vb=v * beta, a=a_mat, qk=qk))

        invs = _unit_lower_inverses([p["a"] for p in probs], row, col)
        uws = [_dot_unit_left(inv.astype(BF16), jnp.concatenate([p["vb"], p["kb"] * jnp.exp(p["gc"])], axis=1))
               for inv, p in zip(invs, probs)]
        for p, uw in zip(probs, uws):
            r0, hs, gc, h = p["r0"], p["hs"], p["gc"], p["h"]
            g_last = gc[CHUNK - 1:CHUNK, :]
            u_ref[pl.ds(r0, CHUNK), hs] = uw[:, :HEAD_DIM]
            w_ref[pl.ds(r0, CHUNK), hs] = uw[:, HEAD_DIM:].astype(BF16)
            qg_ref[pl.ds(r0, CHUNK), hs] = (p["q"] * jnp.exp(gc)).astype(BF16)
            kd_ref[pl.ds(r0, CHUNK), hs] = (p["k"] * jnp.exp(g_last - gc)).astype(BF16)
            qk_ref[pl.ds(r0, CHUNK), h * HEAD_DIM:h * HEAD_DIM + CHUNK] = p["qk"]
            el_ref[pl.ds(pl.multiple_of(p["c"] * SUBLANES, SUBLANES), SUBLANES), hs] = jnp.broadcast_to(
                jnp.exp(g_last), (SUBLANES, HEAD_DIM))
        return 0

    lax.fori_loop(0, ts // (CHUNK * GDN_CHUNKS_PER_STEP), prepare, 0)

    def scan(step, _):
        heads = [slice(h * HEAD_DIM, (h + 1) * HEAD_DIM) for h in range(N_HEADS)]
        states = [st_ref[h] for h in range(N_HEADS)]
        for sub in range(GDN_SCAN_CHUNKS_PER_STEP):
            c = step * GDN_SCAN_CHUNKS_PER_STEP + sub
            r0 = pl.multiple_of(c * CHUNK, CHUNK)
            s16 = [s.astype(BF16) for s in states]
            ws = [_dot(w_ref[pl.ds(r0, CHUNK), hs], s) for hs, s in zip(heads, s16)]
            inter = [_dot(qg_ref[pl.ds(r0, CHUNK), hs], s) for hs, s in zip(heads, s16)]
            v_new = [(u_ref[pl.ds(r0, CHUNK), hs] - x).astype(BF16) for hs, x in zip(heads, ws)]
            upd = [_dot_tn(kd_ref[pl.ds(r0, CHUNK), hs], vn) for hs, vn in zip(heads, v_new)]
            for h, hs in enumerate(heads):
                e_last = el_ref[pl.ds(pl.multiple_of(c * SUBLANES, SUBLANES), SUBLANES), hs][0:1, :]
                states[h] = states[h] * e_last + upd[h]
            intra = [_dot(qk_ref[pl.ds(r0, CHUNK), h * HEAD_DIM:h * HEAD_DIM + CHUNK], vn)
                     for h, vn in enumerate(v_new)]
            for h, hs in enumerate(heads):
                zg = z_ref[pl.ds(r0, CHUNK), hs].astype(F32)
                o_ref[pl.ds(r0, CHUNK), hs] = (
                    _rms(inter[h] + intra[h], gain_ref[...]) * _silu(zg)).astype(o_ref.dtype)
        for h in range(N_HEADS):
            st_ref[h] = states[h]
        return 0

    lax.fori_loop(0, ts // (CHUNK * GDN_SCAN_CHUNKS_PER_STEP), scan, 0)


def _gdn(proj, small, conv_w, head_params, out_gain, batch, seq, *, ts):
    nt = seq // ts
    kern = functools.partial(_gdn_kernel, ts=ts)
    qkv_blk = COL_GD // (3 * GROUP)
    z_blk = (COL_GD + 3 * GROUP) // GROUP
    return pl.pallas_call(
        kern,
        grid=(batch, nt),
        in_specs=[
            pl.BlockSpec((ts, 3 * GROUP), lambda b, t: (b * nt + t, qkv_blk)),
            pl.BlockSpec((ts, GROUP), lambda b, t: (b * nt + t, z_blk)),
            pl.BlockSpec((ts, SMALL_W), lambda b, t: (b * nt + t, 0)),
            pl.BlockSpec((CONV_TAPS, 3 * GROUP), lambda b, t: (0, 0)),
            pl.BlockSpec(head_params.shape, lambda b, t: (0, 0)),
            pl.BlockSpec((1, HEAD_DIM), lambda b, t: (0, 0)),
        ],
        out_specs=pl.BlockSpec((ts, GROUP), lambda b, t: (b * nt + t, 0)),
        out_shape=jax.ShapeDtypeStruct((batch * seq, GROUP), BF16),
        scratch_shapes=[
            pltpu.VMEM((ts + CONV_HALO, 3 * GROUP), BF16),
            pltpu.VMEM((ts, GROUP), F32),
            pltpu.VMEM((ts, GROUP), F32),
            pltpu.VMEM((ts, GROUP), F32),
            pltpu.VMEM((ts, GROUP), F32),
            pltpu.VMEM((ts, GROUP), BF16),
            pltpu.VMEM((ts, GROUP), BF16),
            pltpu.VMEM((ts, GROUP), BF16),
            pltpu.VMEM((ts, GROUP), BF16),
            pltpu.VMEM((ts // CHUNK * SUBLANES, GROUP), F32),
            pltpu.VMEM((N_HEADS, HEAD_DIM, HEAD_DIM), F32),
        ],
        compiler_params=_cparams("parallel", "arbitrary"),
        name="gated_deltanet",
    )(proj, proj, small, conv_w, head_params, out_gain.reshape(1, HEAD_DIM))


def _gla_kernel(qk_ref, v_ref, og_ref, sm_ref, w2_ref, gb_ref, gain_ref, o_ref,
                att_ref, qin_ref, kout_ref, el_ref, st_ref, *, ts):
    t = pl.program_id(1)

    @pl.when(t == 0)
    def _():
        st_ref[...] = jnp.zeros_like(st_ref)

    row, col = _chunk_iotas()
    lower_ones = jnp.where(row >= col, 1.0, 0.0).astype(BF16)
    eye = row == col
    row_w, lane = _chunk_iotas(HEAD_DIM)
    qk_scale = GLA_DK ** -0.5

    levels = []
    for level in range(CHUNK.bit_length() - 1):
        ref_row = ((row >> (level + 1)) << (level + 1)) + (1 << level)
        pick = jnp.where(col == ref_row, 1.0, 0.0).astype(BF16)
        upper_w = ((row_w >> level) & 1) == 1
        levels.append((pick, _lower_left_blocks(row, col, level), upper_w))

    n_pairs = N_HEADS * GLA_DK // HEAD_DIM
    heads_per_pair = HEAD_DIM // GLA_DK

    def prepare(step, _):
        cums = []
        for sub in range(GLA_CHUNKS_PER_STEP):
            c = step * GLA_CHUNKS_PER_STEP + sub
            r0 = pl.multiple_of(c * CHUNK, CHUNK)
            sm = sm_ref[pl.ds(r0, CHUNK), :].astype(BF16)
            gate = _dot(sm, w2_ref[...]) + gb_ref[...]
            log_a = (gate - _softplus(gate)) * (1.0 / GLA_TAU)
            cums.append((c, r0, _dot_sel(lower_ones, log_a)))
        probs = []
        for c, r0, b_all in cums:
            for pair in range(n_pairs):
                ps = slice(pair * HEAD_DIM, (pair + 1) * HEAD_DIM)
                b = b_all[:, ps]
                q2 = qk_ref[pl.ds(r0, CHUNK), ps].astype(F32) * qk_scale
                k2 = qk_ref[pl.ds(r0, CHUNK),
                            GROUP // 2 + pair * HEAD_DIM:GROUP // 2 + (pair + 1) * HEAD_DIM].astype(F32)
                probs.append(dict(c=c, r0=r0, pair=pair, ps=ps, b=b, q2=q2, k2=k2,
                                  b_refs=[_dot_sel(pick, b) for pick, _, _ in levels]))
        for p in probs:
            b, q2, k2 = p["b"], p["q2"], p["k2"]
            p["q_lv"] = [jnp.where(upper_w, q2 * jnp.exp(jnp.where(upper_w, b - b_ref, 0.0)), 0.0)
                         for (_, _, upper_w), b_ref in zip(levels, p["b_refs"])]
            p["k_lv"] = [jnp.where(upper_w, 0.0, k2 * jnp.exp(jnp.where(upper_w, 0.0, b_ref - b))).astype(BF16)
                         for (_, _, upper_w), b_ref in zip(levels, p["b_refs"])]
        for p in probs:
            r0, b, q2, k2 = p["r0"], p["b"], p["q2"], p["k2"]
            b_last = b[CHUNK - 1:CHUNK, :]
            q_in = (q2 * jnp.exp(b)).astype(BF16)
            k_out = (k2 * jnp.exp(b_last - b)).astype(BF16)
            diag_qk = q2 * k2
            for r in range(heads_per_pair):
                h = p["pair"] * heads_per_pair + r
                hs = slice(h * HEAD_DIM, (h + 1) * HEAD_DIM)
                mine = (lane >= r * GLA_DK) & (lane < (r + 1) * GLA_DK)
                att = jnp.where(eye, jnp.sum(jnp.where(mine, diag_qk, 0.0), axis=1, keepdims=True), 0.0)
                for (_, pair_mask, _), ql, kl in zip(levels, p["q_lv"], p["k_lv"]):
                    s = _dot_nt(jnp.where(mine, ql, 0.0).astype(BF16), kl)
                    att = att + jnp.where(pair_mask, s, 0.0)
                att_ref[pl.ds(r0, CHUNK), h * HEAD_DIM:h * HEAD_DIM + CHUNK] = att.astype(BF16)
                qin_ref[pl.ds(r0, CHUNK), hs] = jnp.where(mine, q_in, jnp.zeros_like(q_in))
                kout_ref[pl.ds(r0, CHUNK), hs] = jnp.where(mine, k_out, jnp.zeros_like(k_out))
            el_ref[pl.ds(pl.multiple_of(p["c"] * SUBLANES, SUBLANES), SUBLANES), p["ps"]] = jnp.broadcast_to(
                jnp.exp(b_last), (SUBLANES, HEAD_DIM))
        return 0

    lax.fori_loop(0, ts // (CHUNK * GLA_CHUNKS_PER_STEP), prepare, 0)

    def scan(step, _):
        heads = [slice(h * HEAD_DIM, (h + 1) * HEAD_DIM) for h in range(N_HEADS)]
        chunks = []
        for sub in range(GLA_SCAN_CHUNKS_PER_STEP):
            c = step * GLA_SCAN_CHUNKS_PER_STEP + sub
            r0 = pl.multiple_of(c * CHUNK, CHUNK)
            values = [v_ref[pl.ds(r0, CHUNK), hs] for hs in heads]
            intra = [_dot(att_ref[pl.ds(r0, CHUNK), h * HEAD_DIM:h * HEAD_DIM + CHUNK], v)
                     for h, v in enumerate(values)]
            upd = [_dot_tn(v, kout_ref[pl.ds(r0, CHUNK), hs]) for hs, v in zip(heads, values)]
            chunks.append((c, r0, intra, upd))
        states = [st_ref[h] for h in range(N_HEADS)]
        for c, r0, intra, upd in chunks:
            inter = [_dot_nt(qin_ref[pl.ds(r0, CHUNK), hs], s.astype(BF16)) for hs, s in zip(heads, states)]
            for h, hs in enumerate(heads):
                ps = slice((h // heads_per_pair) * HEAD_DIM, (h // heads_per_pair + 1) * HEAD_DIM)
                e_last = el_ref[pl.ds(pl.multiple_of(c * SUBLANES, SUBLANES), SUBLANES), ps][0:1, :]
                states[h] = states[h] * e_last + upd[h]
                og = og_ref[pl.ds(r0, CHUNK), hs].astype(F32)
                o_ref[pl.ds(r0, CHUNK), hs] = (
                    _rms(inter[h] + intra[h], gain_ref[...]) * _silu(og)).astype(o_ref.dtype)
        for h in range(N_HEADS):
            st_ref[h] = states[h]
        return 0

    lax.fori_loop(0, ts // (CHUNK * GLA_SCAN_CHUNKS_PER_STEP), scan, 0)


def _gla(proj, small, w2_padded, gate_bias, out_gain, batch, seq, *, ts):
    nt = seq // ts
    kern = functools.partial(_gla_kernel, ts=ts)
    qk_blk = COL_GL // GROUP
    return pl.pallas_call(
        kern,
        grid=(batch, nt),
        in_specs=[
            pl.BlockSpec((ts, GROUP), lambda b, t: (b * nt + t, qk_blk)),
            pl.BlockSpec((ts, GROUP), lambda b, t: (b * nt + t, qk_blk + 1)),
            pl.BlockSpec((ts, GROUP), lambda b, t: (b * nt + t, qk_blk + 2)),
            pl.BlockSpec((ts, SMALL_W), lambda b, t: (b * nt + t, 0)),
            pl.BlockSpec(w2_padded.shape, lambda b, t: (0, 0)),
            pl.BlockSpec((1, N_HEADS * GLA_DK), lambda b, t: (0, 0)),
            pl.BlockSpec((1, HEAD_DIM), lambda b, t: (0, 0)),
        ],
        out_specs=pl.BlockSpec((ts, GROUP), lambda b, t: (b * nt + t, 0)),
        out_shape=jax.ShapeDtypeStruct((batch * seq, GROUP), BF16),
        scratch_shapes=[
            pltpu.VMEM((ts, GROUP), BF16),
            pltpu.VMEM((ts, GROUP), BF16),
            pltpu.VMEM((ts, GROUP), BF16),
            pltpu.VMEM((ts // CHUNK * SUBLANES, N_HEADS * GLA_DK), F32),
            pltpu.VMEM((N_HEADS, HEAD_DIM, HEAD_DIM), F32),
        ],
        compiler_params=_cparams("parallel", "arbitrary"),
        name="gla",
    )(proj, proj, proj, small, w2_padded, gate_bias.reshape(1, -1), out_gain.reshape(1, HEAD_DIM))


def _out_proj_kernel(x_ref, m0_ref, m1_ref, m2_ref, m3_ref, w_ref, o_ref):
    acc = x_ref[...]
    for g, m_ref in enumerate((m0_ref, m1_ref, m2_ref, m3_ref)):
        acc = acc + _dot(m_ref[...], w_ref[g * GROUP:(g + 1) * GROUP, :])
    o_ref[...] = acc


def _out_proj(x, mixes, w_out, layer, *, tm):
    t, d = x.shape
    mix_spec = pl.BlockSpec((tm, GROUP), lambda i: (i, 0))
    return pl.pallas_call(
        _out_proj_kernel,
        grid=(t // tm,),
        in_specs=[pl.BlockSpec((tm, d), lambda i: (i, 0)), mix_spec, mix_spec, mix_spec, mix_spec,
                  pl.BlockSpec((None,) + w_out.shape[1:], lambda i: (layer, 0, 0))],
        out_specs=pl.BlockSpec((tm, d), lambda i: (i, 0)),
        out_shape=jax.ShapeDtypeStruct((t, d), F32),
        compiler_params=_cparams("parallel"),
        name="out_proj_residual",
    )(x, *mixes, w_out)


def _ffn_kernel(x_ref, g_ref, wg_ref, wu_ref, wd_ref, og_ref, o_ref, h_ref, *, norm_output):
    f = pl.program_id(1)

    def swiglu_tile(h):
        hid = _silu(_dot(h, wg_ref[...])) * _dot(h, wu_ref[...])
        return _dot(hid.astype(BF16), wd_ref[...])

    @pl.when(f == 0)
    def _():
        x = x_ref[...]
        h = _rms(x, g_ref[...]).astype(BF16)
        h_ref[...] = h
        o_ref[...] = x + swiglu_tile(h)

    @pl.when(f > 0)
    def _():
        o_ref[...] += swiglu_tile(h_ref[...])

    if norm_output:
        @pl.when(f == pl.num_programs(1) - 1)
        def _():
            o_ref[...] = _rms(o_ref[...], og_ref[...])


def _ffn(x, gain, w_gate, w_up, w_down, layer, out_gain, *, norm_output, tm, tf):
    t, d = x.shape
    hidden = w_gate.shape[2]
    return pl.pallas_call(
        functools.partial(_ffn_kernel, norm_output=norm_output),
        grid=(t // tm, hidden // tf),
        in_specs=[
            pl.BlockSpec((tm, d), lambda i, f: (i, 0)),
            pl.BlockSpec((1, d), lambda i, f: (0, 0)),
            pl.BlockSpec((None, d, tf), lambda i, f: (layer, 0, f)),
            pl.BlockSpec((None, d, tf), lambda i, f: (layer, 0, f)),
            pl.BlockSpec((None, tf, d), lambda i, f: (layer, f, 0)),
            pl.BlockSpec((1, d), lambda i, f: (0, 0)),
        ],
        out_specs=pl.BlockSpec((tm, d), lambda i, f: (i, 0)),
        out_shape=jax.ShapeDtypeStruct((t, d), F32),
        scratch_shapes=[pltpu.VMEM((tm, d), BF16)],
        compiler_params=_cparams("parallel", "arbitrary"),
        name="ffn_residual",
    )(x, gain.reshape(1, d), w_gate, w_up, w_down, out_gain.reshape(1, d))


def _pack_w_in(w_in):
    gd_small = COL_GD + 4 * GROUP
    df0 = gd_small + 2 * N_HEADS
    gl0 = df0 + 3 * GROUP
    rank0 = gl0 + 3 * GROUP
    pad = jnp.zeros(w_in.shape[:-1] + (IN_COLS_PACKED - COL_SMALL - 2 * N_HEADS - GLA_RANK,), BF16)
    parts = [w_in[..., :gd_small], w_in[..., df0:rank0], w_in[..., gd_small:df0], w_in[..., rank0:rank0 + GLA_RANK]]
    return jnp.concatenate([part.astype(BF16) for part in parts] + [pad], axis=-1)


def _mixer_layer(x, l, p, big, batch, seq):
    lam_init = 0.8 - 0.6 * math.exp(-0.3 * l)
    proj, small = _norm_in_proj(x, p["attn_norm"], big["w_in"], l, tm=IN_PROJ_TILE[0], tn=IN_PROJ_TILE[1])
    o_sb = _sb_attention(proj, batch, seq)
    head_params = jnp.zeros((SUBLANES, SMALL_W), F32)
    head_params = head_params.at[0, N_HEADS:2 * N_HEADS].set(p["gdn_a_log"])
    head_params = head_params.at[1, N_HEADS:2 * N_HEADS].set(p["gdn_dt_bias"])
    o_gd = _gdn(proj, small, p["gdn_conv_w"], head_params, p["gdn_out_norm"], batch, seq, ts=SCAN_TILE)
    lam_vecs = jnp.stack([p["diff_lam_q1"], p["diff_lam_k1"], p["diff_lam_q2"], p["diff_lam_k2"]])
    o_df = _diff_attention(proj, lam_vecs, p["diff_out_norm"], lam_init, batch, seq)
    w2 = jnp.zeros((SMALL_W, N_HEADS * GLA_DK), F32).at[2 * N_HEADS:2 * N_HEADS + GLA_RANK].set(p["gla_gate_w2"])
    o_gl = _gla(proj, small, w2.astype(BF16), p["gla_gate_b"], p["gla_out_norm"], batch, seq, ts=SCAN_TILE)
    return _out_proj(x, (o_sb, o_gd, o_df, o_gl), big["w_out"], l, tm=OUT_PROJ_ROWS)


def kernel(x, attn_norm, w_in, gdn_conv_w, gdn_a_log, gdn_dt_bias, gdn_out_norm, diff_lam_q1, diff_lam_k1,
           diff_lam_q2, diff_lam_k2, diff_out_norm, gla_gate_w2, gla_gate_b, gla_out_norm, w_out, ffn_norm,
           w_gate, w_up, w_down, final_norm):
    batch, seq, d = x.shape
    depth = w_in.shape[0]
    tokens = batch * seq
    assert seq % (2 * ATT_BLOCK) == 0 and seq % SCAN_TILE == 0, (seq, ATT_BLOCK, SCAN_TILE)
    assert tokens % IN_PROJ_TILE[0] == 0 and tokens % FFN_TILE[0] == 0 and tokens % OUT_PROJ_ROWS == 0, tokens
    assert w_gate.shape[2] % FFN_TILE[1] == 0, w_gate.shape
    stacked = dict(attn_norm=attn_norm, gdn_conv_w=gdn_conv_w, gdn_a_log=gdn_a_log,
                   gdn_dt_bias=gdn_dt_bias, gdn_out_norm=gdn_out_norm, diff_lam_q1=diff_lam_q1,
                   diff_lam_k1=diff_lam_k1, diff_lam_q2=diff_lam_q2, diff_lam_k2=diff_lam_k2,
                   diff_out_norm=diff_out_norm, gla_gate_w2=gla_gate_w2, gla_gate_b=gla_gate_b,
                   gla_out_norm=gla_out_norm)
    big = dict(w_in=_pack_w_in(w_in), w_out=w_out.astype(BF16), w_gate=w_gate.astype(BF16),
               w_up=w_up.astype(BF16), w_down=w_down.astype(BF16))
    xt = x.reshape(tokens, d)
    for l in range(depth):
        p = {name: arr[l] for name, arr in stacked.items()}
        xt = _mixer_layer(xt, l, p, big, batch, seq)
        xt = _ffn(xt, ffn_norm[l], big["w_gate"], big["w_up"], big["w_down"], l, final_norm,
                  norm_output=(l == depth - 1), tm=FFN_TILE[0], tf=FFN_TILE[1])
    return xt.reshape(batch, seq, d)
```

```python
import functools
import math

import jax
import jax.numpy as jnp
from jax import lax
from jax.experimental import pallas as pl
from jax.experimental.pallas import tpu as pltpu

F32 = jnp.float32
BF16 = jnp.bfloat16

NORM_EPS = 1e-6
LOG2_E = math.log2(math.e)
SB_DEAD_LOG2 = 110.0 * LOG2_E
N_HEADS = 4
HEAD_DIM = 128
GROUP = N_HEADS * HEAD_DIM
CHUNK = 64
CONV_TAPS = 4
GLA_DK = 64
GLA_RANK = 16
GLA_TAU = 16.0
CONV_HALO = 16
SUBLANES = 8

ATT_BLOCK = 256
DIFF_HEADS_PER_STEP = 4
SB_HEADS_PER_STEP = 4
SCAN_TILE = 512
GDN_CONV_CHUNKS_PER_STEP = 4
GDN_CHUNKS_PER_STEP = 4
GDN_SCAN_CHUNKS_PER_STEP = 4
GLA_CHUNKS_PER_STEP = 4
GLA_SCAN_CHUNKS_PER_STEP = 4
IN_PROJ_TILE = (1024, 1792)
OUT_PROJ_ROWS = 512
FFN_TILE = (1024, 512)
VMEM_LIMIT = 56 * 1024 * 1024

COL_SB = 0
COL_GD = 3 * GROUP
COL_DF = 7 * GROUP
COL_GL = 10 * GROUP
COL_SMALL = 13 * GROUP
SMALL_W = 128
IN_COLS_PACKED = 14 * GROUP


def _cparams(*sem):
    return pltpu.CompilerParams(dimension_semantics=sem, vmem_limit_bytes=VMEM_LIMIT)


def _dot(a, b):
    return jnp.dot(a, b, preferred_element_type=F32)


def _dot_nt(a, b):
    return lax.dot_general(a, b, (((1,), (1,)), ((), ())), preferred_element_type=F32)


def _dot_tn(a, b):
    return lax.dot_general(a, b, (((0,), (0,)), ((), ())), preferred_element_type=F32)


def _split(x):
    hi = x.astype(BF16)
    return hi, (x - hi.astype(F32)).astype(BF16)


def _dot_sel(sel16, x):
    hi, lo = _split(x)
    return _dot(sel16, hi) + _dot(sel16, lo)


def _dot_unit_left(unit16, x):
    hi, lo = _split(x)
    return _dot(unit16, hi) + _dot(unit16, lo)


def _dot_unit_right(x, unit16):
    hi, lo = _split(x)
    return _dot(hi, unit16) + _dot(lo, unit16)


def _rms(x, gain):
    return x * lax.rsqrt(jnp.mean(x * x, axis=-1, keepdims=True) + NORM_EPS) * gain


def _silu(x):
    return x / (1.0 + jnp.exp(-x))


def _sigmoid(x):
    return 1.0 / (1.0 + jnp.exp(-x))


def _softplus(x):
    return jnp.maximum(x, 0.0) + jnp.log(1.0 + jnp.exp(-jnp.abs(x)))


def _norm_matmul_kernel(x_ref, g_ref, w_ref, o_ref, small_ref, h_ref):
    j = pl.program_id(1)
    last = pl.num_programs(1) - 1

    @pl.when(j == 0)
    def _():
        h = _rms(x_ref[...], g_ref[...]).astype(BF16)
        h_ref[...] = h
        o_ref[...] = _dot(h, w_ref[...]).astype(o_ref.dtype)

    @pl.when(j > 0)
    def _():
        y = _dot(h_ref[...], w_ref[...])
        o_ref[...] = y.astype(o_ref.dtype)

        @pl.when(j == last)
        def _():
            small_ref[...] = y[:, y.shape[1] - small_ref.shape[1]:]


def _norm_in_proj(x, gain, w, layer, *, tm, tn):
    t, d = x.shape
    return pl.pallas_call(
        _norm_matmul_kernel,
        grid=(t // tm, IN_COLS_PACKED // tn),
        in_specs=[
            pl.BlockSpec((tm, d), lambda i, j: (i, 0)),
            pl.BlockSpec((1, d), lambda i, j: (0, 0)),
            pl.BlockSpec((None, d, tn), lambda i, j: (layer, 0, j)),
        ],
        out_specs=[
            pl.BlockSpec((tm, tn), lambda i, j: (i, j)),
            pl.BlockSpec((tm, IN_COLS_PACKED - COL_SMALL), lambda i, j: (i, 0)),
        ],
        out_shape=[jax.ShapeDtypeStruct((t, IN_COLS_PACKED), BF16),
                   jax.ShapeDtypeStruct((t, IN_COLS_PACKED - COL_SMALL), F32)],
        scratch_shapes=[pltpu.VMEM((tm, d), BF16)],
        compiler_params=_cparams("parallel", "arbitrary"),
        name="norm_in_proj",
    )(x, gain.reshape(1, d), w)


def _sb_kernel(q_ref, k_ref, v_ref, o_ref, *, blk, scale):
    i = pl.program_id(2)
    heads = [slice(h * HEAD_DIM, (h + 1) * HEAD_DIM) for h in range(SB_HEADS_PER_STEP)]
    row = lax.broadcasted_iota(jnp.int32, (blk, blk), 0)
    col = lax.broadcasted_iota(jnp.int32, (blk, blk), 1)
    suffix = jnp.where(row > col, 1.0, 0.0).astype(BF16)
    lead = jnp.maximum(i - 1, 0)
    col_minus_row = (lax.broadcasted_iota(jnp.int32, (blk, 2 * blk), 1)
                     - lax.broadcasted_iota(jnp.int32, (blk, 2 * blk), 0))
    causal = col_minus_row < (i - lead) * blk

    def span(first_block, nblk, state, masked):
        rows = pl.ds(pl.multiple_of(first_block * blk, blk), nblk * blk)
        parts = [slice(b * blk, (b + 1) * blk) for b in range(nblk)]
        zs = [_dot_nt(q_ref[:, hs], k_ref[rows, hs]) * (scale * LOG2_E) for hs in heads]
        sps = []
        for z in zs:
            sp = jnp.maximum(z, 0.0) + jnp.log2(1.0 + jnp.exp2(-jnp.abs(z)))
            sps.append(jnp.where(causal, sp, 0.0) if masked else sp)
        tails = [[_dot_sel_rhs(sp[:, s], suffix) for s in parts] for sp in sps]
        out = []
        for hs, z, sp, tail, (carry, acc) in zip(heads, zs, sps, tails, state):
            sums = [jnp.sum(sp[:, s], axis=1, keepdims=True) for s in parts]
            offsets = [None] * nblk
            for b in reversed(range(nblk)):
                offsets[b] = carry
                carry = carry + sums[b]
            att = jnp.concatenate(
                [jnp.exp2(z[:, s] - sp[:, s] - tail[b] - offsets[b])
                 for b, s in enumerate(parts)], axis=1)
            if masked:
                att = jnp.where(causal, att, 0.0)
            out.append((carry, acc + _dot(att.astype(BF16), v_ref[rows, hs])))
        return tuple(out)

    init = (jnp.zeros((blk, 1), F32), jnp.zeros((blk, HEAD_DIM), F32))
    state = span(lead, 2, (init,) * len(heads), True)

    def alive(state):
        low = state[0][0]
        for carry, _ in state[1:]:
            low = jnp.minimum(low, carry)
        return jnp.min(low) < SB_DEAD_LOG2

    _, state = lax.while_loop(
        lambda carried: jnp.logical_and(carried[0] < lead // 2, alive(carried[1])),
        lambda carried: (carried[0] + 1, span(lead - 2 - 2 * carried[0], 2, carried[1], False)),
        (jnp.int32(0), state))
    odd_left = jnp.where(alive(state), lead % 2, 0)
    state = lax.fori_loop(0, odd_left, lambda n, st: span(0, 1, st, False), state)
    for hs, (_, acc) in zip(heads, state):
        o_ref[:, hs] = acc.astype(o_ref.dtype)


def _dot_sel_rhs(x, sel16):
    hi, lo = _split(x)
    return _dot(hi, sel16) + _dot(lo, sel16)


def _sb_attention(proj, batch, seq):
    blk = ATT_BLOCK
    nq = seq // blk
    width = SB_HEADS_PER_STEP * HEAD_DIM
    qb, kb, vb = COL_SB // width, (COL_SB + GROUP) // width, (COL_SB + 2 * GROUP) // width
    kern = functools.partial(_sb_kernel, blk=blk, scale=HEAD_DIM ** -0.5)
    return pl.pallas_call(
        kern,
        grid=(batch, N_HEADS // SB_HEADS_PER_STEP, nq),
        in_specs=[
            pl.BlockSpec((blk, width), lambda b, h, i: (b * nq + i, qb + h)),
            pl.BlockSpec((seq, width), lambda b, h, i: (b, kb + h)),
            pl.BlockSpec((seq, width), lambda b, h, i: (b, vb + h)),
        ],
        out_specs=pl.BlockSpec((blk, width), lambda b, h, i: (b * nq + i, h)),
        out_shape=jax.ShapeDtypeStruct((batch * seq, GROUP), BF16),
        compiler_params=_cparams("parallel", "parallel", "arbitrary"),
        name="sb_attention",
    )(proj, proj, proj)


def _diff_kernel(lam_ref, q_ref, k_ref, v_ref, g_ref, o_ref, *, blk, scale, lam_init):
    i = pl.program_id(2)
    half = HEAD_DIM // 2
    lane = lax.broadcasted_iota(jnp.int32, (blk, HEAD_DIM), 1)
    heads = [slice(h * HEAD_DIM, (h + 1) * HEAD_DIM) for h in range(DIFF_HEADS_PER_STEP)]
    qs = []
    for hs in heads:
        q = q_ref[:, hs].astype(F32) * scale
        qs.append(jnp.concatenate([jnp.where(lane < half, q, 0.0), jnp.where(lane >= half, q, 0.0)],
                                  axis=0).astype(BF16))
    lead = jnp.maximum(i - 1, 0)
    row = lax.broadcasted_iota(jnp.int32, (2 * blk, 2 * blk), 0) & (blk - 1)
    col = lax.broadcasted_iota(jnp.int32, (2 * blk, 2 * blk), 1)
    causal = col - row <= (i - lead) * blk

    def softmax(s, state, masked):
        m, l_lanes, acc = state
        if masked:
            s = jnp.where(causal, s, -jnp.inf)
        m_new = jnp.maximum(m, jnp.max(s, axis=1, keepdims=True))
        alpha = jnp.exp(m - m_new)
        p = jnp.exp(s - m_new)
        p_lanes = p[:, 0:HEAD_DIM]
        for c in range(1, s.shape[1] // HEAD_DIM):
            p_lanes = p_lanes + p[:, c * HEAD_DIM:(c + 1) * HEAD_DIM]
        return m_new, alpha * l_lanes + p_lanes, alpha * acc, p.astype(BF16)

    def span(first_block, nblk, states, masked):
        rows = pl.ds(pl.multiple_of(first_block * blk, blk), nblk * blk)
        out = []
        s_next = _dot_nt(qs[0], k_ref[rows, heads[0]])
        for h, (hs, state) in enumerate(zip(heads, states)):
            s = s_next
            if h + 1 < len(heads):
                s_next = _dot_nt(qs[h + 1], k_ref[rows, heads[h + 1]])
            m, l_lanes, acc, p = softmax(s, state, masked)
            out.append((m, l_lanes, acc + _dot(p, v_ref[rows, hs])))
        return tuple(out)

    init = (jnp.full((2 * blk, 1), -jnp.inf, F32), jnp.zeros((2 * blk, HEAD_DIM), F32),
            jnp.zeros((2 * blk, HEAD_DIM), F32))
    states = span(lead, 2, (init,) * len(heads), True)
    states = lax.fori_loop(0, lead // 2, lambda n, st: span(2 * n, 2, st, False), states)
    states = lax.fori_loop(0, lead % 2, lambda n, st: span(lead - 1, 1, st, False), states)

    lam_v = lam_ref[...]
    d1 = jnp.sum(lam_v[0:1, :] * lam_v[1:2, :], axis=1, keepdims=True)
    d2 = jnp.sum(lam_v[2:3, :] * lam_v[3:4, :], axis=1, keepdims=True)
    lam = jnp.exp(d1) - jnp.exp(d2) + lam_init
    for hs, (_, l_lanes, acc) in zip(heads, states):
        norm = acc / jnp.sum(l_lanes, axis=1, keepdims=True)
        out = norm[:blk] - lam * norm[blk:]
        o_ref[:, hs] = (_rms(out, g_ref[...]) * (1.0 - lam_init)).astype(o_ref.dtype)


def _diff_attention(proj, lam_vecs, out_gain, lam_init, batch, seq):
    blk = ATT_BLOCK
    nq = seq // blk
    width = DIFF_HEADS_PER_STEP * HEAD_DIM
    qb, kb, vb = COL_DF // width, (COL_DF + GROUP) // width, (COL_DF + 2 * GROUP) // width
    kern = functools.partial(_diff_kernel, blk=blk, scale=(HEAD_DIM // 2) ** -0.5, lam_init=lam_init)
    return pl.pallas_call(
        kern,
        grid=(batch, N_HEADS // DIFF_HEADS_PER_STEP, nq),
        in_specs=[
            pl.BlockSpec(lam_vecs.shape, lambda b, h, i: (0, 0)),
            pl.BlockSpec((blk, width), lambda b, h, i: (b * nq + i, qb + h)),
            pl.BlockSpec((seq, width), lambda b, h, i: (b, kb + h)),
            pl.BlockSpec((seq, width), lambda b, h, i: (b, vb + h)),
            pl.BlockSpec((1, HEAD_DIM), lambda b, h, i: (0, 0)),
        ],
        out_specs=pl.BlockSpec((blk, width), lambda b, h, i: (b * nq + i, h)),
        out_shape=jax.ShapeDtypeStruct((batch * seq, GROUP), BF16),
        compiler_params=_cparams("parallel", "parallel", "arbitrary"),
        name="diff_attention",
    )(lam_vecs, proj, proj, proj, out_gain.reshape(1, HEAD_DIM))


def _chunk_iotas(width=CHUNK):
    row = lax.broadcasted_iota(jnp.int32, (CHUNK, width), 0)
    col = lax.broadcasted_iota(jnp.int32, (CHUNK, width), 1)
    return row, col


def _lower_left_blocks(row, col, level):
    same_pair = (row >> (level + 1)) == (col >> (level + 1))
    return same_pair & (((row >> level) & 1) == 1) & (((col >> level) & 1) == 0)


def _unit_lower_inverses(a_list, row, col):
    eye = jnp.where(row == col, 1.0, 0.0).astype(F32)
    invs = [eye - jnp.where(_lower_left_blocks(row, col, 0), a, 0.0) for a in a_list]
    for level in range(1, CHUNK.bit_length() - 1):
        mask = _lower_left_blocks(row, col, level)
        inv16 = [inv.astype(BF16) for inv in invs]
        left = [_dot_unit_left(i16, jnp.where(mask, a, 0.0)) for i16, a in zip(inv16, a_list)]
        invs = [inv - _dot_unit_right(x, i16) for inv, x, i16 in zip(invs, left, inv16)]
    return invs


def _gdn_kernel(x_ref, z_ref, sm_ref, cw_ref, hp_ref, gain_ref, o_ref,
                xs_ref, qn_ref, kn_ref, vn_ref, u_ref, w_ref, qg_ref, kd_ref, qk_ref, el_ref, st_ref, *, ts):
    t = pl.program_id(1)

    @pl.when(t == 0)
    def _():
        xs_ref[0:CONV_HALO, :] = jnp.zeros((CONV_HALO, 3 * GROUP), BF16)
        st_ref[...] = jnp.zeros_like(st_ref)

    xs_ref[CONV_HALO:CONV_HALO + ts, :] = x_ref[...]

    win = CHUNK + CONV_HALO
    shift_row = lax.broadcasted_iota(jnp.int32, ((CONV_TAPS - 1) * CHUNK, win), 0)
    shift_col = lax.broadcasted_iota(jnp.int32, ((CONV_TAPS - 1) * CHUNK, win), 1)
    tap = shift_row >> (CHUNK.bit_length() - 1)
    shifts = jnp.where(shift_col == (shift_row & (CHUNK - 1)) + CONV_HALO - (CONV_TAPS - 1) + tap,
                       1.0, 0.0).astype(BF16)

    def conv(step, _):
        starts = [pl.multiple_of((step * GDN_CONV_CHUNKS_PER_STEP + sub) * CHUNK, CHUNK)
                  for sub in range(GDN_CONV_CHUNKS_PER_STEP)]
        windows = [xs_ref[pl.ds(r0, win), :] for r0 in starts]
        shifted = [_dot(shifts, window) for window in windows]
        for r0, window, taps in zip(starts, windows, shifted):
            acc = window[CONV_HALO:].astype(F32) * cw_ref[CONV_TAPS - 1:CONV_TAPS, :]
            for j in range(CONV_TAPS - 1):
                acc = acc + taps[j * CHUNK:(j + 1) * CHUNK] * cw_ref[j:j + 1, :]
            y = _silu(acc)
            for part, dst in enumerate((qn_ref, kn_ref, vn_ref)):
                for h in range(N_HEADS):
                    hs = slice(h * HEAD_DIM, (h + 1) * HEAD_DIM)
                    yb = y[:, part * GROUP + h * HEAD_DIM:part * GROUP + (h + 1) * HEAD_DIM]
                    if part < 2:
                        yb = yb * lax.rsqrt(jnp.sum(yb * yb, axis=-1, keepdims=True) + NORM_EPS)
                    if part == 0:
                        yb = yb * (HEAD_DIM ** -0.5)
                    dst[pl.ds(r0, CHUNK), hs] = yb
        return 0

    lax.fori_loop(0, ts // (CHUNK * GDN_CONV_CHUNKS_PER_STEP), conv, 0)
    xs_ref[0:CONV_HALO, :] = xs_ref[ts:ts + CONV_HALO, :]

    row, col = _chunk_iotas()
    strict = row > col
    row_w, col_w = _chunk_iotas(HEAD_DIM)
    incl_w = row_w >= col_w
    eye_w = row_w == col_w
    lower_ones = jnp.where(row >= col, 1.0, 0.0).astype(BF16)
    all_ones = jnp.ones((CHUNK, CHUNK), BF16)

    def prepare(pair, _):
        gates = []
        for sub in range(GDN_CHUNKS_PER_STEP):
            c = pair * GDN_CHUNKS_PER_STEP + sub
            r0 = pl.multiple_of(c * CHUNK, CHUNK)
            sm = sm_ref[pl.ds(r0, CHUNK), :]
            beta_all = _sigmoid(sm)
            g_all = -jnp.exp(hp_ref[0:1, :]) * _softplus(sm + hp_ref[1:2, :])
            g_b = jnp.concatenate(
                [jnp.broadcast_to(g_all[:, N_HEADS + h:N_HEADS + h + 1], (CHUNK, HEAD_DIM))
                 for h in range(N_HEADS)], axis=1)
            gc_all = _dot_sel(lower_ones, g_b)
            gates.append((c, r0, beta_all, gc_all))
        gc_rows = []
        for _, _, _, gc_all in gates:
            gc_diag = jnp.concatenate(
                [jnp.where(eye_w, gc_all[:, h * HEAD_DIM:(h + 1) * HEAD_DIM], 0.0) for h in range(N_HEADS)], axis=1)
            gc_rows.append(_dot_sel(all_ones, gc_diag))

        probs = []
        for (c, r0, beta_all, gc_all), gc_row_all in zip(gates, gc_rows):
            for h in range(N_HEADS):
                hs = slice(h * HEAD_DIM, (h + 1) * HEAD_DIM)
                beta = beta_all[:, h:h + 1]
                gc = gc_all[:, hs]
                decay = jnp.exp(jnp.where(incl_w, gc - gc_row_all[:, hs], -jnp.inf))[:, :CHUNK]
                q = qn_ref[pl.ds(r0, CHUNK), hs]
                k = kn_ref[pl.ds(r0, CHUNK), hs]
                v = vn_ref[pl.ds(r0, CHUNK), hs]
                kb = k * beta
                k16 = k.astype(BF16)
                a_mat = jnp.where(strict, _dot_nt(kb.astype(BF16), k16) * decay, 0.0)
                qk = (_dot_nt(q.astype(BF16), k16) * decay).astype(BF16)
                probs.append(dict(c=c, r0=r0, h=h, hs=hs, gc=gc, q=q, k=k, kb=kb, vb=v * beta, a=a_mat, qk=qk))

        invs = _unit_lower_inverses([p["a"] for p in probs], row, col)
        uws = [_dot_unit_left(inv.astype(BF16), jnp.concatenate([p["vb"], p["kb"] * jnp.exp(p["gc"])], axis=1))
               for inv, p in zip(invs, probs)]
        for p, uw in zip(probs, uws):
            r0, hs, gc, h = p["r0"], p["hs"], p["gc"], p["h"]
            g_last = gc[CHUNK - 1:CHUNK, :]
            u_ref[pl.ds(r0, CHUNK), hs] = uw[:, :HEAD_DIM]
            w_ref[pl.ds(r0, CHUNK), hs] = uw[:, HEAD_DIM:].astype(BF16)
            qg_ref[pl.ds(r0, CHUNK), hs] = (p["q"] * jnp.exp(gc)).astype(BF16)
            kd_ref[pl.ds(r0, CHUNK), hs] = (p["k"] * jnp.exp(g_last - gc)).astype(BF16)
            qk_ref[pl.ds(r0, CHUNK), h * HEAD_DIM:h * HEAD_DIM + CHUNK] = p["qk"]
            el_ref[pl.ds(pl.multiple_of(p["c"] * SUBLANES, SUBLANES), SUBLANES), hs] = jnp.broadcast_to(
                jnp.exp(g_last), (SUBLANES, HEAD_DIM))
        return 0

    lax.fori_loop(0, ts // (CHUNK * GDN_CHUNKS_PER_STEP), prepare, 0)

    def scan(step, _):
        heads = [slice(h * HEAD_DIM, (h + 1) * HEAD_DIM) for h in range(N_HEADS)]
        states = [st_ref[h] for h in range(N_HEADS)]
        for sub in range(GDN_SCAN_CHUNKS_PER_STEP):
            c = step * GDN_SCAN_CHUNKS_PER_STEP + sub
            r0 = pl.multiple_of(c * CHUNK, CHUNK)
            s16 = [s.astype(BF16) for s in states]
            ws = [_dot(w_ref[pl.ds(r0, CHUNK), hs], s) for hs, s in zip(heads, s16)]
            inter = [_dot(qg_ref[pl.ds(r0, CHUNK), hs], s) for hs, s in zip(heads, s16)]
            v_new = [(u_ref[pl.ds(r0, CHUNK), hs] - x).astype(BF16) for hs, x in zip(heads, ws)]
            upd = [_dot_tn(kd_ref[pl.ds(r0, CHUNK), hs], vn) for hs, vn in zip(heads, v_new)]
            for h, hs in enumerate(heads):
                e_last = el_ref[pl.ds(pl.multiple_of(c * SUBLANES, SUBLANES), SUBLANES), hs][0:1, :]
                states[h] = states[h] * e_last + upd[h]
            intra = [_dot(qk_ref[pl.ds(r0, CHUNK), h * HEAD_DIM:h * HEAD_DIM + CHUNK], vn)
                     for h, vn in enumerate(v_new)]
            for h, hs in enumerate(heads):
                zg = z_ref[pl.ds(r0, CHUNK), hs].astype(F32)
                o_ref[pl.ds(r0, CHUNK), hs] = (
                    _rms(inter[h] + intra[h], gain_ref[...]) * _silu(zg)).astype(o_ref.dtype)
        for h in range(N_HEADS):
            st_ref[h] = states[h]
        return 0

    lax.fori_loop(0, ts // (CHUNK * GDN_SCAN_CHUNKS_PER_STEP), scan, 0)


def _gdn(proj, small, conv_w, head_params, out_gain, batch, seq, *, ts):
    nt = seq // ts
    kern = functools.partial(_gdn_kernel, ts=ts)
    qkv_blk = COL_GD // (3 * GROUP)
    z_blk = (COL_GD + 3 * GROUP) // GROUP
    return pl.pallas_call(
        kern,
        grid=(batch, nt),
        in_specs=[
            pl.BlockSpec((ts, 3 * GROUP), lambda b, t: (b * nt + t, qkv_blk)),
            pl.BlockSpec((ts, GROUP), lambda b, t: (b * nt + t, z_blk)),
            pl.BlockSpec((ts, SMALL_W), lambda b, t: (b * nt + t, 0)),
            pl.BlockSpec((CONV_TAPS, 3 * GROUP), lambda b, t: (0, 0)),
            pl.BlockSpec(head_params.shape, lambda b, t: (0, 0)),
            pl.BlockSpec((1, HEAD_DIM), lambda b, t: (0, 0)),
        ],
        out_specs=pl.BlockSpec((ts, GROUP), lambda b, t: (b * nt + t, 0)),
        out_shape=jax.ShapeDtypeStruct((batch * seq, GROUP), BF16),
        scratch_shapes=[
            pltpu.VMEM((ts + CONV_HALO, 3 * GROUP), BF16),
            pltpu.VMEM((ts, GROUP), F32),
            pltpu.VMEM((ts, GROUP), F32),
            pltpu.VMEM((ts, GROUP), F32),
            pltpu.VMEM((ts, GROUP), F32),
            pltpu.VMEM((ts, GROUP), BF16),
            pltpu.VMEM((ts, GROUP), BF16),
            pltpu.VMEM((ts, GROUP), BF16),
            pltpu.VMEM((ts, GROUP), BF16),
            pltpu.VMEM((ts // CHUNK * SUBLANES, GROUP), F32),
            pltpu.VMEM((N_HEADS, HEAD_DIM, HEAD_DIM), F32),
        ],
        compiler_params=_cparams("parallel", "arbitrary"),
        name="gated_deltanet",
    )(proj, proj, small, conv_w, head_params, out_gain.reshape(1, HEAD_DIM))


def _gla_kernel(qk_ref, v_ref, og_ref, sm_ref, w2_ref, gb_ref, gain_ref, o_ref,
                att_ref, qin_ref, kout_ref, el_ref, st_ref, *, ts):
    t = pl.program_id(1)

    @pl.when(t == 0)
    def _():
        st_ref[...] = jnp.zeros_like(st_ref)

    row, col = _chunk_iotas()
    lower_ones = jnp.where(row >= col, 1.0, 0.0).astype(BF16)
    eye = row == col
    row_w, lane = _chunk_iotas(HEAD_DIM)
    qk_scale = GLA_DK ** -0.5

    levels = []
    for level in range(CHUNK.bit_length() - 1):
        ref_row = ((row >> (level + 1)) << (level + 1)) + (1 << level)
        pick = jnp.where(col == ref_row, 1.0, 0.0).astype(BF16)
        upper_w = ((row_w >> level) & 1) == 1
        levels.append((pick, _lower_left_blocks(row, col, level), upper_w))

    n_pairs = N_HEADS * GLA_DK // HEAD_DIM
    heads_per_pair = HEAD_DIM // GLA_DK

    def prepare(step, _):
        cums = []
        for sub in range(GLA_CHUNKS_PER_STEP):
            c = step * GLA_CHUNKS_PER_STEP + sub
            r0 = pl.multiple_of(c * CHUNK, CHUNK)
            sm = sm_ref[pl.ds(r0, CHUNK), :].astype(BF16)
            gate = _dot(sm, w2_ref[...]) + gb_ref[...]
            log_a = (gate - _softplus(gate)) * (1.0 / GLA_TAU)
            cums.append((c, r0, _dot_sel(lower_ones, log_a)))
        probs = []
        for c, r0, b_all in cums:
            for pair in range(n_pairs):
                ps = slice(pair * HEAD_DIM, (pair + 1) * HEAD_DIM)
                b = b_all[:, ps]
                q2 = qk_ref[pl.ds(r0, CHUNK), ps].astype(F32) * qk_scale
                k2 = qk_ref[pl.ds(r0, CHUNK),
                            GROUP // 2 + pair * HEAD_DIM:GROUP // 2 + (pair + 1) * HEAD_DIM].astype(F32)
                probs.append(dict(c=c, r0=r0, pair=pair, ps=ps, b=b, q2=q2, k2=k2,
                                  b_refs=[_dot_sel(pick, b) for pick, _, _ in levels]))
        for p in probs:
            b, q2, k2 = p["b"], p["q2"], p["k2"]
            p["q_lv"] = [jnp.where(upper_w, q2 * jnp.exp(jnp.where(upper_w, b - b_ref, 0.0)), 0.0)
                         for (_, _, upper_w), b_ref in zip(levels, p["b_refs"])]
            p["k_lv"] = [jnp.where(upper_w, 0.0, k2 * jnp.exp(jnp.where(upper_w, 0.0, b_ref - b))).astype(BF16)
                         for (_, _, upper_w), b_ref in zip(levels, p["b_refs"])]
        for p in probs:
            r0, b, q2, k2 = p["r0"], p["b"], p["q2"], p["k2"]
            b_last = b[CHUNK - 1:CHUNK, :]
            q_in = (q2 * jnp.exp(b)).astype(BF16)
            k_out = (k2 * jnp.exp(b_last - b)).astype(BF16)
            diag_qk = q2 * k2
            for r in range(heads_per_pair):
                h = p["pair"] * heads_per_pair + r
                hs = slice(h * HEAD_DIM, (h + 1) * HEAD_DIM)
                mine = (lane >= r * GLA_DK) & (lane < (r + 1) * GLA_DK)
                att = jnp.where(eye, jnp.sum(jnp.where(mine, diag_qk, 0.0), axis=1, keepdims=True), 0.0)
                for (_, pair_mask, _), ql, kl in zip(levels, p["q_lv"], p["k_lv"]):
                    s = _dot_nt(jnp.where(mine, ql, 0.0).astype(BF16), kl)
                    att = att + jnp.where(pair_mask, s, 0.0)
                att_ref[pl.ds(r0, CHUNK), h * HEAD_DIM:h * HEAD_DIM + CHUNK] = att.astype(BF16)
                qin_ref[pl.ds(r0, CHUNK), hs] = jnp.where(mine, q_in, jnp.zeros_like(q_in))
                kout_ref[pl.ds(r0, CHUNK), hs] = jnp.where(mine, k_out, jnp.zeros_like(k_out))
            el_ref[pl.ds(pl.multiple_of(p["c"] * SUBLANES, SUBLANES), SUBLANES), p["ps"]] = jnp.broadcast_to(
                jnp.exp(b_last), (SUBLANES, HEAD_DIM))
        return 0

    lax.fori_loop(0, ts // (CHUNK * GLA_CHUNKS_PER_STEP), prepare, 0)

    def scan(step, _):
        heads = [slice(h * HEAD_DIM, (h + 1) * HEAD_DIM) for h in range(N_HEADS)]
        chunks = []
        for sub in range(GLA_SCAN_CHUNKS_PER_STEP):
            c = step * GLA_SCAN_CHUNKS_PER_STEP + sub
            r0 = pl.multiple_of(c * CHUNK, CHUNK)
            values = [v_ref[pl.ds(r0, CHUNK), hs] for hs in heads]
            intra = [_dot(att_ref[pl.ds(r0, CHUNK), h * HEAD_DIM:h * HEAD_DIM + CHUNK], v)
                     for h, v in enumerate(values)]
            upd = [_dot_tn(v, kout_ref[pl.ds(r0, CHUNK), hs]) for hs, v in zip(heads, values)]
            chunks.append((c, r0, intra, upd))
        states = [st_ref[h] for h in range(N_HEADS)]
        for c, r0, intra, upd in chunks:
            inter = [_dot_nt(qin_ref[pl.ds(r0, CHUNK), hs], s.astype(BF16)) for hs, s in zip(heads, states)]
            for h, hs in enumerate(heads):
                ps = slice((h // heads_per_pair) * HEAD_DIM, (h // heads_per_pair + 1) * HEAD_DIM)
                e_last = el_ref[pl.ds(pl.multiple_of(c * SUBLANES, SUBLANES), SUBLANES), ps][0:1, :]
                states[h] = states[h] * e_last + upd[h]
                og = og_ref[pl.ds(r0, CHUNK), hs].astype(F32)
                o_ref[pl.ds(r0, CHUNK), hs] = (
                    _rms(inter[h] + intra[h], gain_ref[...]) * _silu(og)).astype(o_ref.dtype)
        for h in range(N_HEADS):
            st_ref[h] = states[h]
        return 0

    lax.fori_loop(0, ts // (CHUNK * GLA_SCAN_CHUNKS_PER_STEP), scan, 0)


def _gla(proj, small, w2_padded, gate_bias, out_gain, batch, seq, *, ts):
    nt = seq // ts
    kern = functools.partial(_gla_kernel, ts=ts)
    qk_blk = COL_GL // GROUP
    return pl.pallas_call(
        kern,
        grid=(batch, nt),
        in_specs=[
            pl.BlockSpec((ts, GROUP), lambda b, t: (b * nt + t, qk_blk)),
            pl.BlockSpec((ts, GROUP), lambda b, t: (b * nt + t, qk_blk + 1)),
            pl.BlockSpec((ts, GROUP), lambda b, t: (b * nt + t, qk_blk + 2)),
            pl.BlockSpec((ts, SMALL_W), lambda b, t: (b * nt + t, 0)),
            pl.BlockSpec(w2_padded.shape, lambda b, t: (0, 0)),
            pl.BlockSpec((1, N_HEADS * GLA_DK), lambda b, t: (0, 0)),
            pl.BlockSpec((1, HEAD_DIM), lambda b, t: (0, 0)),
        ],
        out_specs=pl.BlockSpec((ts, GROUP), lambda b, t: (b * nt + t, 0)),
        out_shape=jax.ShapeDtypeStruct((batch * seq, GROUP), BF16),
        scratch_shapes=[
            pltpu.VMEM((ts, GROUP), BF16),
            pltpu.VMEM((ts, GROUP), BF16),
            pltpu.VMEM((ts, GROUP), BF16),
            pltpu.VMEM((ts // CHUNK * SUBLANES, N_HEADS * GLA_DK), F32),
            pltpu.VMEM((N_HEADS, HEAD_DIM, HEAD_DIM), F32),
        ],
        compiler_params=_cparams("parallel", "arbitrary"),
        name="gla",
    )(proj, proj, proj, small, w2_padded, gate_bias.reshape(1, -1), out_gain.reshape(1, HEAD_DIM))


def _out_proj_kernel(x_ref, m0_ref, m1_ref, m2_ref, m3_ref, w_ref, o_ref):
    acc = x_ref[...]
    for g, m_ref in enumerate((m0_ref, m1_ref, m2_ref, m3_ref)):
        acc = acc + _dot(m_ref[...], w_ref[g * GROUP:(g + 1) * GROUP, :])
    o_ref[...] = acc


def _out_proj(x, mixes, w_out, layer, *, tm):
    t, d = x.shape
    mix_spec = pl.BlockSpec((tm, GROUP), lambda i: (i, 0))
    return pl.pallas_call(
        _out_proj_kernel,
        grid=(t // tm,),
        in_specs=[pl.BlockSpec((tm, d), lambda i: (i, 0)), mix_spec, mix_spec, mix_spec, mix_spec,
                  pl.BlockSpec((None,) + w_out.shape[1:], lambda i: (layer, 0, 0))],
        out_specs=pl.BlockSpec((tm, d), lambda i: (i, 0)),
        out_shape=jax.ShapeDtypeStruct((t, d), F32),
        compiler_params=_cparams("parallel"),
        name="out_proj_residual",
    )(x, *mixes, w_out)


def _ffn_kernel(x_ref, g_ref, wg_ref, wu_ref, wd_ref, og_ref, o_ref, h_ref, *, norm_output):
    f = pl.program_id(1)

    def swiglu_tile(h):
        hid = _silu(_dot(h, wg_ref[...])) * _dot(h, wu_ref[...])
        return _dot(hid.astype(BF16), wd_ref[...])

    @pl.when(f == 0)
    def _():
        x = x_ref[...]
        h = _rms(x, g_ref[...]).astype(BF16)
        h_ref[...] = h
        o_ref[...] = x + swiglu_tile(h)

    @pl.when(f > 0)
    def _():
        o_ref[...] += swiglu_tile(h_ref[...])

    if norm_output:
        @pl.when(f == pl.num_programs(1) - 1)
        def _():
            o_ref[...] = _rms(o_ref[...], og_ref[...])


def _ffn(x, gain, w_gate, w_up, w_down, layer, out_gain, *, norm_output, tm, tf):
    t, d = x.shape
    hidden = w_gate.shape[2]
    return pl.pallas_call(
        functools.partial(_ffn_kernel, norm_output=norm_output),
        grid=(t // tm, hidden // tf),
        in_specs=[
            pl.BlockSpec((tm, d), lambda i, f: (i, 0)),
            pl.BlockSpec((1, d), lambda i, f: (0, 0)),
            pl.BlockSpec((None, d, tf), lambda i, f: (layer, 0, f)),
            pl.BlockSpec((None, d, tf), lambda i, f: (layer, 0, f)),
            pl.BlockSpec((None, tf, d), lambda i, f: (layer, f, 0)),
            pl.BlockSpec((1, d), lambda i, f: (0, 0)),
        ],
        out_specs=pl.BlockSpec((tm, d), lambda i, f: (i, 0)),
        out_shape=jax.ShapeDtypeStruct((t, d), F32),
        scratch_shapes=[pltpu.VMEM((tm, d), BF16)],
        compiler_params=_cparams("parallel", "arbitrary"),
        name="ffn_residual",
    )(x, gain.reshape(1, d), w_gate, w_up, w_down, out_gain.reshape(1, d))


def _pack_w_in(w_in):
    gd_small = COL_GD + 4 * GROUP
    df0 = gd_small + 2 * N_HEADS
    gl0 = df0 + 3 * GROUP
    rank0 = gl0 + 3 * GROUP
    pad = jnp.zeros(w_in.shape[:-1] + (IN_COLS_PACKED - COL_SMALL - 2 * N_HEADS - GLA_RANK,), BF16)
    parts = [w_in[..., :gd_small], w_in[..., df0:rank0], w_in[..., gd_small:df0], w_in[..., rank0:rank0 + GLA_RANK]]
    return jnp.concatenate([part.astype(BF16) for part in parts] + [pad], axis=-1)


def _mixer_layer(x, l, p, big, batch, seq):
    lam_init = 0.8 - 0.6 * math.exp(-0.3 * l)
    proj, small = _norm_in_proj(x, p["attn_norm"], big["w_in"], l, tm=IN_PROJ_TILE[0], tn=IN_PROJ_TILE[1])
    o_sb = _sb_attention(proj, batch, seq)
    head_params = jnp.zeros((SUBLANES, SMALL_W), F32)
    head_params = head_params.at[0, N_HEADS:2 * N_HEADS].set(p["gdn_a_log"])
    head_params = head_params.at[1, N_HEADS:2 * N_HEADS].set(p["gdn_dt_bias"])
    o_gd = _gdn(proj, small, p["gdn_conv_w"], head_params, p["gdn_out_norm"], batch, seq, ts=SCAN_TILE)
    lam_vecs = jnp.stack([p["diff_lam_q1"], p["diff_lam_k1"], p["diff_lam_q2"], p["diff_lam_k2"]])
    o_df = _diff_attention(proj, lam_vecs, p["diff_out_norm"], lam_init, batch, seq)
    w2 = jnp.zeros((SMALL_W, N_HEADS * GLA_DK), F32).at[2 * N_HEADS:2 * N_HEADS + GLA_RANK].set(p["gla_gate_w2"])
    o_gl = _gla(proj, small, w2.astype(BF16), p["gla_gate_b"], p["gla_out_norm"], batch, seq, ts=SCAN_TILE)
    return _out_proj(x, (o_sb, o_gd, o_df, o_gl), big["w_out"], l, tm=OUT_PROJ_ROWS)


def kernel(x, attn_norm, w_in, gdn_conv_w, gdn_a_log, gdn_dt_bias, gdn_out_norm, diff_lam_q1, diff_lam_k1,
           diff_lam_q2, diff_lam_k2, diff_out_norm, gla_gate_w2, gla_gate_b, gla_out_norm, w_out, ffn_norm,
           w_gate, w_up, w_down, final_norm):
    batch, seq, d = x.shape
    depth = w_in.shape[0]
    tokens = batch * seq
    assert seq % (2 * ATT_BLOCK) == 0 and seq % SCAN_TILE == 0, (seq, ATT_BLOCK, SCAN_TILE)
    assert tokens % IN_PROJ_TILE[0] == 0 and tokens % FFN_TILE[0] == 0 and tokens % OUT_PROJ_ROWS == 0, tokens
    assert w_gate.shape[2] % FFN_TILE[1] == 0, w_gate.shape
    stacked = dict(attn_norm=attn_norm, gdn_conv_w=gdn_conv_w, gdn_a_log=gdn_a_log,
                   gdn_dt_bias=gdn_dt_bias, gdn_out_norm=gdn_out_norm, diff_lam_q1=diff_lam_q1,
                   diff_lam_k1=diff_lam_k1, diff_lam_q2=diff_lam_q2, diff_lam_k2=diff_lam_k2,
                   diff_out_norm=diff_out_norm, gla_gate_w2=gla_gate_w2, gla_gate_b=gla_gate_b,
                   gla_out_norm=gla_out_norm)
    big = dict(w_in=_pack_w_in(w_in), w_out=w_out.astype(BF16), w_gate=w_gate.astype(BF16),
               w_up=w_up.astype(BF16), w_down=w_down.astype(BF16))
    xt = x.reshape(tokens, d)
    for l in range(depth):
        p = {name: arr[l] for name, arr in stacked.items()}
        xt = _mixer_layer(xt, l, p, big, batch, seq)
        xt = _ffn(xt, ffn_norm[l], big["w_gate"], big["w_up"], big["w_down"], l, final_norm,
                  norm_output=(l == depth - 1), tm=FFN_TILE[0], tf=FFN_TILE[1])
    return xt.reshape(batch, seq, d)
```

```python
import functools
import math

import jax
import jax.numpy as jnp
from jax import lax
from jax.experimental import pallas as pl
from jax.experimental.pallas import tpu as pltpu

F32 = jnp.float32
BF16 = jnp.bfloat16

NORM_EPS = 1e-6
LOG2_E = math.log2(math.e)
SB_DEAD_LOG2 = 110.0 * LOG2_E
N_HEADS = 4
HEAD_DIM = 128
GROUP = N_HEADS * HEAD_DIM
CHUNK = 64
CONV_TAPS = 4
GLA_DK = 64
GLA_RANK = 16
GLA_TAU = 16.0
CONV_HALO = 16
SUBLANES = 8

ATT_BLOCK = 256
DIFF_HEADS_PER_STEP = 4
SB_HEADS_PER_STEP = 4
SCAN_TILE = 1024
GDN_CONV_CHUNKS_PER_STEP = 4
GDN_CHUNKS_PER_STEP = 4
GDN_SCAN_CHUNKS_PER_STEP = 4
GLA_CHUNKS_PER_STEP = 4
GLA_SCAN_CHUNKS_PER_STEP = 4
IN_PROJ_TILE = (1024, 1792)
OUT_PROJ_ROWS = 512
FFN_TILE = (1024, 512)
VMEM_LIMIT = 56 * 1024 * 1024

COL_SB = 0
COL_GD = 3 * GROUP
COL_DF = 7 * GROUP
COL_GL = 10 * GROUP
COL_SMALL = 13 * GROUP
SMALL_W = 128
IN_COLS_PACKED = 14 * GROUP


def _cparams(*sem):
    return pltpu.CompilerParams(dimension_semantics=sem, vmem_limit_bytes=VMEM_LIMIT)


def _dot(a, b):
    return jnp.dot(a, b, preferred_element_type=F32)


def _dot_nt(a, b):
    return lax.dot_general(a, b, (((1,), (1,)), ((), ())), preferred_element_type=F32)


def _dot_tn(a, b):
    return lax.dot_general(a, b, (((0,), (0,)), ((), ())), preferred_element_type=F32)


def _split(x):
    hi = x.astype(BF16)
    return hi, (x - hi.astype(F32)).astype(BF16)


def _dot_sel(sel16, x):
    hi, lo = _split(x)
    return _dot(sel16, hi) + _dot(sel16, lo)


def _dot_unit_left(unit16, x):
    hi, lo = _split(x)
    return _dot(unit16, hi) + _dot(unit16, lo)


def _dot_unit_right(x, unit16):
    hi, lo = _split(x)
    return _dot(hi, unit16) + _dot(lo, unit16)


def _rms(x, gain):
    return x * lax.rsqrt(jnp.mean(x * x, axis=-1, keepdims=True) + NORM_EPS) * gain


def _silu(x):
    return x / (1.0 + jnp.exp(-x))


def _sigmoid(x):
    return 1.0 / (1.0 + jnp.exp(-x))


def _softplus(x):
    return jnp.maximum(x, 0.0) + jnp.log(1.0 + jnp.exp(-jnp.abs(x)))


def _norm_matmul_kernel(x_ref, g_ref, w_ref, o_ref, small_ref, h_ref):
    j = pl.program_id(1)
    last = pl.num_programs(1) - 1

    @pl.when(j == 0)
    def _():
        h = _rms(x_ref[...], g_ref[...]).astype(BF16)
        h_ref[...] = h
        o_ref[...] = _dot(h, w_ref[...]).astype(o_ref.dtype)

    @pl.when(j > 0)
    def _():
        y = _dot(h_ref[...], w_ref[...])
        o_ref[...] = y.astype(o_ref.dtype)

        @pl.when(j == last)
        def _():
            small_ref[...] = y[:, y.shape[1] - small_ref.shape[1]:]


def _norm_in_proj(x, gain, w, layer, *, tm, tn):
    t, d = x.shape
    return pl.pallas_call(
        _norm_matmul_kernel,
        grid=(t // tm, IN_COLS_PACKED // tn),
        in_specs=[
            pl.BlockSpec((tm, d), lambda i, j: (i, 0)),
            pl.BlockSpec((1, d), lambda i, j: (0, 0)),
            pl.BlockSpec((None, d, tn), lambda i, j: (layer, 0, j)),
        ],
        out_specs=[
            pl.BlockSpec((tm, tn), lambda i, j: (i, j)),
            pl.BlockSpec((tm, IN_COLS_PACKED - COL_SMALL), lambda i, j: (i, 0)),
        ],
        out_shape=[jax.ShapeDtypeStruct((t, IN_COLS_PACKED), BF16),
                   jax.ShapeDtypeStruct((t, IN_COLS_PACKED - COL_SMALL), F32)],
        scratch_shapes=[pltpu.VMEM((tm, d), BF16)],
        compiler_params=_cparams("parallel", "arbitrary"),
        name="norm_in_proj",
    )(x, gain.reshape(1, d), w)


def _sb_kernel(q_ref, k_ref, v_ref, o_ref, *, blk, scale):
    i = pl.program_id(2)
    heads = [slice(h * HEAD_DIM, (h + 1) * HEAD_DIM) for h in range(SB_HEADS_PER_STEP)]
    row = lax.broadcasted_iota(jnp.int32, (blk, blk), 0)
    col = lax.broadcasted_iota(jnp.int32, (blk, blk), 1)
    suffix = jnp.where(row > col, 1.0, 0.0).astype(BF16)
    lead = jnp.maximum(i - 1, 0)
    col_minus_row = (lax.broadcasted_iota(jnp.int32, (blk, 2 * blk), 1)
                     - lax.broadcasted_iota(jnp.int32, (blk, 2 * blk), 0))
    causal = col_minus_row < (i - lead) * blk

    def span(first_block, nblk, state, masked):
        rows = pl.ds(pl.multiple_of(first_block * blk, blk), nblk * blk)
        parts = [slice(b * blk, (b + 1) * blk) for b in range(nblk)]
        zs = [_dot_nt(q_ref[:, hs], k_ref[rows, hs]) * (scale * LOG2_E) for hs in heads]
        sps = []
        for z in zs:
            sp = jnp.maximum(z, 0.0) + jnp.log2(1.0 + jnp.exp2(-jnp.abs(z)))
            sps.append(jnp.where(causal, sp, 0.0) if masked else sp)
        tails = [[_dot_sel_rhs(sp[:, s], suffix) for s in parts] for sp in sps]
        out = []
        for hs, z, sp, tail, (carry, acc) in zip(heads, zs, sps, tails, state):
            sums = [jnp.sum(sp[:, s], axis=1, keepdims=True) for s in parts]
            offsets = [None] * nblk
            for b in reversed(range(nblk)):
                offsets[b] = carry
                carry = carry + sums[b]
            att = jnp.concatenate(
                [jnp.exp2(z[:, s] - sp[:, s] - tail[b] - offsets[b])
                 for b, s in enumerate(parts)], axis=1)
            if masked:
                att = jnp.where(causal, att, 0.0)
            out.append((carry, acc + _dot(att.astype(BF16), v_ref[rows, hs])))
        return tuple(out)

    init = (jnp.zeros((blk, 1), F32), jnp.zeros((blk, HEAD_DIM), F32))
    state = span(lead, 2, (init,) * len(heads), True)

    def alive(state):
        low = state[0][0]
        for carry, _ in state[1:]:
            low = jnp.minimum(low, carry)
        return jnp.min(low) < SB_DEAD_LOG2

    _, state = lax.while_loop(
        lambda carried: jnp.logical_and(carried[0] < lead // 2, alive(carried[1])),
        lambda carried: (carried[0] + 1, span(lead - 2 - 2 * carried[0], 2, carried[1], False)),
        (jnp.int32(0), state))
    odd_left = jnp.where(alive(state), lead % 2, 0)
    state = lax.fori_loop(0, odd_left, lambda n, st: span(0, 1, st, False), state)
    for hs, (_, acc) in zip(heads, state):
        o_ref[:, hs] = acc.astype(o_ref.dtype)


def _dot_sel_rhs(x, sel16):
    hi, lo = _split(x)
    return _dot(hi, sel16) + _dot(lo, sel16)


def _sb_attention(proj, batch, seq):
    blk = ATT_BLOCK
    nq = seq // blk
    width = SB_HEADS_PER_STEP * HEAD_DIM
    qb, kb, vb = COL_SB // width, (COL_SB + GROUP) // width, (COL_SB + 2 * GROUP) // width
    kern = functools.partial(_sb_kernel, blk=blk, scale=HEAD_DIM ** -0.5)
    return pl.pallas_call(
        kern,
        grid=(batch, N_HEADS // SB_HEADS_PER_STEP, nq),
        in_specs=[
            pl.BlockSpec((blk, width), lambda b, h, i: (b * nq + i, qb + h)),
            pl.BlockSpec((seq, width), lambda b, h, i: (b, kb + h)),
            pl.BlockSpec((seq, width), lambda b, h, i: (b, vb + h)),
        ],
        out_specs=pl.BlockSpec((blk, width), lambda b, h, i: (b * nq + i, h)),
        out_shape=jax.ShapeDtypeStruct((batch * seq, GROUP), BF16),
        compiler_params=_cparams("parallel", "parallel", "arbitrary"),
        name="sb_attention",
    )(proj, proj, proj)


def _diff_kernel(lam_ref, q_ref, k_ref, v_ref, g_ref, o_ref, *, blk, scale, lam_init):
    i = pl.program_id(2)
    half = HEAD_DIM // 2
    lane = lax.broadcasted_iota(jnp.int32, (blk, HEAD_DIM), 1)
    heads = [slice(h * HEAD_DIM, (h + 1) * HEAD_DIM) for h in range(DIFF_HEADS_PER_STEP)]
    qs = []
    for hs in heads:
        q = q_ref[:, hs].astype(F32) * scale
        qs.append(jnp.concatenate([jnp.where(lane < half, q, 0.0), jnp.where(lane >= half, q, 0.0)],
                                  axis=0).astype(BF16))
    lead = jnp.maximum(i - 1, 0)
    row = lax.broadcasted_iota(jnp.int32, (2 * blk, 2 * blk), 0) & (blk - 1)
    col = lax.broadcasted_iota(jnp.int32, (2 * blk, 2 * blk), 1)
    causal = col - row <= (i - lead) * blk

    def softmax(s, state, masked):
        m, l_lanes, acc = state
        if masked:
            s = jnp.where(causal, s, -jnp.inf)
        m_new = jnp.maximum(m, jnp.max(s, axis=1, keepdims=True))
        alpha = jnp.exp(m - m_new)
        p = jnp.exp(s - m_new)
        p_lanes = p[:, 0:HEAD_DIM]
        for c in range(1, s.shape[1] // HEAD_DIM):
            p_lanes = p_lanes + p[:, c * HEAD_DIM:(c + 1) * HEAD_DIM]
        return m_new, alpha * l_lanes + p_lanes, alpha * acc, p.astype(BF16)

    def span(first_block, nblk, states, masked):
        rows = pl.ds(pl.multiple_of(first_block * blk, blk), nblk * blk)
        out = []
        s_next = _dot_nt(qs[0], k_ref[rows, heads[0]])
        for h, (hs, state) in enumerate(zip(heads, states)):
            s = s_next
            if h + 1 < len(heads):
                s_next = _dot_nt(qs[h + 1], k_ref[rows, heads[h + 1]])
            m, l_lanes, acc, p = softmax(s, state, masked)
            out.append((m, l_lanes, acc + _dot(p, v_ref[rows, hs])))
        return tuple(out)

    init = (jnp.full((2 * blk, 1), -jnp.inf, F32), jnp.zeros((2 * blk, HEAD_DIM), F32),
            jnp.zeros((2 * blk, HEAD_DIM), F32))
    states = span(lead, 2, (init,) * len(heads), True)
    states = lax.fori_loop(0, lead // 2, lambda n, st: span(2 * n, 2, st, False), states)
    states = lax.fori_loop(0, lead % 2, lambda n, st: span(lead - 1, 1, st, False), states)

    lam_v = lam_ref[...]
    d1 = jnp.sum(lam_v[0:1, :] * lam_v[1:2, :], axis=1, keepdims=True)
    d2 = jnp.sum(lam_v[2:3, :] * lam_v[3:4, :], axis=1, keepdims=True)
    lam = jnp.exp(d1) - jnp.exp(d2) + lam_init
    for hs, (_, l_lanes, acc) in zip(heads, states):
        norm = acc / jnp.sum(l_lanes, axis=1, keepdims=True)
        out = norm[:blk] - lam * norm[blk:]
        o_ref[:, hs] = (_rms(out, g_ref[...]) * (1.0 - lam_init)).astype(o_ref.dtype)


def _diff_attention(proj, lam_vecs, out_gain, lam_init, batch, seq):
    blk = ATT_BLOCK
    nq = seq // blk
    width = DIFF_HEADS_PER_STEP * HEAD_DIM
    qb, kb, vb = COL_DF // width, (COL_DF + GROUP) // width, (COL_DF + 2 * GROUP) // width
    kern = functools.partial(_diff_kernel, blk=blk, scale=(HEAD_DIM // 2) ** -0.5, lam_init=lam_init)
    return pl.pallas_call(
        kern,
        grid=(batch, N_HEADS // DIFF_HEADS_PER_STEP, nq),
        in_specs=[
            pl.BlockSpec(lam_vecs.shape, lambda b, h, i: (0, 0)),
            pl.BlockSpec((blk, width), lambda b, h, i: (b * nq + i, qb + h)),
            pl.BlockSpec((seq, width), lambda b, h, i: (b, kb + h)),
            pl.BlockSpec((seq, width), lambda b, h, i: (b, vb + h)),
            pl.BlockSpec((1, HEAD_DIM), lambda b, h, i: (0, 0)),
        ],
        out_specs=pl.BlockSpec((blk, width), lambda b, h, i: (b * nq + i, h)),
        out_shape=jax.ShapeDtypeStruct((batch * seq, GROUP), BF16),
        compiler_params=_cparams("parallel", "parallel", "arbitrary"),
        name="diff_attention",
    )(lam_vecs, proj, proj, proj, out_gain.reshape(1, HEAD_DIM))


def _chunk_iotas(width=CHUNK):
    row = lax.broadcasted_iota(jnp.int32, (CHUNK, width), 0)
    col = lax.broadcasted_iota(jnp.int32, (CHUNK, width), 1)
    return row, col


def _lower_left_blocks(row, col, level):
    same_pair = (row >> (level + 1)) == (col >> (level + 1))
    return same_pair & (((row >> level) & 1) == 1) & (((col >> level) & 1) == 0)


def _unit_lower_inverses(a_list, row, col):
    eye = jnp.where(row == col, 1.0, 0.0).astype(F32)
    invs = [eye - jnp.where(_lower_left_blocks(row, col, 0), a, 0.0) for a in a_list]
    for level in range(1, CHUNK.bit_length() - 1):
        mask = _lower_left_blocks(row, col, level)
        inv16 = [inv.astype(BF16) for inv in invs]
        left = [_dot_unit_left(i16, jnp.where(mask, a, 0.0)) for i16, a in zip(inv16, a_list)]
        invs = [inv - _dot_unit_right(x, i16) for inv, x, i16 in zip(invs, left, inv16)]
    return invs


def _gdn_kernel(x_ref, z_ref, sm_ref, cw_ref, hp_ref, gain_ref, o_ref,
                xs_ref, qn_ref, kn_ref, vn_ref, u_ref, w_ref, qg_ref, kd_ref, qk_ref, el_ref, st_ref, *, ts):
    t = pl.program_id(1)

    @pl.when(t == 0)
    def _():
        xs_ref[0:CONV_HALO, :] = jnp.zeros((CONV_HALO, 3 * GROUP), BF16)
        st_ref[...] = jnp.zeros_like(st_ref)

    xs_ref[CONV_HALO:CONV_HALO + ts, :] = x_ref[...]

    win = CHUNK + CONV_HALO
    shift_row = lax.broadcasted_iota(jnp.int32, ((CONV_TAPS - 1) * CHUNK, win), 0)
    shift_col = lax.broadcasted_iota(jnp.int32, ((CONV_TAPS - 1) * CHUNK, win), 1)
    tap = shift_row >> (CHUNK.bit_length() - 1)
    shifts = jnp.where(shift_col == (shift_row & (CHUNK - 1)) + CONV_HALO - (CONV_TAPS - 1) + tap,
                       1.0, 0.0).astype(BF16)

    def conv(step, _):
        starts = [pl.multiple_of((step * GDN_CONV_CHUNKS_PER_STEP + sub) * CHUNK, CHUNK)
                  for sub in range(GDN_CONV_CHUNKS_PER_STEP)]
        windows = [xs_ref[pl.ds(r0, win), :] for r0 in starts]
        shifted = [_dot(shifts, window) for window in windows]
        for r0, window, taps in zip(starts, windows, shifted):
            acc = window[CONV_HALO:].astype(F32) * cw_ref[CONV_TAPS - 1:CONV_TAPS, :]
            for j in range(CONV_TAPS - 1):
                acc = acc + taps[j * CHUNK:(j + 1) * CHUNK] * cw_ref[j:j + 1, :]
            y = _silu(acc)
            for part, dst in enumerate((qn_ref, kn_ref, vn_ref)):
                for h in range(N_HEADS):
                    hs = slice(h * HEAD_DIM, (h + 1) * HEAD_DIM)
                    yb = y[:, part * GROUP + h * HEAD_DIM:part * GROUP + (h + 1) * HEAD_DIM]
                    if part < 2:
                        yb = yb * lax.rsqrt(jnp.sum(yb * yb, axis=-1, keepdims=True) + NORM_EPS)
                    if part == 0:
                        yb = yb * (HEAD_DIM ** -0.5)
                    dst[pl.ds(r0, CHUNK), hs] = yb
        return 0

    lax.fori_loop(0, ts // (CHUNK * GDN_CONV_CHUNKS_PER_STEP), conv, 0)
    xs_ref[0:CONV_HALO, :] = xs_ref[ts:ts + CONV_HALO, :]

    row, col = _chunk_iotas()
    strict = row > col
    row_w, col_w = _chunk_iotas(HEAD_DIM)
    incl_w = row_w >= col_w
    eye_w = row_w == col_w
    lower_ones = jnp.where(row >= col, 1.0, 0.0).astype(BF16)
    all_ones = jnp.ones((CHUNK, CHUNK), BF16)

    def prepare(pair, _):
        gates = []
        for sub in range(GDN_CHUNKS_PER_STEP):
            c = pair * GDN_CHUNKS_PER_STEP + sub
            r0 = pl.multiple_of(c * CHUNK, CHUNK)
            sm = sm_ref[pl.ds(r0, CHUNK), :]
            beta_all = _sigmoid(sm)
            g_all = -jnp.exp(hp_ref[0:1, :]) * _softplus(sm + hp_ref[1:2, :])
            g_b = jnp.concatenate(
                [jnp.broadcast_to(g_all[:, N_HEADS + h:N_HEADS + h + 1], (CHUNK, HEAD_DIM))
                 for h in range(N_HEADS)], axis=1)
            gc_all = _dot_sel(lower_ones, g_b)
            gates.append((c, r0, beta_all, gc_all))
        gc_rows = []
        for _, _, _, gc_all in gates:
            gc_diag = jnp.concatenate(
                [jnp.where(eye_w, gc_all[:, h * HEAD_DIM:(h + 1) * HEAD_DIM], 0.0) for h in range(N_HEADS)], axis=1)
            gc_rows.append(_dot_sel(all_ones, gc_diag))

        probs = []
        for (c, r0, beta_all, gc_all), gc_row_all in zip(gates, gc_rows):
            for h in range(N_HEADS):
                hs = slice(h * HEAD_DIM, (h + 1) * HEAD_DIM)
                beta = beta_all[:, h:h + 1]
                gc = gc_all[:, hs]
                decay = jnp.exp(jnp.where(incl_w, gc - gc_row_all[:, hs], -jnp.inf))[:, :CHUNK]
                q = qn_ref[pl.ds(r0, CHUNK), hs]
                k = kn_ref[pl.ds(r0, CHUNK), hs]
                v = vn_ref[pl.ds(r0, CHUNK), hs]
                kb = k * beta
                k16 = k.astype(BF16)
                a_mat = jnp.where(strict, _dot_nt(kb.astype(BF16), k16) * decay, 0.0)
                qk = (_dot_nt(q.astype(BF16), k16) * decay).astype(BF16)
                probs.append(dict(c=c, r0=r0, h=h, hs=hs, gc=gc, q=q, k=k, kb=kb, vb=v * beta, a=a_mat, qk=qk))

        invs = _unit_lower_inverses([p["a"] for p in probs], row, col)
        uws = [_dot_unit_left(inv.astype(BF16), jnp.concatenate([p["vb"], p["kb"] * jnp.exp(p["gc"])], axis=1))
               for inv, p in zip(invs, probs)]
        for p, uw in zip(probs, uws):
            r0, hs, gc, h = p["r0"], p["hs"], p["gc"], p["h"]
            g_last = gc[CHUNK - 1:CHUNK, :]
            u_ref[pl.ds(r0, CHUNK), hs] = uw[:, :HEAD_DIM]
            w_ref[pl.ds(r0, CHUNK), hs] = uw[:, HEAD_DIM:].astype(BF16)
            qg_ref[pl.ds(r0, CHUNK), hs] = (p["q"] * jnp.exp(gc)).astype(BF16)
            kd_ref[pl.ds(r0, CHUNK), hs] = (p["k"] * jnp.exp(g_last - gc)).astype(BF16)
            qk_ref[pl.ds(r0, CHUNK), h * HEAD_DIM:h * HEAD_DIM + CHUNK] = p["qk"]
            el_ref[pl.ds(pl.multiple_of(p["c"] * SUBLANES, SUBLANES), SUBLANES), hs] = jnp.broadcast_to(
                jnp.exp(g_last), (SUBLANES, HEAD_DIM))
        return 0

    lax.fori_loop(0, ts // (CHUNK * GDN_CHUNKS_PER_STEP), prepare, 0)

    def scan(step, _):
        heads = [slice(h * HEAD_DIM, (h + 1) * HEAD_DIM) for h in range(N_HEADS)]
        states = [st_ref[h] for h in range(N_HEADS)]
        for sub in range(GDN_SCAN_CHUNKS_PER_STEP):
            c = step * GDN_SCAN_CHUNKS_PER_STEP + sub
            r0 = pl.multiple_of(c * CHUNK, CHUNK)
            s16 = [s.astype(BF16) for s in states]
            ws = [_dot(w_ref[pl.ds(r0, CHUNK), hs], s) for hs, s in zip(heads, s16)]
            inter = [_dot(qg_ref[pl.ds(r0, CHUNK), hs], s) for hs, s in zip(heads, s16)]
            v_new = [(u_ref[pl.ds(r0, CHUNK), hs] - x).astype(BF16) for hs, x in zip(heads, ws)]
            upd = [_dot_tn(kd_ref[pl.ds(r0, CHUNK), hs], vn) for hs, vn in zip(heads, v_new)]
            for h, hs in enumerate(heads):
                e_last = el_ref[pl.ds(pl.multiple_of(c * SUBLANES, SUBLANES), SUBLANES), hs][0:1, :]
                states[h] = states[h] * e_last + upd[h]
            intra = [_dot(qk_ref[pl.ds(r0, CHUNK), h * HEAD_DIM:h * HEAD_DIM + CHUNK], vn)
                     for h, vn in enumerate(v_new)]
            for h, hs in enumerate(heads):
                zg = z_ref[pl.ds(r0, CHUNK), hs].astype(F32)
                o_ref[pl.ds(r0, CHUNK), hs] = (
                    _rms(inter[h] + intra[h], gain_ref[...]) * _silu(zg)).astype(o_ref.dtype)
        for h in range(N_HEADS):
            st_ref[h] = states[h]
        return 0

    lax.fori_loop(0, ts // (CHUNK * GDN_SCAN_CHUNKS_PER_STEP), scan, 0)


def _gdn(proj, small, conv_w, head_params, out_gain, batch, seq, *, ts):
    nt = seq // ts
    kern = functools.partial(_gdn_kernel, ts=ts)
    qkv_blk = COL_GD // (3 * GROUP)
    z_blk = (COL_GD + 3 * GROUP) // GROUP
    return pl.pallas_call(
        kern,
        grid=(batch, nt),
        in_specs=[
            pl.BlockSpec((ts, 3 * GROUP), lambda b, t: (b * nt + t, qkv_blk)),
            pl.BlockSpec((ts, GROUP), lambda b, t: (b * nt + t, z_blk)),
            pl.BlockSpec((ts, SMALL_W), lambda b, t: (b * nt + t, 0)),
            pl.BlockSpec((CONV_TAPS, 3 * GROUP), lambda b, t: (0, 0)),
            pl.BlockSpec(head_params.shape, lambda b, t: (0, 0)),
            pl.BlockSpec((1, HEAD_DIM), lambda b, t: (0, 0)),
        ],
        out_specs=pl.BlockSpec((ts, GROUP), lambda b, t: (b * nt + t, 0)),
        out_shape=jax.ShapeDtypeStruct((batch * seq, GROUP), BF16),
        scratch_shapes=[
            pltpu.VMEM((ts + CONV_HALO, 3 * GROUP), BF16),
            pltpu.VMEM((ts, GROUP), F32),
            pltpu.VMEM((ts, GROUP), F32),
            pltpu.VMEM((ts, GROUP), F32),
            pltpu.VMEM((ts, GROUP), F32),
            pltpu.VMEM((ts, GROUP), BF16),
            pltpu.VMEM((ts, GROUP), BF16),
            pltpu.VMEM((ts, GROUP), BF16),
            pltpu.VMEM((ts, GROUP), BF16),
            pltpu.VMEM((ts // CHUNK * SUBLANES, GROUP), F32),
            pltpu.VMEM((N_HEADS, HEAD_DIM, HEAD_DIM), F32),
        ],
        compiler_params=_cparams("parallel", "arbitrary"),
        name="gated_deltanet",
    )(proj, proj, small, conv_w, head_params, out_gain.reshape(1, HEAD_DIM))


def _gla_kernel(qk_ref, v_ref, og_ref, sm_ref, w2_ref, gb_ref, gain_ref, o_ref,
                att_ref, qin_ref, kout_ref, el_ref, st_ref, *, ts):
    t = pl.program_id(1)

    @pl.when(t == 0)
    def _():
        st_ref[...] = jnp.zeros_like(st_ref)

    row, col = _chunk_iotas()
    lower_ones = jnp.where(row >= col, 1.0, 0.0).astype(BF16)
    eye = row == col
    row_w, lane = _chunk_iotas(HEAD_DIM)
    qk_scale = GLA_DK ** -0.5

    levels = []
    for level in range(CHUNK.bit_length() - 1):
        ref_row = ((row >> (level + 1)) << (level + 1)) + (1 << level)
        pick = jnp.where(col == ref_row, 1.0, 0.0).astype(BF16)
        upper_w = ((row_w >> level) & 1) == 1
        levels.append((pick, _lower_left_blocks(row, col, level), upper_w))

    n_pairs = N_HEADS * GLA_DK // HEAD_DIM
    heads_per_pair = HEAD_DIM // GLA_DK

    def prepare(step, _):
        cums = []
        for sub in range(GLA_CHUNKS_PER_STEP):
            c = step * GLA_CHUNKS_PER_STEP + sub
            r0 = pl.multiple_of(c * CHUNK, CHUNK)
            sm = sm_ref[pl.ds(r0, CHUNK), :].astype(BF16)
            gate = _dot(sm, w2_ref[...]) + gb_ref[...]
            log_a = (gate - _softplus(gate)) * (1.0 / GLA_TAU)
            cums.append((c, r0, _dot_sel(lower_ones, log_a)))
        probs = []
        for c, r0, b_all in cums:
            for pair in range(n_pairs):
                ps = slice(pair * HEAD_DIM, (pair + 1) * HEAD_DIM)
                b = b_all[:, ps]
                q2 = qk_ref[pl.ds(r0, CHUNK), ps].astype(F32) * qk_scale
                k2 = qk_ref[pl.ds(r0, CHUNK),
                            GROUP // 2 + pair * HEAD_DIM:GROUP // 2 + (pair + 1) * HEAD_DIM].astype(F32)
                probs.append(dict(c=c, r0=r0, pair=pair, ps=ps, b=b, q2=q2, k2=k2,
                                  b_refs=[_dot_sel(pick, b) for pick, _, _ in levels]))
        for p in probs:
            b, q2, k2 = p["b"], p["q2"], p["k2"]
            p["q_lv"] = [jnp.where(upper_w, q2 * jnp.exp(jnp.where(upper_w, b - b_ref, 0.0)), 0.0)
                         for (_, _, upper_w), b_ref in zip(levels, p["b_refs"])]
            p["k_lv"] = [jnp.where(upper_w, 0.0, k2 * jnp.exp(jnp.where(upper_w, 0.0, b_ref - b))).astype(BF16)
                         for (_, _, upper_w), b_ref in zip(levels, p["b_refs"])]
        for p in probs:
            r0, b, q2, k2 = p["r0"], p["b"], p["q2"], p["k2"]
            b_last = b[CHUNK - 1:CHUNK, :]
            q_in = (q2 * jnp.exp(b)).astype(BF16)
            k_out = (k2 * jnp.exp(b_last - b)).astype(BF16)
            diag_qk = q2 * k2
            for r in range(heads_per_pair):
                h = p["pair"] * heads_per_pair + r
                hs = slice(h * HEAD_DIM, (h + 1) * HEAD_DIM)
                mine = (lane >= r * GLA_DK) & (lane < (r + 1) * GLA_DK)
                att = jnp.where(eye, jnp.sum(jnp.where(mine, diag_qk, 0.0), axis=1, keepdims=True), 0.0)
                for (_, pair_mask, _), ql, kl in zip(levels, p["q_lv"], p["k_lv"]):
                    s = _dot_nt(jnp.where(mine, ql, 0.0).astype(BF16), kl)
                    att = att + jnp.where(pair_mask, s, 0.0)
                att_ref[pl.ds(r0, CHUNK), h * HEAD_DIM:h * HEAD_DIM + CHUNK] = att.astype(BF16)
                qin_ref[pl.ds(r0, CHUNK), hs] = jnp.where(mine, q_in, jnp.zeros_like(q_in))
                kout_ref[pl.ds(r0, CHUNK), hs] = jnp.where(mine, k_out, jnp.zeros_like(k_out))
            el_ref[pl.ds(pl.multiple_of(p["c"] * SUBLANES, SUBLANES), SUBLANES), p["ps"]] = jnp.broadcast_to(
                jnp.exp(b_last), (SUBLANES, HEAD_DIM))
        return 0

    lax.fori_loop(0, ts // (CHUNK * GLA_CHUNKS_PER_STEP), prepare, 0)

    def scan(step, _):
        heads = [slice(h * HEAD_DIM, (h + 1) * HEAD_DIM) for h in range(N_HEADS)]
        chunks = []
        for sub in range(GLA_SCAN_CHUNKS_PER_STEP):
            c = step * GLA_SCAN_CHUNKS_PER_STEP + sub
            r0 = pl.multiple_of(c * CHUNK, CHUNK)
            values = [v_ref[pl.ds(r0, CHUNK), hs] for hs in heads]
            intra = [_dot(att_ref[pl.ds(r0, CHUNK), h * HEAD_DIM:h * HEAD_DIM + CHUNK], v)
                     for h, v in enumerate(values)]
            upd = [_dot_tn(v, kout_ref[pl.ds(r0, CHUNK), hs]) for hs, v in zip(heads, values)]
            chunks.append((c, r0, intra, upd))
        states = [st_ref[h] for h in range(N_HEADS)]
        for c, r0, intra, upd in chunks:
            inter = [_dot_nt(qin_ref[pl.ds(r0, CHUNK), hs], s.astype(BF16)) for hs, s in zip(heads, states)]
            for h, hs in enumerate(heads):
                ps = slice((h // heads_per_pair) * HEAD_DIM, (h // heads_per_pair + 1) * HEAD_DIM)
                e_last = el_ref[pl.ds(pl.multiple_of(c * SUBLANES, SUBLANES), SUBLANES), ps][0:1, :]
                states[h] = states[h] * e_last + upd[h]
                og = og_ref[pl.ds(r0, CHUNK), hs].astype(F32)
                o_ref[pl.ds(r0, CHUNK), hs] = (
                    _rms(inter[h] + intra[h], gain_ref[...]) * _silu(og)).astype(o_ref.dtype)
        for h in range(N_HEADS):
            st_ref[h] = states[h]
        return 0

    lax.fori_loop(0, ts // (CHUNK * GLA_SCAN_CHUNKS_PER_STEP), scan, 0)


def _gla(proj, small, w2_padded, gate_bias, out_gain, batch, seq, *, ts):
    nt = seq // ts
    kern = functools.partial(_gla_kernel, ts=ts)
    qk_blk = COL_GL // GROUP
    return pl.pallas_call(
        kern,
        grid=(batch, nt),
        in_specs=[
            pl.BlockSpec((ts, GROUP), lambda b, t: (b * nt + t, qk_blk)),
            pl.BlockSpec((ts, GROUP), lambda b, t: (b * nt + t, qk_blk + 1)),
            pl.BlockSpec((ts, GROUP), lambda b, t: (b * nt + t, qk_blk + 2)),
            pl.BlockSpec((ts, SMALL_W), lambda b, t: (b * nt + t, 0)),
            pl.BlockSpec(w2_padded.shape, lambda b, t: (0, 0)),
            pl.BlockSpec((1, N_HEADS * GLA_DK), lambda b, t: (0, 0)),
            pl.BlockSpec((1, HEAD_DIM), lambda b, t: (0, 0)),
        ],
        out_specs=pl.BlockSpec((ts, GROUP), lambda b, t: (b * nt + t, 0)),
        out_shape=jax.ShapeDtypeStruct((batch * seq, GROUP), BF16),
        scratch_shapes=[
            pltpu.VMEM((ts, GROUP), BF16),
            pltpu.VMEM((ts, GROUP), BF16),
            pltpu.VMEM((ts, GROUP), BF16),
            pltpu.VMEM((ts // CHUNK * SUBLANES, N_HEADS * GLA_DK), F32),
            pltpu.VMEM((N_HEADS, HEAD_DIM, HEAD_DIM), F32),
        ],
        compiler_params=_cparams("parallel", "arbitrary"),
        name="gla",
    )(proj, proj, proj, small, w2_padded, gate_bias.reshape(1, -1), out_gain.reshape(1, HEAD_DIM))


def _out_proj_kernel(x_ref, m0_ref, m1_ref, m2_ref, m3_ref, w_ref, o_ref):
    acc = x_ref[...]
    for g, m_ref in enumerate((m0_ref, m1_ref, m2_ref, m3_ref)):
        acc = acc + _dot(m_ref[...], w_ref[g * GROUP:(g + 1) * GROUP, :])
    o_ref[...] = acc


def _out_proj(x, mixes, w_out, layer, *, tm):
    t, d = x.shape
    mix_spec = pl.BlockSpec((tm, GROUP), lambda i: (i, 0))
    return pl.pallas_call(
        _out_proj_kernel,
        grid=(t // tm,),
        in_specs=[pl.BlockSpec((tm, d), lambda i: (i, 0)), mix_spec, mix_spec, mix_spec, mix_spec,
                  pl.BlockSpec((None,) + w_out.shape[1:], lambda i: (layer, 0, 0))],
        out_specs=pl.BlockSpec((tm, d), lambda i: (i, 0)),
        out_shape=jax.ShapeDtypeStruct((t, d), F32),
        compiler_params=_cparams("parallel"),
        name="out_proj_residual",
    )(x, *mixes, w_out)


def _ffn_kernel(x_ref, g_ref, wg_ref, wu_ref, wd_ref, og_ref, o_ref, h_ref, *, norm_output):
    f = pl.program_id(1)

    def swiglu_tile(h):
        hid = _silu(_dot(h, wg_ref[...])) * _dot(h, wu_ref[...])
        return _dot(hid.astype(BF16), wd_ref[...])

    @pl.when(f == 0)
    def _():
        x = x_ref[...]
        h = _rms(x, g_ref[...]).astype(BF16)
        h_ref[...] = h
        o_ref[...] = x + swiglu_tile(h)

    @pl.when(f > 0)
    def _():
        o_ref[...] += swiglu_tile(h_ref[...])

    if norm_output:
        @pl.when(f == pl.num_programs(1) - 1)
        def _():
            o_ref[...] = _rms(o_ref[...], og_ref[...])


def _ffn(x, gain, w_gate, w_up, w_down, layer, out_gain, *, norm_output, tm, tf):
    t, d = x.shape
    hidden = w_gate.shape[2]
    return pl.pallas_call(
        functools.partial(_ffn_kernel, norm_output=norm_output),
        grid=(t // tm, hidden // tf),
        in_specs=[
            pl.BlockSpec((tm, d), lambda i, f: (i, 0)),
            pl.BlockSpec((1, d), lambda i, f: (0, 0)),
            pl.BlockSpec((None, d, tf), lambda i, f: (layer, 0, f)),
            pl.BlockSpec((None, d, tf), lambda i, f: (layer, 0, f)),
            pl.BlockSpec((None, tf, d), lambda i, f: (layer, f, 0)),
            pl.BlockSpec((1, d), lambda i, f: (0, 0)),
        ],
        out_specs=pl.BlockSpec((tm, d), lambda i, f: (i, 0)),
        out_shape=jax.ShapeDtypeStruct((t, d), F32),
        scratch_shapes=[pltpu.VMEM((tm, d), BF16)],
        compiler_params=_cparams("parallel", "arbitrary"),
        name="ffn_residual",
    )(x, gain.reshape(1, d), w_gate, w_up, w_down, out_gain.reshape(1, d))


def _pack_w_in(w_in):
    gd_small = COL_GD + 4 * GROUP
    df0 = gd_small + 2 * N_HEADS
    gl0 = df0 + 3 * GROUP
    rank0 = gl0 + 3 * GROUP
    pad = jnp.zeros(w_in.shape[:-1] + (IN_COLS_PACKED - COL_SMALL - 2 * N_HEADS - GLA_RANK,), BF16)
    parts = [w_in[..., :gd_small], w_in[..., df0:rank0], w_in[..., gd_small:df0], w_in[..., rank0:rank0 + GLA_RANK]]
    return jnp.concatenate([part.astype(BF16) for part in parts] + [pad], axis=-1)


def _mixer_layer(x, l, p, big, batch, seq):
    lam_init = 0.8 - 0.6 * math.exp(-0.3 * l)
    proj, small = _norm_in_proj(x, p["attn_norm"], big["w_in"], l, tm=IN_PROJ_TILE[0], tn=IN_PROJ_TILE[1])
    o_sb = _sb_attention(proj, batch, seq)
    head_params = jnp.zeros((SUBLANES, SMALL_W), F32)
    head_params = head_params.at[0, N_HEADS:2 * N_HEADS].set(p["gdn_a_log"])
    head_params = head_params.at[1, N_HEADS:2 * N_HEADS].set(p["gdn_dt_bias"])
    o_gd = _gdn(proj, small, p["gdn_conv_w"], head_params, p["gdn_out_norm"], batch, seq, ts=SCAN_TILE)
    lam_vecs = jnp.stack([p["diff_lam_q1"], p["diff_lam_k1"], p["diff_lam_q2"], p["diff_lam_k2"]])
    o_df = _diff_attention(proj, lam_vecs, p["diff_out_norm"], lam_init, batch, seq)
    w2 = jnp.zeros((SMALL_W, N_HEADS * GLA_DK), F32).at[2 * N_HEADS:2 * N_HEADS + GLA_RANK].set(p["gla_gate_w2"])
    o_gl = _gla(proj, small, w2.astype(BF16), p["gla_gate_b"], p["gla_out_norm"], batch, seq, ts=SCAN_TILE)
    return _out_proj(x, (o_sb, o_gd, o_df, o_gl), big["w_out"], l, tm=OUT_PROJ_ROWS)


def kernel(x, attn_norm, w_in, gdn_conv_w, gdn_a_log, gdn_dt_bias, gdn_out_norm, diff_lam_q1, diff_lam_k1,
           diff_lam_q2, diff_lam_k2, diff_out_norm, gla_gate_w2, gla_gate_b, gla_out_norm, w_out, ffn_norm,
           w_gate, w_up, w_down, final_norm):
    batch, seq, d = x.shape
    depth = w_in.shape[0]
    tokens = batch * seq
    assert seq % (2 * ATT_BLOCK) == 0 and seq % SCAN_TILE == 0, (seq, ATT_BLOCK, SCAN_TILE)
    assert tokens % IN_PROJ_TILE[0] == 0 and tokens % FFN_TILE[0] == 0 and tokens % OUT_PROJ_ROWS == 0, tokens
    assert w_gate.shape[2] % FFN_TILE[1] == 0, w_gate.shape
    stacked = dict(attn_norm=attn_norm, gdn_conv_w=gdn_conv_w, gdn_a_log=gdn_a_log,
                   gdn_dt_bias=gdn_dt_bias, gdn_out_norm=gdn_out_norm, diff_lam_q1=diff_lam_q1,
                   diff_lam_k1=diff_lam_k1, diff_lam_q2=diff_lam_q2, diff_lam_k2=diff_lam_k2,
                   diff_out_norm=diff_out_norm, gla_gate_w2=gla_gate_w2, gla_gate_b=gla_gate_b,
                   gla_out_norm=gla_out_norm)
    big = dict(w_in=_pack_w_in(w_in), w_out=w_out.astype(BF16), w_gate=w_gate.astype(BF16),
               w_up=w_up.astype(BF16), w_down=w_down.astype(BF16))
    xt = x.reshape(tokens, d)
    for l in range(depth):
        p = {name: arr[l] for name, arr in stacked.items()}
        xt = _mixer_layer(xt, l, p, big, batch, seq)
        xt = _ffn(xt, ffn_norm[l], big["w_gate"], big["w_up"], big["w_down"], l, final_norm,
                  norm_output=(l == depth - 1), tm=FFN_TILE[0], tf=FFN_TILE[1])
    return xt.reshape(batch, seq, d)
```

```python
import functools
import math

import jax
import jax.numpy as jnp
from jax import lax
from jax.experimental import pallas as pl
from jax.experimental.pallas import tpu as pltpu

F32 = jnp.float32
BF16 = jnp.bfloat16

NORM_EPS = 1e-6
LOG2_E = math.log2(math.e)
SB_DEAD_LOG2 = 110.0 * LOG2_E
N_HEADS = 4
HEAD_DIM = 128
GROUP = N_HEADS * HEAD_DIM
CHUNK = 64
CONV_TAPS = 4
GLA_DK = 64
GLA_RANK = 16
GLA_TAU = 16.0
CONV_HALO = 16
SUBLANES = 8

ATT_BLOCK = 256
DIFF_HEADS_PER_STEP = 4
SB_HEADS_PER_STEP = 4
SCAN_TILE = 1024
GDN_CONV_CHUNKS_PER_STEP = 4
GDN_CHUNKS_PER_STEP = 4
GDN_SCAN_CHUNKS_PER_STEP = 4
GLA_CHUNKS_PER_STEP = 4
GLA_SCAN_CHUNKS_PER_STEP = 4
IN_PROJ_TILE = (1024, 1792)
OUT_PROJ_ROWS = 512
FFN_TILE = (1024, 512)
VMEM_LIMIT = 56 * 1024 * 1024

COL_SB = 0
COL_GD = 3 * GROUP
COL_DF = 7 * GROUP
COL_GL = 10 * GROUP
COL_SMALL = 13 * GROUP
SMALL_W = 128
IN_COLS_PACKED = 14 * GROUP


def _cparams(*sem):
    return pltpu.CompilerParams(dimension_semantics=sem, vmem_limit_bytes=VMEM_LIMIT)


def _dot(a, b):
    return jnp.dot(a, b, preferred_element_type=F32)


def _dot_nt(a, b):
    return lax.dot_general(a, b, (((1,), (1,)), ((), ())), preferred_element_type=F32)


def _dot_tn(a, b):
    return lax.dot_general(a, b, (((0,), (0,)), ((), ())), preferred_element_type=F32)


def _split(x):
    hi = x.astype(BF16)
    return hi, (x - hi.astype(F32)).astype(BF16)


def _dot_sel(sel16, x):
    hi, lo = _split(x)
    return _dot(sel16, hi) + _dot(sel16, lo)


def _dot_unit_left(unit16, x):
    hi, lo = _split(x)
    return _dot(unit16, hi) + _dot(unit16, lo)


def _dot_unit_right(x, unit16):
    hi, lo = _split(x)
    return _dot(hi, unit16) + _dot(lo, unit16)


def _rms(x, gain):
    return x * lax.rsqrt(jnp.mean(x * x, axis=-1, keepdims=True) + NORM_EPS) * gain


def _silu(x):
    return x / (1.0 + jnp.exp(-x))


def _sigmoid(x):
    return 1.0 / (1.0 + jnp.exp(-x))


def _softplus(x):
    return jnp.maximum(x, 0.0) + jnp.log(1.0 + jnp.exp(-jnp.abs(x)))


def _norm_matmul_kernel(x_ref, g_ref, w_ref, o_ref, small_ref, h_ref):
    j = pl.program_id(1)
    last = pl.num_programs(1) - 1

    @pl.when(j == 0)
    def _():
        h = _rms(x_ref[...], g_ref[...]).astype(BF16)
        h_ref[...] = h
        o_ref[...] = _dot(h, w_ref[...]).astype(o_ref.dtype)

    @pl.when(j > 0)
    def _():
        y = _dot(h_ref[...], w_ref[...])
        o_ref[...] = y.astype(o_ref.dtype)

        @pl.when(j == last)
        def _():
            small_ref[...] = y[:, y.shape[1] - small_ref.shape[1]:]


def _norm_in_proj(x, gain, w, layer, *, tm, tn):
    t, d = x.shape
    return pl.pallas_call(
        _norm_matmul_kernel,
        grid=(t // tm, IN_COLS_PACKED // tn),
        in_specs=[
            pl.BlockSpec((tm, d), lambda i, j: (i, 0)),
            pl.BlockSpec((1, d), lambda i, j: (0, 0)),
            pl.BlockSpec((None, d, tn), lambda i, j: (layer, 0, j)),
        ],
        out_specs=[
            pl.BlockSpec((tm, tn), lambda i, j: (i, j)),
            pl.BlockSpec((tm, IN_COLS_PACKED - COL_SMALL), lambda i, j: (i, 0)),
        ],
        out_shape=[jax.ShapeDtypeStruct((t, IN_COLS_PACKED), BF16),
                   jax.ShapeDtypeStruct((t, IN_COLS_PACKED - COL_SMALL), F32)],
        scratch_shapes=[pltpu.VMEM((tm, d), BF16)],
        compiler_params=_cparams("parallel", "arbitrary"),
        name="norm_in_proj",
    )(x, gain.reshape(1, d), w)


def _sb_kernel(q_ref, k_ref, v_ref, o_ref, *, blk, scale):
    i = pl.program_id(2)
    heads = [slice(h * HEAD_DIM, (h + 1) * HEAD_DIM) for h in range(SB_HEADS_PER_STEP)]
    row = lax.broadcasted_iota(jnp.int32, (blk, blk), 0)
    col = lax.broadcasted_iota(jnp.int32, (blk, blk), 1)
    suffix = jnp.where(row > col, 1.0, 0.0).astype(BF16)
    lead = jnp.maximum(i - 1, 0)
    col_minus_row = (lax.broadcasted_iota(jnp.int32, (blk, 2 * blk), 1)
                     - lax.broadcasted_iota(jnp.int32, (blk, 2 * blk), 0))
    causal = col_minus_row < (i - lead) * blk

    def span(first_block, nblk, state, masked):
        rows = pl.ds(pl.multiple_of(first_block * blk, blk), nblk * blk)
        parts = [slice(b * blk, (b + 1) * blk) for b in range(nblk)]
        zs = [_dot_nt(q_ref[:, hs], k_ref[rows, hs]) * (scale * LOG2_E) for hs in heads]
        sps = []
        for z in zs:
            sp = jnp.maximum(z, 0.0) + jnp.log2(1.0 + jnp.exp2(-jnp.abs(z)))
            sps.append(jnp.where(causal, sp, 0.0) if masked else sp)
        tails = [[_dot_sel_rhs(sp[:, s], suffix) for s in parts] for sp in sps]
        out = []
        for hs, z, sp, tail, (carry, acc) in zip(heads, zs, sps, tails, state):
            sums = [jnp.sum(sp[:, s], axis=1, keepdims=True) for s in parts]
            offsets = [None] * nblk
            for b in reversed(range(nblk)):
                offsets[b] = carry
                carry = carry + sums[b]
            att = jnp.concatenate(
                [jnp.exp2(z[:, s] - sp[:, s] - tail[b] - offsets[b])
                 for b, s in enumerate(parts)], axis=1)
            if masked:
                att = jnp.where(causal, att, 0.0)
            out.append((carry, acc + _dot(att.astype(BF16), v_ref[rows, hs])))
        return tuple(out)

    init = (jnp.zeros((blk, 1), F32), jnp.zeros((blk, HEAD_DIM), F32))
    state = span(lead, 2, (init,) * len(heads), True)

    def alive(state):
        low = state[0][0]
        for carry, _ in state[1:]:
            low = jnp.minimum(low, carry)
        return jnp.min(low) < SB_DEAD_LOG2

    _, state = lax.while_loop(
        lambda carried: jnp.logical_and(carried[0] < lead // 2, alive(carried[1])),
        lambda carried: (carried[0] + 1, span(lead - 2 - 2 * carried[0], 2, carried[1], False)),
        (jnp.int32(0), state))
    odd_left = jnp.where(alive(state), lead % 2, 0)
    state = lax.fori_loop(0, odd_left, lambda n, st: span(0, 1, st, False), state)
    for hs, (_, acc) in zip(heads, state):
        o_ref[:, hs] = acc.astype(o_ref.dtype)


def _dot_sel_rhs(x, sel16):
    hi, lo = _split(x)
    return _dot(hi, sel16) + _dot(lo, sel16)


def _sb_attention(proj, batch, seq):
    blk = ATT_BLOCK
    nq = seq // blk
    width = SB_HEADS_PER_STEP * HEAD_DIM
    qb, kb, vb = COL_SB // width, (COL_SB + GROUP) // width, (COL_SB + 2 * GROUP) // width
    kern = functools.partial(_sb_kernel, blk=blk, scale=HEAD_DIM ** -0.5)
    return pl.pallas_call(
        kern,
        grid=(batch, N_HEADS // SB_HEADS_PER_STEP, nq),
        in_specs=[
            pl.BlockSpec((blk, width), lambda b, h, i: (b * nq + i, qb + h)),
            pl.BlockSpec((seq, width), lambda b, h, i: (b, kb + h)),
            pl.BlockSpec((seq, width), lambda b, h, i: (b, vb + h)),
        ],
        out_specs=pl.BlockSpec((blk, width), lambda b, h, i: (b * nq + i, h)),
        out_shape=jax.ShapeDtypeStruct((batch * seq, GROUP), BF16),
        compiler_params=_cparams("parallel", "parallel", "arbitrary"),
        name="sb_attention",
    )(proj, proj, proj)


def _diff_kernel(lam_ref, q_ref, k_ref, v_ref, g_ref, o_ref, *, blk, scale, lam_init):
    i = pl.program_id(2)
    half = HEAD_DIM // 2
    lane = lax.broadcasted_iota(jnp.int32, (blk, HEAD_DIM), 1)
    real_heads = [slice(h * HEAD_DIM, (h + 1) * HEAD_DIM) for h in range(DIFF_HEADS_PER_STEP)]
    heads, qs = [], []
    for hs in real_heads:
        q = q_ref[:, hs].astype(F32) * scale
        heads += [hs, hs]
        qs += [jnp.where(lane < half, q, 0.0).astype(BF16), jnp.where(lane >= half, q, 0.0).astype(BF16)]
    lead = jnp.maximum(i - 1, 0)
    row = lax.broadcasted_iota(jnp.int32, (blk, 2 * blk), 0)
    col = lax.broadcasted_iota(jnp.int32, (blk, 2 * blk), 1)
    causal = col - row <= (i - lead) * blk

    def softmax(s, state, masked):
        m, l_lanes, acc = state
        if masked:
            s = jnp.where(causal, s, -jnp.inf)
        m_new = jnp.maximum(m, jnp.max(s, axis=1, keepdims=True))
        alpha = jnp.exp(m - m_new)
        p = jnp.exp(s - m_new)
        p_lanes = p[:, 0:HEAD_DIM]
        for c in range(1, s.shape[1] // HEAD_DIM):
            p_lanes = p_lanes + p[:, c * HEAD_DIM:(c + 1) * HEAD_DIM]
        return m_new, alpha * l_lanes + p_lanes, alpha * acc, p.astype(BF16)

    def span(first_block, nblk, states, masked):
        rows = pl.ds(pl.multiple_of(first_block * blk, blk), nblk * blk)
        out = []
        s_next = _dot_nt(qs[0], k_ref[rows, heads[0]])
        for h, (hs, state) in enumerate(zip(heads, states)):
            s = s_next
            if h + 1 < len(heads):
                s_next = _dot_nt(qs[h + 1], k_ref[rows, heads[h + 1]])
            m, l_lanes, acc, p = softmax(s, state, masked)
            out.append((m, l_lanes, acc + _dot(p, v_ref[rows, hs])))
        return tuple(out)

    init = (jnp.full((blk, 1), -jnp.inf, F32), jnp.zeros((blk, HEAD_DIM), F32),
            jnp.zeros((blk, HEAD_DIM), F32))
    states = span(lead, 2, (init,) * len(heads), True)
    states = lax.fori_loop(0, lead // 2, lambda n, st: span(2 * n, 2, st, False), states)
    states = lax.fori_loop(0, lead % 2, lambda n, st: span(lead - 1, 1, st, False), states)

    lam_v = lam_ref[...]
    d1 = jnp.sum(lam_v[0:1, :] * lam_v[1:2, :], axis=1, keepdims=True)
    d2 = jnp.sum(lam_v[2:3, :] * lam_v[3:4, :], axis=1, keepdims=True)
    lam = jnp.exp(d1) - jnp.exp(d2) + lam_init
    norms = [acc / jnp.sum(l_lanes, axis=1, keepdims=True) for _, l_lanes, acc in states]
    for h, hs in enumerate(real_heads):
        out = norms[2 * h] - lam * norms[2 * h + 1]
        o_ref[:, hs] = (_rms(out, g_ref[...]) * (1.0 - lam_init)).astype(o_ref.dtype)


def _diff_attention(proj, lam_vecs, out_gain, lam_init, batch, seq):
    blk = ATT_BLOCK
    nq = seq // blk
    width = DIFF_HEADS_PER_STEP * HEAD_DIM
    qb, kb, vb = COL_DF // width, (COL_DF + GROUP) // width, (COL_DF + 2 * GROUP) // width
    kern = functools.partial(_diff_kernel, blk=blk, scale=(HEAD_DIM // 2) ** -0.5, lam_init=lam_init)
    return pl.pallas_call(
        kern,
        grid=(batch, N_HEADS // DIFF_HEADS_PER_STEP, nq),
        in_specs=[
            pl.BlockSpec(lam_vecs.shape, lambda b, h, i: (0, 0)),
            pl.BlockSpec((blk, width), lambda b, h, i: (b * nq + i, qb + h)),
            pl.BlockSpec((seq, width), lambda b, h, i: (b, kb + h)),
            pl.BlockSpec((seq, width), lambda b, h, i: (b, vb + h)),
            pl.BlockSpec((1, HEAD_DIM), lambda b, h, i: (0, 0)),
        ],
        out_specs=pl.BlockSpec((blk, width), lambda b, h, i: (b * nq + i, h)),
        out_shape=jax.ShapeDtypeStruct((batch * seq, GROUP), BF16),
        compiler_params=_cparams("parallel", "parallel", "arbitrary"),
        name="diff_attention",
    )(lam_vecs, proj, proj, proj, out_gain.reshape(1, HEAD_DIM))


def _chunk_iotas(width=CHUNK):
    row = lax.broadcasted_iota(jnp.int32, (CHUNK, width), 0)
    col = lax.broadcasted_iota(jnp.int32, (CHUNK, width), 1)
    return row, col


def _lower_left_blocks(row, col, level):
    same_pair = (row >> (level + 1)) == (col >> (level + 1))
    return same_pair & (((row >> level) & 1) == 1) & (((col >> level) & 1) == 0)


def _unit_lower_inverses(a_list, row, col):
    eye = jnp.where(row == col, 1.0, 0.0).astype(F32)
    invs = [eye - jnp.where(_lower_left_blocks(row, col, 0), a, 0.0) for a in a_list]
    for level in range(1, CHUNK.bit_length() - 1):
        mask = _lower_left_blocks(row, col, level)
        inv16 = [inv.astype(BF16) for inv in invs]
        left = [_dot_unit_left(i16, jnp.where(mask, a, 0.0)) for i16, a in zip(inv16, a_list)]
        invs = [inv - _dot_unit_right(x, i16) for inv, x, i16 in zip(invs, left, inv16)]
    return invs


def _gdn_kernel(x_ref, z_ref, sm_ref, cw_ref, hp_ref, gain_ref, o_ref,
                xs_ref, qn_ref, kn_ref, vn_ref, u_ref, w_ref, qg_ref, kd_ref, qk_ref, el_ref, st_ref, *, ts):
    t = pl.program_id(1)

    @pl.when(t == 0)
    def _():
        xs_ref[0:CONV_HALO, :] = jnp.zeros((CONV_HALO, 3 * GROUP), BF16)
        st_ref[...] = jnp.zeros_like(st_ref)

    xs_ref[CONV_HALO:CONV_HALO + ts, :] = x_ref[...]

    win = CHUNK + CONV_HALO
    shift_row = lax.broadcasted_iota(jnp.int32, ((CONV_TAPS - 1) * CHUNK, win), 0)
    shift_col = lax.broadcasted_iota(jnp.int32, ((CONV_TAPS - 1) * CHUNK, win), 1)
    tap = shift_row >> (CHUNK.bit_length() - 1)
    shifts = jnp.where(shift_col == (shift_row & (CHUNK - 1)) + CONV_HALO - (CONV_TAPS - 1) + tap,
                       1.0, 0.0).astype(BF16)

    def conv(step, _):
        starts = [pl.multiple_of((step * GDN_CONV_CHUNKS_PER_STEP + sub) * CHUNK, CHUNK)
                  for sub in range(GDN_CONV_CHUNKS_PER_STEP)]
        windows = [xs_ref[pl.ds(r0, win), :] for r0 in starts]
        shifted = [_dot(shifts, window) for window in windows]
        for r0, window, taps in zip(starts, windows, shifted):
            acc = window[CONV_HALO:].astype(F32) * cw_ref[CONV_TAPS - 1:CONV_TAPS, :]
            for j in range(CONV_TAPS - 1):
                acc = acc + taps[j * CHUNK:(j + 1) * CHUNK] * cw_ref[j:j + 1, :]
            y = _silu(acc)
            for part, dst in enumerate((qn_ref, kn_ref, vn_ref)):
                for h in range(N_HEADS):
                    hs = slice(h * HEAD_DIM, (h + 1) * HEAD_DIM)
                    yb = y[:, part * GROUP + h * HEAD_DIM:part * GROUP + (h + 1) * HEAD_DIM]
                    if part < 2:
                        yb = yb * lax.rsqrt(jnp.sum(yb * yb, axis=-1, keepdims=True) + NORM_EPS)
                    if part == 0:
                        yb = yb * (HEAD_DIM ** -0.5)
                    dst[pl.ds(r0, CHUNK), hs] = yb
        return 0

    lax.fori_loop(0, ts // (CHUNK * GDN_CONV_CHUNKS_PER_STEP), conv, 0)
    xs_ref[0:CONV_HALO, :] = xs_ref[ts:ts + CONV_HALO, :]

    row, col = _chunk_iotas()
    strict = row > col
    row_w, col_w = _chunk_iotas(HEAD_DIM)
    incl_w = row_w >= col_w
    eye_w = row_w == col_w
    lower_ones = jnp.where(row >= col, 1.0, 0.0).astype(BF16)
    all_ones = jnp.ones((CHUNK, CHUNK), BF16)

    def prepare(pair, _):
        gates = []
        for sub in range(GDN_CHUNKS_PER_STEP):
            c = pair * GDN_CHUNKS_PER_STEP + sub
            r0 = pl.multiple_of(c * CHUNK, CHUNK)
            sm = sm_ref[pl.ds(r0, CHUNK), :]
            beta_all = _sigmoid(sm)
            g_all = -jnp.exp(hp_ref[0:1, :]) * _softplus(sm + hp_ref[1:2, :])
            g_b = jnp.concatenate(
                [jnp.broadcast_to(g_all[:, N_HEADS + h:N_HEADS + h + 1], (CHUNK, HEAD_DIM))
                 for h in range(N_HEADS)], axis=1)
            gc_all = _dot_sel(lower_ones, g_b)
            gates.append((c, r0, beta_all, gc_all))
        gc_rows = []
        for _, _, _, gc_all in gates:
            gc_diag = jnp.concatenate(
                [jnp.where(eye_w, gc_all[:, h * HEAD_DIM:(h + 1) * HEAD_DIM], 0.0) for h in range(N_HEADS)], axis=1)
            gc_rows.append(_dot_sel(all_ones, gc_diag))

        probs = []
        for (c, r0, beta_all, gc_all), gc_row_all in zip(gates, gc_rows):
            for h in range(N_HEADS):
                hs = slice(h * HEAD_DIM, (h + 1) * HEAD_DIM)
                beta = beta_all[:, h:h + 1]
                gc = gc_all[:, hs]
                decay = jnp.exp(jnp.where(incl_w, gc - gc_row_all[:, hs], -jnp.inf))[:, :CHUNK]
                q = qn_ref[pl.ds(r0, CHUNK), hs]
                k = kn_ref[pl.ds(r0, CHUNK), hs]
                v = vn_ref[pl.ds(r0, CHUNK), hs]
                kb = k * beta
                k16 = k.astype(BF16)
                a_mat = jnp.where(strict, _dot_nt(kb.astype(BF16), k16) * decay, 0.0)
                qk = (_dot_nt(q.astype(BF16), k16) * decay).astype(BF16)
                probs.append(dict(c=c, r0=r0, h=h, hs=hs, gc=gc, q=q, k=k, kb=kb, vb=v * beta, a=a_mat, qk=qk))

        invs = _unit_lower_inverses([p["a"] for p in probs], row, col)
        uws = [_dot_unit_left(inv.astype(BF16), jnp.concatenate([p["vb"], p["kb"] * jnp.exp(p["gc"])], axis=1))
               for inv, p in zip(invs, probs)]
        for p, uw in zip(probs, uws):
            r0, hs, gc, h = p["r0"], p["hs"], p["gc"], p["h"]
            g_last = gc[CHUNK - 1:CHUNK, :]
            u_ref[pl.ds(r0, CHUNK), hs] = uw[:, :HEAD_DIM]
            w_ref[pl.ds(r0, CHUNK), hs] = uw[:, HEAD_DIM:].astype(BF16)
            qg_ref[pl.ds(r0, CHUNK), hs] = (p["q"] * jnp.exp(gc)).astype(BF16)
            kd_ref[pl.ds(r0, CHUNK), hs] = (p["k"] * jnp.exp(g_last - gc)).astype(BF16)
            qk_ref[pl.ds(r0, CHUNK), h * HEAD_DIM:h * HEAD_DIM + CHUNK] = p["qk"]
            el_ref[pl.ds(pl.multiple_of(p["c"] * SUBLANES, SUBLANES), SUBLANES), hs] = jnp.broadcast_to(
                jnp.exp(g_last), (SUBLANES, HEAD_DIM))
        return 0

    lax.fori_loop(0, ts // (CHUNK * GDN_CHUNKS_PER_STEP), prepare, 0)

    def scan(step, _):
        heads = [slice(h * HEAD_DIM, (h + 1) * HEAD_DIM) for h in range(N_HEADS)]
        states = [st_ref[h] for h in range(N_HEADS)]
        for sub in range(GDN_SCAN_CHUNKS_PER_STEP):
            c = step * GDN_SCAN_CHUNKS_PER_STEP + sub
            r0 = pl.multiple_of(c * CHUNK, CHUNK)
            s16 = [s.astype(BF16) for s in states]
            ws = [_dot(w_ref[pl.ds(r0, CHUNK), hs], s) for hs, s in zip(heads, s16)]
            inter = [_dot(qg_ref[pl.ds(r0, CHUNK), hs], s) for hs, s in zip(heads, s16)]
            v_new = [(u_ref[pl.ds(r0, CHUNK), hs] - x).astype(BF16) for hs, x in zip(heads, ws)]
            upd = [_dot_tn(kd_ref[pl.ds(r0, CHUNK), hs], vn) for hs, vn in zip(heads, v_new)]
            for h, hs in enumerate(heads):
                e_last = el_ref[pl.ds(pl.multiple_of(c * SUBLANES, SUBLANES), SUBLANES), hs][0:1, :]
                states[h] = states[h] * e_last + upd[h]
            intra = [_dot(qk_ref[pl.ds(r0, CHUNK), h * HEAD_DIM:h * HEAD_DIM + CHUNK], vn)
                     for h, vn in enumerate(v_new)]
            for h, hs in enumerate(heads):
                zg = z_ref[pl.ds(r0, CHUNK), hs].astype(F32)
                o_ref[pl.ds(r0, CHUNK), hs] = (
                    _rms(inter[h] + intra[h], gain_ref[...]) * _silu(zg)).astype(o_ref.dtype)
        for h in range(N_HEADS):
            st_ref[h] = states[h]
        return 0

    lax.fori_loop(0, ts // (CHUNK * GDN_SCAN_CHUNKS_PER_STEP), scan, 0)


def _gdn(proj, small, conv_w, head_params, out_gain, batch, seq, *, ts):
    nt = seq // ts
    kern = functools.partial(_gdn_kernel, ts=ts)
    qkv_blk = COL_GD // (3 * GROUP)
    z_blk = (COL_GD + 3 * GROUP) // GROUP
    return pl.pallas_call(
        kern,
        grid=(batch, nt),
        in_specs=[
            pl.BlockSpec((ts, 3 * GROUP), lambda b, t: (b * nt + t, qkv_blk)),
            pl.BlockSpec((ts, GROUP), lambda b, t: (b * nt + t, z_blk)),
            pl.BlockSpec((ts, SMALL_W), lambda b, t: (b * nt + t, 0)),
            pl.BlockSpec((CONV_TAPS, 3 * GROUP), lambda b, t: (0, 0)),
            pl.BlockSpec(head_params.shape, lambda b, t: (0, 0)),
            pl.BlockSpec((1, HEAD_DIM), lambda b, t: (0, 0)),
        ],
        out_specs=pl.BlockSpec((ts, GROUP), lambda b, t: (b * nt + t, 0)),
        out_shape=jax.ShapeDtypeStruct((batch * seq, GROUP), BF16),
        scratch_shapes=[
            pltpu.VMEM((ts + CONV_HALO, 3 * GROUP), BF16),
            pltpu.VMEM((ts, GROUP), F32),
            pltpu.VMEM((ts, GROUP), F32),
            pltpu.VMEM((ts, GROUP), F32),
            pltpu.VMEM((ts, GROUP), F32),
            pltpu.VMEM((ts, GROUP), BF16),
            pltpu.VMEM((ts, GROUP), BF16),
            pltpu.VMEM((ts, GROUP), BF16),
            pltpu.VMEM((ts, GROUP), BF16),
            pltpu.VMEM((ts // CHUNK * SUBLANES, GROUP), F32),
            pltpu.VMEM((N_HEADS, HEAD_DIM, HEAD_DIM), F32),
        ],
        compiler_params=_cparams("parallel", "arbitrary"),
        name="gated_deltanet",
    )(proj, proj, small, conv_w, head_params, out_gain.reshape(1, HEAD_DIM))


def _gla_kernel(qk_ref, v_ref, og_ref, sm_ref, w2_ref, gb_ref, gain_ref, o_ref,
                att_ref, qin_ref, kout_ref, el_ref, st_ref, *, ts):
    t = pl.program_id(1)

    @pl.when(t == 0)
    def _():
        st_ref[...] = jnp.zeros_like(st_ref)

    row, col = _chunk_iotas()
    lower_ones = jnp.where(row >= col, 1.0, 0.0).astype(BF16)
    eye = row == col
    row_w, lane = _chunk_iotas(HEAD_DIM)
    qk_scale = GLA_DK ** -0.5

    levels = []
    for level in range(CHUNK.bit_length() - 1):
        ref_row = ((row >> (level + 1)) << (level + 1)) + (1 << level)
        pick = jnp.where(col == ref_row, 1.0, 0.0).astype(BF16)
        upper_w = ((row_w >> level) & 1) == 1
        levels.append((pick, _lower_left_blocks(row, col, level), upper_w))

    n_pairs = N_HEADS * GLA_DK // HEAD_DIM
    heads_per_pair = HEAD_DIM // GLA_DK

    def prepare(step, _):
        cums = []
        for sub in range(GLA_CHUNKS_PER_STEP):
            c = step * GLA_CHUNKS_PER_STEP + sub
            r0 = pl.multiple_of(c * CHUNK, CHUNK)
            sm = sm_ref[pl.ds(r0, CHUNK), :].astype(BF16)
            gate = _dot(sm, w2_ref[...]) + gb_ref[...]
            log_a = (gate - _softplus(gate)) * (1.0 / GLA_TAU)
            cums.append((c, r0, _dot_sel(lower_ones, log_a)))
        probs = []
        for c, r0, b_all in cums:
            for pair in range(n_pairs):
                ps = slice(pair * HEAD_DIM, (pair + 1) * HEAD_DIM)
                b = b_all[:, ps]
                q2 = qk_ref[pl.ds(r0, CHUNK), ps].astype(F32) * qk_scale
                k2 = qk_ref[pl.ds(r0, CHUNK),
                            GROUP // 2 + pair * HEAD_DIM:GROUP // 2 + (pair + 1) * HEAD_DIM].astype(F32)
                probs.append(dict(c=c, r0=r0, pair=pair, ps=ps, b=b, q2=q2, k2=k2,
                                  b_refs=[_dot_sel(pick, b) for pick, _, _ in levels]))
        for p in probs:
            b, q2, k2 = p["b"], p["q2"], p["k2"]
            p["q_lv"] = [jnp.where(upper_w, q2 * jnp.exp(jnp.where(upper_w, b - b_ref, 0.0)), 0.0)
                         for (_, _, upper_w), b_ref in zip(levels, p["b_refs"])]
            p["k_lv"] = [jnp.where(upper_w, 0.0, k2 * jnp.exp(jnp.where(upper_w, 0.0, b_ref - b))).astype(BF16)
                         for (_, _, upper_w), b_ref in zip(levels, p["b_refs"])]
        for p in probs:
            r0, b, q2, k2 = p["r0"], p["b"], p["q2"], p["k2"]
            b_last = b[CHUNK - 1:CHUNK, :]
            q_in = (q2 * jnp.exp(b)).astype(BF16)
            k_out = (k2 * jnp.exp(b_last - b)).astype(BF16)
            diag_qk = q2 * k2
            for r in range(heads_per_pair):
                h = p["pair"] * heads_per_pair + r
                hs = slice(h * HEAD_DIM, (h + 1) * HEAD_DIM)
                mine = (lane >= r * GLA_DK) & (lane < (r + 1) * GLA_DK)
                att = jnp.where(eye, jnp.sum(jnp.where(mine, diag_qk, 0.0), axis=1, keepdims=True), 0.0)
                for (_, pair_mask, _), ql, kl in zip(levels, p["q_lv"], p["k_lv"]):
                    s = _dot_nt(jnp.where(mine, ql, 0.0).astype(BF16), kl)
                    att = att + jnp.where(pair_mask, s, 0.0)
                att_ref[pl.ds(r0, CHUNK), h * HEAD_DIM:h * HEAD_DIM + CHUNK] = att.astype(BF16)
                qin_ref[pl.ds(r0, CHUNK), hs] = jnp.where(mine, q_in, jnp.zeros_like(q_in))
                kout_ref[pl.ds(r0, CHUNK), hs] = jnp.where(mine, k_out, jnp.zeros_like(k_out))
            el_ref[pl.ds(pl.multiple_of(p["c"] * SUBLANES, SUBLANES), SUBLANES), p["ps"]] = jnp.broadcast_to(
                jnp.exp(b_last), (SUBLANES, HEAD_DIM))
        return 0

    lax.fori_loop(0, ts // (CHUNK * GLA_CHUNKS_PER_STEP), prepare, 0)

    def scan(step, _):
        heads = [slice(h * HEAD_DIM, (h + 1) * HEAD_DIM) for h in range(N_HEADS)]
        chunks = []
        for sub in range(GLA_SCAN_CHUNKS_PER_STEP):
            c = step * GLA_SCAN_CHUNKS_PER_STEP + sub
            r0 = pl.multiple_of(c * CHUNK, CHUNK)
            values = [v_ref[pl.ds(r0, CHUNK), hs] for hs in heads]
            intra = [_dot(att_ref[pl.ds(r0, CHUNK), h * HEAD_DIM:h * HEAD_DIM + CHUNK], v)
                     for h, v in enumerate(values)]
            upd = [_dot_tn(v, kout_ref[pl.ds(r0, CHUNK), hs]) for hs, v in zip(heads, values)]
            chunks.append((c, r0, intra, upd))
        states = [st_ref[h] for h in range(N_HEADS)]
        for c, r0, intra, upd in chunks:
            inter = [_dot_nt(qin_ref[pl.ds(r0, CHUNK), hs], s.astype(BF16)) for hs, s in zip(heads, states)]
            for h, hs in enumerate(heads):
                ps = slice((h // heads_per_pair) * HEAD_DIM, (h // heads_per_pair + 1) * HEAD_DIM)
                e_last = el_ref[pl.ds(pl.multiple_of(c * SUBLANES, SUBLANES), SUBLANES), ps][0:1, :]
                states[h] = states[h] * e_last + upd[h]
                og = og_ref[pl.ds(r0, CHUNK), hs].astype(F32)
                o_ref[pl.ds(r0, CHUNK), hs] = (
                    _rms(inter[h] + intra[h], gain_ref[...]) * _silu(og)).astype(o_ref.dtype)
        for h in range(N_HEADS):
            st_ref[h] = states[h]
        return 0

    lax.fori_loop(0, ts // (CHUNK * GLA_SCAN_CHUNKS_PER_STEP), scan, 0)


def _gla(proj, small, w2_padded, gate_bias, out_gain, batch, seq, *, ts):
    nt = seq // ts
    kern = functools.partial(_gla_kernel, ts=ts)
    qk_blk = COL_GL // GROUP
    return pl.pallas_call(
        kern,
        grid=(batch, nt),
        in_specs=[
            pl.BlockSpec((ts, GROUP), lambda b, t: (b * nt + t, qk_blk)),
            pl.BlockSpec((ts, GROUP), lambda b, t: (b * nt + t, qk_blk + 1)),
            pl.BlockSpec((ts, GROUP), lambda b, t: (b * nt + t, qk_blk + 2)),
            pl.BlockSpec((ts, SMALL_W), lambda b, t: (b * nt + t, 0)),
            pl.BlockSpec(w2_padded.shape, lambda b, t: (0, 0)),
            pl.BlockSpec((1, N_HEADS * GLA_DK), lambda b, t: (0, 0)),
            pl.BlockSpec((1, HEAD_DIM), lambda b, t: (0, 0)),
        ],
        out_specs=pl.BlockSpec((ts, GROUP), lambda b, t: (b * nt + t, 0)),
        out_shape=jax.ShapeDtypeStruct((batch * seq, GROUP), BF16),
        scratch_shapes=[
            pltpu.VMEM((ts, GROUP), BF16),
            pltpu.VMEM((ts, GROUP), BF16),
            pltpu.VMEM((ts, GROUP), BF16),
            pltpu.VMEM((ts // CHUNK * SUBLANES, N_HEADS * GLA_DK), F32),
            pltpu.VMEM((N_HEADS, HEAD_DIM, HEAD_DIM), F32),
        ],
        compiler_params=_cparams("parallel", "arbitrary"),
        name="gla",
    )(proj, proj, proj, small, w2_padded, gate_bias.reshape(1, -1), out_gain.reshape(1, HEAD_DIM))


def _out_proj_kernel(x_ref, m0_ref, m1_ref, m2_ref, m3_ref, w_ref, o_ref):
    acc = x_ref[...]
    for g, m_ref in enumerate((m0_ref, m1_ref, m2_ref, m3_ref)):
        acc = acc + _dot(m_ref[...], w_ref[g * GROUP:(g + 1) * GROUP, :])
    o_ref[...] = acc


def _out_proj(x, mixes, w_out, layer, *, tm):
    t, d = x.shape
    mix_spec = pl.BlockSpec((tm, GROUP), lambda i: (i, 0))
    return pl.pallas_call(
        _out_proj_kernel,
        grid=(t // tm,),
        in_specs=[pl.BlockSpec((tm, d), lambda i: (i, 0)), mix_spec, mix_spec, mix_spec, mix_spec,
                  pl.BlockSpec((None,) + w_out.shape[1:], lambda i: (layer, 0, 0))],
        out_specs=pl.BlockSpec((tm, d), lambda i: (i, 0)),
        out_shape=jax.ShapeDtypeStruct((t, d), F32),
        compiler_params=_cparams("parallel"),
        name="out_proj_residual",
    )(x, *mixes, w_out)


def _ffn_kernel(x_ref, g_ref, wg_ref, wu_ref, wd_ref, og_ref, o_ref, h_ref, *, norm_output):
    f = pl.program_id(1)

    def swiglu_tile(h):
        hid = _silu(_dot(h, wg_ref[...])) * _dot(h, wu_ref[...])
        return _dot(hid.astype(BF16), wd_ref[...])

    @pl.when(f == 0)
    def _():
        x = x_ref[...]
        h = _rms(x, g_ref[...]).astype(BF16)
        h_ref[...] = h
        o_ref[...] = x + swiglu_tile(h)

    @pl.when(f > 0)
    def _():
        o_ref[...] += swiglu_tile(h_ref[...])

    if norm_output:
        @pl.when(f == pl.num_programs(1) - 1)
        def _():
            o_ref[...] = _rms(o_ref[...], og_ref[...])


def _ffn(x, gain, w_gate, w_up, w_down, layer, out_gain, *, norm_output, tm, tf):
    t, d = x.shape
    hidden = w_gate.shape[2]
    return pl.pallas_call(
        functools.partial(_ffn_kernel, norm_output=norm_output),
        grid=(t // tm, hidden // tf),
        in_specs=[
            pl.BlockSpec((tm, d), lambda i, f: (i, 0)),
            pl.BlockSpec((1, d), lambda i, f: (0, 0)),
            pl.BlockSpec((None, d, tf), lambda i, f: (layer, 0, f)),
            pl.BlockSpec((None, d, tf), lambda i, f: (layer, 0, f)),
            pl.BlockSpec((None, tf, d), lambda i, f: (layer, f, 0)),
            pl.BlockSpec((1, d), lambda i, f: (0, 0)),
        ],
        out_specs=pl.BlockSpec((tm, d), lambda i, f: (i, 0)),
        out_shape=jax.ShapeDtypeStruct((t, d), F32),
        scratch_shapes=[pltpu.VMEM((tm, d), BF16)],
        compiler_params=_cparams("parallel", "arbitrary"),
        name="ffn_residual",
    )(x, gain.reshape(1, d), w_gate, w_up, w_down, out_gain.reshape(1, d))


def _pack_w_in(w_in):
    gd_small = COL_GD + 4 * GROUP
    df0 = gd_small + 2 * N_HEADS
    gl0 = df0 + 3 * GROUP
    rank0 = gl0 + 3 * GROUP
    pad = jnp.zeros(w_in.shape[:-1] + (IN_COLS_PACKED - COL_SMALL - 2 * N_HEADS - GLA_RANK,), BF16)
    parts = [w_in[..., :gd_small], w_in[..., df0:rank0], w_in[..., gd_small:df0], w_in[..., rank0:rank0 + GLA_RANK]]
    return jnp.concatenate([part.astype(BF16) for part in parts] + [pad], axis=-1)


def _mixer_layer(x, l, p, big, batch, seq):
    lam_init = 0.8 - 0.6 * math.exp(-0.3 * l)
    proj, small = _norm_in_proj(x, p["attn_norm"], big["w_in"], l, tm=IN_PROJ_TILE[0], tn=IN_PROJ_TILE[1])
    o_sb = _sb_attention(proj, batch, seq)
    head_params = jnp.zeros((SUBLANES, SMALL_W), F32)
    head_params = head_params.at[0, N_HEADS:2 * N_HEADS].set(p["gdn_a_log"])
    head_params = head_params.at[1, N_HEADS:2 * N_HEADS].set(p["gdn_dt_bias"])
    o_gd = _gdn(proj, small, p["gdn_conv_w"], head_params, p["gdn_out_norm"], batch, seq, ts=SCAN_TILE)
    lam_vecs = jnp.stack([p["diff_lam_q1"], p["diff_lam_k1"], p["diff_lam_q2"], p["diff_lam_k2"]])
    o_df = _diff_attention(proj, lam_vecs, p["diff_out_norm"], lam_init, batch, seq)
    w2 = jnp.zeros((SMALL_W, N_HEADS * GLA_DK), F32).at[2 * N_HEADS:2 * N_HEADS + GLA_RANK].set(p["gla_gate_w2"])
    o_gl = _gla(proj, small, w2.astype(BF16), p["gla_gate_b"], p["gla_out_norm"], batch, seq, ts=SCAN_TILE)
    return _out_proj(x, (o_sb, o_gd, o_df, o_gl), big["w_out"], l, tm=OUT_PROJ_ROWS)


def kernel(x, attn_norm, w_in, gdn_conv_w, gdn_a_log, gdn_dt_bias, gdn_out_norm, diff_lam_q1, diff_lam_k1,
           diff_lam_q2, diff_lam_k2, diff_out_norm, gla_gate_w2, gla_gate_b, gla_out_norm, w_out, ffn_norm,
           w_gate, w_up, w_down, final_norm):
    batch, seq, d = x.shape
    depth = w_in.shape[0]
    tokens = batch * seq
    assert seq % (2 * ATT_BLOCK) == 0 and seq % SCAN_TILE == 0, (seq, ATT_BLOCK, SCAN_TILE)
    assert tokens % IN_PROJ_TILE[0] == 0 and tokens % FFN_TILE[0] == 0 and tokens % OUT_PROJ_ROWS == 0, tokens
    assert w_gate.shape[2] % FFN_TILE[1] == 0, w_gate.shape
    stacked = dict(attn_norm=attn_norm, gdn_conv_w=gdn_conv_w, gdn_a_log=gdn_a_log,
                   gdn_dt_bias=gdn_dt_bias, gdn_out_norm=gdn_out_norm, diff_lam_q1=diff_lam_q1,
                   diff_lam_k1=diff_lam_k1, diff_lam_q2=diff_lam_q2, diff_lam_k2=diff_lam_k2,
                   diff_out_norm=diff_out_norm, gla_gate_w2=gla_gate_w2, gla_gate_b=gla_gate_b,
                   gla_out_norm=gla_out_norm)
    big = dict(w_in=_pack_w_in(w_in), w_out=w_out.astype(BF16), w_gate=w_gate.astype(BF16),
               w_up=w_up.astype(BF16), w_down=w_down.astype(BF16))
    xt = x.reshape(tokens, d)
    for l in range(depth):
        p = {name: arr[l] for name, arr in stacked.items()}
        xt = _mixer_layer(xt, l, p, big, batch, seq)
        xt = _ffn(xt, ffn_norm[l], big["w_gate"], big["w_up"], big["w_down"], l, final_norm,
                  norm_output=(l == depth - 1), tm=FFN_TILE[0], tf=FFN_TILE[1])
    return xt.reshape(batch, seq, d)
```
